```python
import jax, jax.numpy as jnp
from jax import lax
import numpy as np

D_MODEL = 2048
BATCH = 4
SEQ = 2048
DEPTH = 2

GRID_W = 64
CTX_LEN = 256
D_FF = 5632
D_A = 1024
GROUPS_A = 8
CHUNK_A = 128
D_B = 1024
HEADS_B = 8
HEAD_DK = D_B // HEADS_B
HEAD_DV = D_B // HEADS_B
CHUNK_B = 64
N_MOD = 9
N_NORM = 6
MACARON = 0.5
EPS = 1e-6
SPLIT_SIZES = (D_A, D_A, D_B, D_B, D_B, D_B, D_B, D_MODEL, D_MODEL)
IN_COLS = 2 * D_A + 5 * D_B + 2 * D_MODEL
OFF_F = 2 * D_A + D_B
OFF_G = 2 * D_A + 4 * D_B

kernel_name = 'hybrid_gmlp_hgrn2_dit'


def _rms(x, g):
    xf = x.astype(jnp.float32)
    y = xf * lax.rsqrt(jnp.mean(xf * xf, axis=-1, keepdims=True) + EPS)
    return (y * g.astype(jnp.float32)).astype(x.dtype)


def _layernorm(x, g):
    xf = x.astype(jnp.float32)
    xc = xf - jnp.mean(xf, axis=-1, keepdims=True)
    y = xc * lax.rsqrt(jnp.mean(xc * xc, axis=-1, keepdims=True) + EPS)
    return (y * g.astype(jnp.float32)).astype(x.dtype)


def _modulate(z, g_pre, shift, scale):
    return _rms(z, g_pre) * (1 + scale) + shift


def _residual(z, y, g_post, gate, weight):
    return z + weight * gate * _rms(y, g_post)


def _swiglu(h, w_gu, w_down):
    a, b = jnp.split(h @ w_gu, 2, axis=-1)
    return (jax.nn.silu(a) * b) @ w_down


def _split_cols(p):
    idx = [int(i) for i in np.cumsum(SPLIT_SIZES)[:-1]]
    return jnp.split(p, idx, axis=-1)


def _chunk_mlp(u, v, n_chunks, g_norm, w_s, b_s):
    b, n, _ = u.shape
    v = _layernorm(v, g_norm).reshape(b, n_chunks, CHUNK_A, GROUPS_A, D_A // GROUPS_A)
    sv = jnp.einsum('gts,bcsgd->bctgd', w_s, v) + b_s.T[None, None, :, :, None]
    return u * sv.reshape(b, n, D_A)


def _heads(t):
    b, n, _ = t.shape
    return jnp.transpose(t.reshape(b, n, HEADS_B, -1), (0, 2, 1, 3)).astype(jnp.float32)


def _decay(f_logit, lb):
    f = lb + (1.0 - lb) * jax.nn.sigmoid(f_logit.astype(jnp.float32))
    f = jnp.maximum(f, 1e-30)
    return _heads(1.0 - f), _heads(jnp.log(f))


def _hgrn_scan(q, k, v, logf, s0):
    b, h, n, _ = q.shape
    nc = n // CHUNK_B

    def to_chunks(t):
        return jnp.moveaxis(t.reshape(b, h, nc, CHUNK_B, t.shape[-1]), 2, 0)

    tri = jnp.tril(jnp.ones((CHUNK_B, CHUNK_B), dtype=bool))[:, :, None]

    def step(S, inp):
        qc, kc, vc, gc = inp
        cum = jnp.cumsum(gc, axis=2)
        inter = jnp.einsum('bhtd,bhde->bhte', qc * jnp.exp(cum), S)
        diff = cum[:, :, :, None, :] - cum[:, :, None, :, :]
        dec = jnp.where(tri, jnp.exp(jnp.where(tri, diff, 0.0)), 0.0)
        scores = jnp.einsum('bhtd,bhsd,bhtsd->bhts', qc, kc, dec)
        intra = jnp.einsum('bhts,bhse->bhte', scores, vc)
        last = cum[:, :, -1:, :]
        S_new = jnp.exp(last[:, :, 0, :])[..., None] * S + jnp.einsum('bhsd,bhse->bhde', kc * jnp.exp(last - cum), vc)
        return S_new, inter + intra

    s_fin, o = lax.scan(step, s0, (to_chunks(q), to_chunks(k), to_chunks(v), to_chunks(logf)))
    o = jnp.moveaxis(o, 0, 2).reshape(b, h, n, -1)
    return o, s_fin


def _final_state(k, v, logf):
    cum = jnp.cumsum(logf, axis=2)
    return jnp.einsum('bhsd,bhse->bhde', k * jnp.exp(cum[:, :, -1:, :] - cum), v)


def _readout(o, g_gate, gain):
    b, h, n, dv = o.shape
    o = jnp.transpose(o, (0, 2, 1, 3))
    o = o * lax.rsqrt(jnp.mean(o * o, axis=-1, keepdims=True) + EPS)
    o = o.reshape(b, n, h * dv) * gain.astype(jnp.float32)
    return o.astype(g_gate.dtype) * jax.nn.silu(g_gate)


def _merge(y_a, y_b, ga, gb, w_up_a, w_up_b, w_out):
    m = jax.nn.sigmoid(ga) * (y_a @ w_up_a) + jax.nn.sigmoid(gb) * (y_b @ w_up_b)
    return m @ w_out


def _mixer(h, hc, n_chunks, w_in, chunk_g, w_s, b_s, lb_f, lb_b, hgrn_g, w_up_a, w_up_b, w_out, need_ctx):
    b = h.shape[0]
    flip = lambda t: jnp.flip(t, axis=2)
    u, v, q, f_f, f_b, i, g, ga, gb = _split_cols(h @ w_in)
    if need_ctx:
        uc, vc, qc, f_fc, f_bc, ic, gc, gac, gbc = _split_cols(hc @ w_in)
    else:
        f_fc, f_bc, ic = jnp.split(hc @ w_in[:, OFF_F:OFF_G], 3, axis=-1)
    k_fc, lf_c = _decay(f_fc, lb_f)
    k_bc, lbw_c = _decay(f_bc, lb_b)
    v_c = _heads(ic)
    if need_ctx:
        q_c = _heads(qc)
        s0 = jnp.zeros((b, HEADS_B, HEAD_DK, HEAD_DV), jnp.float32)
        o_cf, s_cf = _hgrn_scan(q_c, k_fc, v_c, lf_c, s0)
        o_cb, s_cb = _hgrn_scan(flip(q_c), flip(k_bc), flip(v_c), flip(lbw_c), s0)
        y_bc = _readout(o_cf + flip(o_cb), gc, hgrn_g)
        y_ac = _chunk_mlp(jax.nn.gelu(uc), jax.nn.gelu(vc), hc.shape[1] // CHUNK_A, chunk_g, w_s, b_s)
        out_c = _merge(y_ac, y_bc, gac, gbc, w_up_a, w_up_b, w_out)
    else:
        s_cf = _final_state(k_fc, v_c, lf_c)
        s_cb = _final_state(flip(k_bc), flip(v_c), flip(lbw_c))
        out_c = None
    q_l = _heads(q)
    v_l = _heads(i)
    k_f, lf = _decay(f_f, lb_f)
    k_b, lbw = _decay(f_b, lb_b)
    o_f, _ = _hgrn_scan(q_l, k_f, v_l, lf, s_cf)
    o_b, _ = _hgrn_scan(flip(q_l), flip(k_b), flip(v_l), flip(lbw), s_cb)
    y_b = _readout(o_f + flip(o_b), g, hgrn_g)
    y_a = _chunk_mlp(jax.nn.gelu(u), jax.nn.gelu(v), n_chunks, chunk_g, w_s, b_s)
    out = _merge(y_a, y_b, ga, gb, w_up_a, w_up_b, w_out)
    return out, out_c


def setup_inputs(seed: int = 0) -> dict:
    key = jax.random.key(seed)
    ks = jax.random.split(key, 20)
    D = D_MODEL

    def nrm(k, shape, fan_in):
        return jax.random.normal(k, shape, jnp.float32) * (fan_in ** -0.5)

    def near_one(k, shape, s):
        return 1.0 + s * jax.random.normal(k, shape, jnp.float32)

    return {
        'x': jax.random.normal(ks[0], (BATCH, SEQ, D), jnp.float32),
        'c': jax.random.normal(ks[1], (BATCH, D), jnp.float32),
        'ctx': jax.random.normal(ks[2], (BATCH, CTX_LEN, D), jnp.float32),
        'c_ctx': jax.random.normal(ks[3], (D,), jnp.float32),
        'w_mod': nrm(ks[4], (DEPTH, D, N_MOD * D), D),
        'b_mod': 0.01 * jax.random.normal(ks[5], (DEPTH, N_MOD * D), jnp.float32),
        'norm_g': near_one(ks[6], (DEPTH, N_NORM, D), 0.05),
        'ffn1_w_gu': nrm(ks[7], (DEPTH, D, 2 * D_FF), D),
        'ffn1_w_down': nrm(ks[8], (DEPTH, D_FF, D), D_FF),
        'ffn2_w_gu': nrm(ks[9], (DEPTH, D, 2 * D_FF), D),
        'ffn2_w_down': nrm(ks[10], (DEPTH, D_FF, D), D_FF),
        'w_in': nrm(ks[11], (DEPTH, D, IN_COLS), D),
        'chunk_norm_g': near_one(ks[12], (DEPTH, D_A), 0.05),
        'w_spatial': nrm(ks[13], (DEPTH, GROUPS_A, CHUNK_A, CHUNK_A), CHUNK_A),
        'b_spatial': near_one(ks[14], (DEPTH, GROUPS_A, CHUNK_A), 0.02),
        'lb_logits': jax.random.normal(ks[15], (DEPTH, 2, D_B), jnp.float32),
        'hgrn_norm_g': near_one(ks[16], (DEPTH, D_B), 0.05),
        'w_up_a': nrm(ks[17], (DEPTH, D_A, D), D_A),
        'w_up_b': nrm(ks[18], (DEPTH, D_B, D), D_B),
        'w_out': nrm(ks[19], (DEPTH, D, D), D),
    }


def reference(x, c, ctx, c_ctx, w_mod, b_mod, norm_g, ffn1_w_gu, ffn1_w_down, ffn2_w_gu, ffn2_w_down,
              w_in, chunk_norm_g, w_spatial, b_spatial, lb_logits, hgrn_norm_g, w_up_a, w_up_b, w_out):
    n = x.shape[1]
    rows = n // GRID_W
    n_chunks = rows // (CHUNK_A // GRID_W)
    lb_all = jnp.cumsum(jax.nn.softmax(lb_logits.astype(jnp.float32), axis=0), axis=0)
    lb_all = lb_all - lb_all[0:1]
    sc = jax.nn.silu(c)
    scc = jax.nn.silu(c_ctx)
    xc = ctx
    for l in range(DEPTH):
        last = l == DEPTH - 1
        mod = (sc @ w_mod[l] + b_mod[l]).reshape(x.shape[0], 1, N_MOD, D_MODEL)
        modc = (scc @ w_mod[l] + b_mod[l]).reshape(1, 1, N_MOD, D_MODEL)
        g = norm_g[l]
        h = _modulate(x, g[0], mod[:, :, 0], mod[:, :, 1])
        x = _residual(x, _swiglu(h, ffn1_w_gu[l], ffn1_w_down[l]), g[1], mod[:, :, 2], MACARON)
        hc = _modulate(xc, g[0], modc[:, :, 0], modc[:, :, 1])
        xc = _residual(xc, _swiglu(hc, ffn1_w_gu[l], ffn1_w_down[l]), g[1], modc[:, :, 2], MACARON)
        h = _modulate(x, g[2], mod[:, :, 3], mod[:, :, 4])
        hc = _modulate(xc, g[2], modc[:, :, 3], modc[:, :, 4])
        y, yc = _mixer(h, hc, n_chunks, w_in[l], chunk_norm_g[l], w_spatial[l], b_spatial[l],
                       lb_all[l, 0], lb_all[l, 1], hgrn_norm_g[l], w_up_a[l], w_up_b[l], w_out[l],
                       not last)
        x = _residual(x, y, g[3], mod[:, :, 5], 1.0)
        h = _modulate(x, g[4], mod[:, :, 6], mod[:, :, 7])
        x = _residual(x, _swiglu(h, ffn2_w_gu[l], ffn2_w_down[l]), g[5], mod[:, :, 8], MACARON)
        if not last:
            xc = _residual(xc, yc, g[3], modc[:, :, 5], 1.0)
            hc = _modulate(xc, g[4], modc[:, :, 6], modc[:, :, 7])
            xc = _residual(xc, _swiglu(hc, ffn2_w_gu[l], ffn2_w_down[l]), g[5], modc[:, :, 8], MACARON)
    return x
```

```python
import functools

import numpy as np
import jax
import jax.numpy as jnp
from jax import lax
from jax.experimental import pallas as pl
from jax.experimental.pallas import tpu as pltpu

BF = jnp.bfloat16
F32 = jnp.float32

EPS = 1e-6
MACARON = 0.5
F_FLOOR = 1e-30
N_MOD = 9
GROUPS_A = 8
CHUNK_A = 128
HEADS_B = 8
HGRN_CHUNK = 64
HGRN_LEVELS = (2, 4, 8, 16, 32, 64)

VMEM_LIMIT = 56 * 1024 * 1024


def _dot(a, b):
    return jnp.dot(a, b, preferred_element_type=F32)


def _dot_nt(a, b):
    return lax.dot_general(a, b, (((1,), (1,)), ((), ())), preferred_element_type=F32)


def _dot_tn(a, b):
    return lax.dot_general(a, b, (((0,), (0,)), ((), ())), preferred_element_type=F32)


def _rms(x, g):
    return x * lax.rsqrt(jnp.mean(x * x, axis=-1, keepdims=True) + EPS) * g


def _params(sem):
    return pltpu.CompilerParams(dimension_semantics=sem, vmem_limit_bytes=VMEM_LIMIT)


def _mod_kernel(cs_ref, w_ref, b_ref, o_ref):
    cs = cs_ref[...]
    s = (cs * jax.nn.sigmoid(cs)).astype(BF)
    o_ref[...] = _dot(s, w_ref[...].astype(BF)) + b_ref[...]


def _mod_call(cs, w_mod, b_mod):
    depth, d, n = w_mod.shape
    tn = 1024
    return pl.pallas_call(
        _mod_kernel,
        grid=(depth, n // tn),
        in_specs=[
            pl.BlockSpec((8, d), lambda l, j: (0, 0)),
            pl.BlockSpec((None, d, tn), lambda l, j: (l, 0, j)),
            pl.BlockSpec((None, 1, tn), lambda l, j: (l, 0, j)),
        ],
        out_specs=pl.BlockSpec((None, 8, tn), lambda l, j: (l, 0, j)),
        out_shape=jax.ShapeDtypeStruct((depth, 8, n), F32),
        compiler_params=_params(("parallel", "parallel")),
        name="mod",
    )(cs, w_mod, b_mod.reshape(depth, 1, n))


def _mod_row(row_div, row_const):
    if row_div is None:
        return row_const
    return pl.program_id(0) // row_div


def _ffn_kernel(x_ref, sh_ref, sc_ref, gt_ref, gpre_ref, gpost_ref, wg_ref, wu_ref, wd_ref,
                o_ref, h_scr, *, row_div, row_const, n_f):
    j = pl.program_id(1)
    row = _mod_row(row_div, row_const)

    @pl.when(j == 0)
    def _():
        y = _rms(x_ref[...], gpre_ref[...])
        h = y * (1.0 + sc_ref[pl.ds(row, 1), :]) + sh_ref[pl.ds(row, 1), :]
        h_scr[...] = h.astype(BF)
        o_ref[...] = jnp.zeros_like(o_ref)

    h = h_scr[...]
    a = _dot(h, wg_ref[...].astype(BF))
    b = _dot(h, wu_ref[...].astype(BF))
    act = (a * jax.nn.sigmoid(a) * b).astype(BF)
    o_ref[...] += _dot(act, wd_ref[...].astype(BF))

    @pl.when(j == n_f - 1)
    def _():
        r = _rms(o_ref[...], gpost_ref[...])
        o_ref[...] = x_ref[...] + MACARON * gt_ref[pl.ds(row, 1), :] * r


def _ffn_call(x, mod_l, mod_k, gpre, gpost, w_gu, w_down, l, *, tm, tf, row_div, row_const):
    m, d = x.shape
    d_ff = w_down.shape[1]
    n_f = d_ff // tf
    kern = functools.partial(_ffn_kernel, row_div=row_div, row_const=row_const, n_f=n_f)
    mod_spec = lambda k: pl.BlockSpec((8, d), lambda i, j: (0, k))
    return pl.pallas_call(
        kern,
        grid=(m // tm, n_f),
        in_specs=[
            pl.BlockSpec((tm, d), lambda i, j: (i, 0), pipeline_mode=pl.Buffered(1)),
            mod_spec(mod_k), mod_spec(mod_k + 1), mod_spec(mod_k + 2),
            pl.BlockSpec((1, d), lambda i, j: (0, 0)),
            pl.BlockSpec((1, d), lambda i, j: (0, 0)),
            pl.BlockSpec((None, d, tf), lambda i, j: (l, 0, j)),
            pl.BlockSpec((None, d, tf), lambda i, j: (l, 0, n_f + j)),
            pl.BlockSpec((None, tf, d), lambda i, j: (l, j, 0)),
        ],
        out_specs=pl.BlockSpec((tm, d), lambda i, j: (i, 0)),
        out_shape=jax.ShapeDtypeStruct((m, d), F32),
        scratch_shapes=[pltpu.VMEM((tm, d), BF)],
        compiler_params=_params(("parallel", "arbitrary")),
        name="ffn",
    )(x, mod_l, mod_l, mod_l, gpre, gpost, w_gu, w_gu, w_down)


WIN_TN = 512


def _gelu_tanh(x):
    return 0.5 * x * (1.0 + jnp.tanh(0.7978845608028654 * (x + 0.044715 * x * x * x)))


def _win_kernel(x_ref, sh_ref, sc_ref, gpre_ref, w_ref, lb_ref, p16_ref, pf_ref, h_scr,
                *, row_div, row_const, kinds, layer):
    j = pl.program_id(1)
    row = _mod_row(row_div, row_const)

    @pl.when(j == 0)
    def _():
        y = _rms(x_ref[...], gpre_ref[...])
        h = y * (1.0 + sc_ref[pl.ds(row, 1), :]) + sh_ref[pl.ds(row, 1), :]
        h_scr[...] = h.astype(BF)

    z = _dot(h_scr[...], w_ref[...].astype(BF))

    def in_kind(kind):
        cond = None
        t = 0
        while t < len(kinds):
            t1 = t
            while t1 + 1 < len(kinds) and kinds[t1 + 1] == kinds[t]:
                t1 += 1
            if kinds[t] == kind:
                run = (j >= t) & (j <= t1)
                cond = run if cond is None else cond | run
            t = t1 + 1
        return cond

    for kind, fn in (("gelu", _gelu_tanh), ("id", lambda t: t),
                     ("silu", lambda t: t * jax.nn.sigmoid(t)), ("sigmoid", jax.nn.sigmoid)):
        cond = in_kind(kind)
        if cond is not None:
            @pl.when(cond)
            def _(fn=fn):
                p16_ref[...] = fn(z).astype(BF)

    cond = in_kind("logf")
    if cond is not None:
        @pl.when(cond)
        def _():
            ll = lb_ref[...]
            e = jnp.exp(ll - jnp.max(ll, axis=0, keepdims=True))
            p = e / jnp.sum(e, axis=0, keepdims=True)
            lb = jnp.zeros_like(p[0:1])
            for r in range(1, layer + 1):
                lb = lb + p[r:r + 1]
            f = lb + (1.0 - lb) * jax.nn.sigmoid(z)
            pf_ref[...] = jnp.log(jnp.maximum(f, F_FLOOR))


def _win_call(x, mod_l, mod_k, gpre, w_in, lb_logits2, l, *, tm, row_div, row_const, col0, kinds):
    m, d = x.shape
    tn = WIN_TN
    nj = len(kinds)
    is_f = np.array([k == "logf" for k in kinds])
    n16, nf = int((~is_f).sum()), int(is_f.sum())
    c16 = np.maximum(np.cumsum(~is_f) - 1, 0)
    cf = np.maximum(np.cumsum(is_f) - 1, 0)
    f0 = int(np.argmax(is_f))

    def sel(table):
        def f(j):
            out = jnp.int32(int(table[0]))
            for t in range(1, nj):
                if table[t] != table[t - 1]:
                    out = jnp.where(j >= t, jnp.int32(int(table[t])), out)
            return out
        return f

    s16, sf = sel(c16), sel(cf)
    kern = functools.partial(_win_kernel, row_div=row_div, row_const=row_const, kinds=tuple(kinds), layer=l)
    mod_spec = lambda k: pl.BlockSpec((8, d), lambda i, j: (0, k))
    depth = lb_logits2.shape[0]
    return pl.pallas_call(
        kern,
        grid=(m // tm, nj),
        in_specs=[
            pl.BlockSpec((tm, d), lambda i, j: (i, 0)),
            mod_spec(mod_k), mod_spec(mod_k + 1),
            pl.BlockSpec((1, d), lambda i, j: (0, 0)),
            pl.BlockSpec((None, d, tn), lambda i, j: (l, 0, col0 + j)),
            pl.BlockSpec((depth, tn), lambda i, j: (0, jnp.clip(j - f0, 0, nf - 1))),
        ],
        out_specs=[
            pl.BlockSpec((tm, tn), lambda i, j: (i, s16(j))),
            pl.BlockSpec((tm, tn), lambda i, j: (i, sf(j))),
        ],
        out_shape=[jax.ShapeDtypeStruct((m, n16 * tn), BF), jax.ShapeDtypeStruct((m, nf * tn), F32)],
        scratch_shapes=[pltpu.VMEM((tm, d), BF)],
        compiler_params=_params(("parallel", "arbitrary")),
        name="win",
    )(x, mod_l, mod_l, gpre, w_in, lb_logits2)


def _cmlp_kernel(u_ref, v_ref, g_ref, ws_ref, bs_ref, o_ref, *, n_chunks):
    v = v_ref[...].astype(F32)
    vc = v - jnp.mean(v, axis=-1, keepdims=True)
    vn = (vc * lax.rsqrt(jnp.mean(vc * vc, axis=-1, keepdims=True) + EPS) * g_ref[...]).astype(BF)
    dg = vn.shape[1] // GROUPS_A
    for g in range(GROUPS_A):
        w = ws_ref[g].astype(BF)
        bias = bs_ref[g]
        for c in range(n_chunks):
            rows = slice(c * CHUNK_A, (c + 1) * CHUNK_A)
            cols = slice(g * dg, (g + 1) * dg)
            sv = _dot(w, vn[rows, cols]) + bias
            o_ref[rows, cols] = (u_ref[rows, cols].astype(F32) * sv).astype(BF)


def _cmlp_call(p16, chunk_g, w_s, b_s, d_a, *, tm):
    m = p16.shape[0]
    n_chunks = tm // CHUNK_A
    return pl.pallas_call(
        functools.partial(_cmlp_kernel, n_chunks=n_chunks),
        grid=(m // tm,),
        in_specs=[
            pl.BlockSpec((tm, d_a), lambda i: (i, 0)),
            pl.BlockSpec((tm, d_a), lambda i: (i, 1)),
            pl.BlockSpec((1, d_a), lambda i: (0, 0)),
            pl.BlockSpec((GROUPS_A, CHUNK_A, CHUNK_A), lambda i: (0, 0, 0)),
            pl.BlockSpec((GROUPS_A, CHUNK_A, 1), lambda i: (0, 0, 0)),
        ],
        out_specs=pl.BlockSpec((tm, d_a), lambda i: (i, 0)),
        out_shape=jax.ShapeDtypeStruct((m, d_a), BF),
        compiler_params=_params(("parallel",)),
        name="cmlp",
    )(p16, p16, chunk_g, w_s, b_s[..., None])


def _hgrn_consts(fwd):
    c = HGRN_CHUNK
    t = np.arange(c)[:, None]
    u = np.arange(c)[None, :]
    mats = [u <= t, u > t] if fwd else [u >= t, u < t]
    masks = [t == u]
    for b in HGRN_LEVELS:
        half = b // 2
        mid = (t // b) * b + half
        upper = (t % b) >= half
        if fwd:
            mats.append(np.where(upper, (u >= mid) & (u <= t), (u > t) & (u < mid)))
            masks.append((t // b == u // b) & upper & ((u % b) < half))
        else:
            mats.append(np.where(upper, (u >= mid) & (u < t), (u >= t) & (u < mid)))
            masks.append((t // b == u // b) & ~upper & ((u % b) >= half))
    return (np.concatenate(mats, 0).astype(np.float32), np.stack(masks).astype(np.float32))


def _hgrn_chunk(q, v, g, st, l_ref, m_ref, total_row):
    c = HGRN_CHUNK
    dk = g.shape[1]
    g_hi = g.astype(BF)
    g_lo = (g - g_hi.astype(F32)).astype(BF)
    sums = _dot(l_ref[...], jnp.concatenate([g_hi, g_lo], axis=1))
    e = jnp.exp(sums[:, :dk] + sums[:, dk:])
    k = 1.0 - jnp.exp(g)
    qf = q.astype(F32)
    a = m_ref[0] * _dot_nt(q, k.astype(BF))
    for lv in range(len(HGRN_LEVELS)):
        el = e[(2 + lv) * c:(3 + lv) * c]
        a = a + m_ref[lv + 1] * _dot_nt((qf * el).astype(BF), (k * el).astype(BF))
    o = _dot(a.astype(BF), v) + _dot_nt((qf * e[0:c]).astype(BF), st.astype(BF))
    k_end = (k * e[c:2 * c]).astype(BF)
    st_new = st * e[total_row:total_row + 1] + _dot_tn(v, k_end)
    return o, st_new


def _hgrn_kernel(*refs, n, with_out, with_init, with_final):
    it = iter(refs)
    q_ref = next(it) if with_out else None
    v_ref = next(it)
    gs_ref = next(it) if with_out else None
    gf_ref, gb_ref = next(it), next(it)
    gain_ref = next(it) if with_out else None
    sf0_ref, sb0_ref = (next(it), next(it)) if with_init else (None, None)
    lf_ref, mf_ref, lbw_ref, mb_ref = next(it), next(it), next(it), next(it)
    y_ref = next(it) if with_out else None
    sf_ref, sb_ref = (next(it), next(it)) if with_final else (None, None)
    of_scr, ob_scr = (next(it), next(it)) if with_out else (None, None)

    c = HGRN_CHUNK
    nc = n // c
    dk = v_ref.shape[1]
    zero_q = jnp.zeros((c, dk), BF)

    def body(ci, carry):
        st_f, st_b = carry
        rf = pl.ds(pl.multiple_of(ci * c, c), c)
        rb = pl.ds(pl.multiple_of((nc - 1 - ci) * c, c), c)
        qf = q_ref[rf, :] if with_out else zero_q
        qb = q_ref[rb, :] if with_out else zero_q
        o_f, st_f = _hgrn_chunk(qf, v_ref[rf, :], gf_ref[rf, :], st_f, lf_ref, mf_ref, c - 1)
        o_b, st_b = _hgrn_chunk(qb, v_ref[rb, :], gb_ref[rb, :], st_b, lbw_ref, mb_ref, 0)
        if with_out:
            of_scr[rf, :] = o_f
            ob_scr[rb, :] = o_b
        return st_f, st_b

    if with_init:
        init = (sf0_ref[...], sb0_ref[...])
    else:
        init = (jnp.zeros((dk, dk), F32), jnp.zeros((dk, dk), F32))
    st_f, st_b = lax.fori_loop(0, nc, body, init)
    if with_final:
        sf_ref[...] = st_f
        sb_ref[...] = st_b

    if with_out:
        rows = 256

        def readout(ri, _):
            r = pl.ds(pl.multiple_of(ri * rows, rows), rows)
            o = of_scr[r, :] + ob_scr[r, :]
            o = o * lax.rsqrt(jnp.mean(o * o, axis=-1, keepdims=True) + EPS) * gain_ref[...]
            y_ref[r, :] = (o * gs_ref[r, :].astype(F32)).astype(BF)
            return 0

        lax.fori_loop(0, n // rows, readout, 0)


def _hgrn_call(p16, pf, gain, init, n, cols, *, with_out, with_final):
    m = p16.shape[0]
    bsz = m // n
    dk = pf.shape[1] // (2 * HEADS_B)
    with_init = init is not None
    cq, cv, cg = cols
    tok = lambda col: pl.BlockSpec((n, dk), lambda b, h: (b, col + h))
    st_spec = pl.BlockSpec((None, None, dk, dk), lambda b, h: (b, h, 0, 0))
    full = lambda a: pl.BlockSpec(a.shape, lambda b, h: (0,) * a.ndim)

    lf, mf = _hgrn_consts(True)
    lbw, mb = _hgrn_consts(False)
    consts = [jnp.asarray(lf, BF), jnp.asarray(mf, F32), jnp.asarray(lbw, BF), jnp.asarray(mb, F32)]

    args, specs = [], []
    if with_out:
        args.append(p16); specs.append(tok(cq))
    args.append(p16); specs.append(tok(cv))
    if with_out:
        args.append(p16); specs.append(tok(cg))
    args += [pf, pf]; specs += [tok(0), tok(HEADS_B)]
    if with_out:
        args.append(gain); specs.append(pl.BlockSpec((1, dk), lambda b, h: (0, h)))
    if with_init:
        args += list(init); specs += [st_spec, st_spec]
    args += consts; specs += [full(a) for a in consts]

    out_shape, out_specs = [], []
    if with_out:
        out_shape.append(jax.ShapeDtypeStruct((m, HEADS_B * dk), BF)); out_specs.append(tok(0))
    if with_final:
        st_shape = jax.ShapeDtypeStruct((bsz, HEADS_B, dk, dk), F32)
        out_shape += [st_shape, st_shape]; out_specs += [st_spec, st_spec]
    scratch = [pltpu.VMEM((n, dk), F32), pltpu.VMEM((n, dk), F32)] if with_out else []

    kern = functools.partial(_hgrn_kernel, n=n, with_out=with_out, with_init=with_init, with_final=with_final)
    return pl.pallas_call(
        kern,
        grid=(bsz, HEADS_B),
        in_specs=specs,
        out_specs=out_specs,
        out_shape=out_shape,
        scratch_shapes=scratch,
        compiler_params=_params(("parallel", "parallel")),
        name="hgrn",
    )(*args)


def _merge_kernel(ya_ref, yb_ref, ga_ref, gb_ref, wa_ref, wb_ref, wo_ref, x_ref, gt_ref, gpost_ref,
                  o_ref, *, row_div, row_const, n_j):
    j = pl.program_id(1)
    row = _mod_row(row_div, row_const)
    ma = _dot(ya_ref[...], wa_ref[...].astype(BF))
    mb = _dot(yb_ref[...], wb_ref[...].astype(BF))
    mm = (ga_ref[...].astype(F32) * ma + gb_ref[...].astype(F32) * mb).astype(BF)

    @pl.when(j == 0)
    def _():
        o_ref[...] = jnp.zeros_like(o_ref)

    o_ref[...] += _dot(mm, wo_ref[...].astype(BF))

    @pl.when(j == n_j - 1)
    def _():
        r = _rms(o_ref[...], gpost_ref[...])
        o_ref[...] = x_ref[...] + gt_ref[pl.ds(row, 1), :] * r


def _merge_call(x, ya, yb, p16, gate_cols, mod_l, mod_k, gpost, w_up_a, w_up_b, w_out, l,
                *, tm, row_div, row_const):
    m, d = x.shape
    da, db = ya.shape[1], yb.shape[1]
    tn = WIN_TN
    n_j = d // tn
    ca, cb = gate_cols
    kern = functools.partial(_merge_kernel, row_div=row_div, row_const=row_const, n_j=n_j)
    return pl.pallas_call(
        kern,
        grid=(m // tm, n_j),
        in_specs=[
            pl.BlockSpec((tm, da), lambda i, j: (i, 0)),
            pl.BlockSpec((tm, db), lambda i, j: (i, 0)),
            pl.BlockSpec((tm, tn), lambda i, j: (i, ca + j)),
            pl.BlockSpec((tm, tn), lambda i, j: (i, cb + j)),
            pl.BlockSpec((None, da, tn), lambda i, j: (l, 0, j)),
            pl.BlockSpec((None, db, tn), lambda i, j: (l, 0, j)),
            pl.BlockSpec((None, tn, d), lambda i, j: (l, j, 0)),
            pl.BlockSpec((tm, d), lambda i, j: (i, 0)),
            pl.BlockSpec((8, d), lambda i, j: (0, mod_k)),
            pl.BlockSpec((1, d), lambda i, j: (0, 0)),
        ],
        out_specs=pl.BlockSpec((tm, d), lambda i, j: (i, 0)),
        out_shape=jax.ShapeDtypeStruct((m, d), F32),
        compiler_params=_params(("parallel", "arbitrary")),
        name="merge",
    )(ya, yb, p16, p16, w_up_a, w_up_b, w_out, x, mod_l, gpost)


_KINDS_FULL = (["gelu"] * 4 + ["id"] * 2 + ["logf"] * 4 + ["id"] * 2 + ["silu"] * 2 + ["sigmoid"] * 8)
_KINDS_STATE = ["logf"] * 4 + ["id"] * 2


def kernel(x, c, ctx, c_ctx, w_mod, b_mod, norm_g, ffn1_w_gu, ffn1_w_down, ffn2_w_gu, ffn2_w_down,
           w_in, chunk_norm_g, w_spatial, b_spatial, lb_logits, hgrn_norm_g, w_up_a, w_up_b, w_out):
    bsz, n, d = x.shape
    n_ctx = ctx.shape[1]
    depth = w_mod.shape[0]
    d_a = chunk_norm_g.shape[1]
    d_b = hgrn_norm_g.shape[1]
    ctx_row = bsz

    cs = jnp.concatenate([c, c_ctx[None, :], jnp.zeros((8 - bsz - 1, d), F32)], axis=0)
    mod = _mod_call(cs, w_mod, b_mod)
    lb2 = lb_logits.reshape(depth, -1)

    xl = x.reshape(bsz * n, d)
    xc = ctx.reshape(bsz * n_ctx, d)
    tm = 1024
    lat = dict(tm=tm, row_div=n // tm, row_const=None)
    cx = dict(tm=bsz * n_ctx, row_div=None, row_const=ctx_row)
    lat_m = dict(tm=512, row_div=n // 512, row_const=None)
    cx_m = dict(tm=512, row_div=None, row_const=ctx_row)
    col_q, col_i, col_g = 2 * d_a // 128, (2 * d_a + d_b) // 128, (2 * d_a + 2 * d_b) // 128
    gate_cols = ((2 * d_a + 3 * d_b) // WIN_TN, (2 * d_a + 3 * d_b + d) // WIN_TN)

    for l in range(depth):
        last = l == depth - 1
        ml = mod[l]
        g = [norm_g[l, k][None, :] for k in range(norm_g.shape[1])]
        ffn1 = functools.partial(_ffn_call, mod_l=ml, mod_k=0, gpre=g[0], gpost=g[1],
                                 w_gu=ffn1_w_gu, w_down=ffn1_w_down, l=l, tf=256)
        ffn2 = functools.partial(_ffn_call, mod_l=ml, mod_k=6, gpre=g[4], gpost=g[5],
                                 w_gu=ffn2_w_gu, w_down=ffn2_w_down, l=l, tf=256)
        xl = ffn1(xl, **lat)
        xc = ffn1(xc, **cx)

        if not last:
            pc16, pcf = _win_call(xc, ml, 3, g[2], w_in, lb2, l, col0=0, kinds=_KINDS_FULL, **cx)
            ybc, s_f, s_b = _hgrn_call(pc16, pcf, hgrn_norm_g[l][None, :], None, n_ctx,
                                       (col_q, col_i, col_g), with_out=True, with_final=True)
        else:
            pc16, pcf = _win_call(xc, ml, 3, g[2], w_in, lb2, l, col0=(2 * d_a + d_b) // WIN_TN,
                                  kinds=_KINDS_STATE, **cx)
            s_f, s_b = _hgrn_call(pc16, pcf, None, None, n_ctx, (0, 0, 0), with_out=False, with_final=True)

        p16, pf = _win_call(xl, ml, 3, g[2], w_in, lb2, l, col0=0, kinds=_KINDS_FULL, **lat)
        (yb,) = _hgrn_call(p16, pf, hgrn_norm_g[l][None, :], (s_f, s_b), n,
                           (col_q, col_i, col_g), with_out=True, with_final=False)
        ya = _cmlp_call(p16, chunk_norm_g[l][None, :], w_spatial[l], b_spatial[l], d_a, tm=512)
        xl = _merge_call(xl, ya, yb, p16, gate_cols, ml, 5, g[3], w_up_a, w_up_b, w_out, l, **lat_m)
        xl = ffn2(xl, **lat)

        if not last:
            yac = _cmlp_call(pc16, chunk_norm_g[l][None, :], w_spatial[l], b_spatial[l], d_a, tm=512)
            xc = _merge_call(xc, yac, ybc, pc16, gate_cols, ml, 5, g[3], w_up_a, w_up_b, w_out, l, **cx_m)
            xc = ffn2(xc, **cx)
    return xl.reshape(bsz, n, d)
```

```python
import functools

import numpy as np
import jax
import jax.numpy as jnp
from jax import lax
from jax.experimental import pallas as pl
from jax.experimental.pallas import tpu as pltpu

BF = jnp.bfloat16
F32 = jnp.float32

EPS = 1e-6
MACARON = 0.5
F_FLOOR = 1e-30
N_MOD = 9
GROUPS_A = 8
CHUNK_A = 128
HEADS_B = 8
HGRN_CHUNK = 64
HGRN_LEVELS = (2, 4, 8, 16, 32, 64)

VMEM_LIMIT = 56 * 1024 * 1024


def _dot(a, b):
    return jnp.dot(a, b, preferred_element_type=F32)


def _dot_nt(a, b):
    return lax.dot_general(a, b, (((1,), (1,)), ((), ())), preferred_element_type=F32)


def _dot_tn(a, b):
    return lax.dot_general(a, b, (((0,), (0,)), ((), ())), preferred_element_type=F32)


def _rms(x, g):
    return x * lax.rsqrt(jnp.mean(x * x, axis=-1, keepdims=True) + EPS) * g


def _params(sem):
    return pltpu.CompilerParams(dimension_semantics=sem, vmem_limit_bytes=VMEM_LIMIT)


def _mod_kernel(cs_ref, w_ref, b_ref, o_ref):
    cs = cs_ref[...]
    s = (cs * jax.nn.sigmoid(cs)).astype(BF)
    o_ref[...] = _dot(s, w_ref[...].astype(BF)) + b_ref[...]


def _mod_call(cs, w_mod, b_mod):
    depth, d, n = w_mod.shape
    tn = 1024
    return pl.pallas_call(
        _mod_kernel,
        grid=(depth, n // tn),
        in_specs=[
            pl.BlockSpec((8, d), lambda l, j: (0, 0)),
            pl.BlockSpec((None, d, tn), lambda l, j: (l, 0, j)),
            pl.BlockSpec((None, 1, tn), lambda l, j: (l, 0, j)),
        ],
        out_specs=pl.BlockSpec((None, 8, tn), lambda l, j: (l, 0, j)),
        out_shape=jax.ShapeDtypeStruct((depth, 8, n), F32),
        compiler_params=_params(("parallel", "parallel")),
        name="mod",
    )(cs, w_mod, b_mod.reshape(depth, 1, n))


def _mod_row(row_div, row_const):
    if row_div is None:
        return row_const
    return pl.program_id(0) // row_div


def _ffn_kernel(x_ref, sh_ref, sc_ref, gt_ref, gpre_ref, gpost_ref, wg_ref, wu_ref, wd_ref,
                o_ref, h_scr, *, row_div, row_const, n_f):
    j = pl.program_id(1)
    row = _mod_row(row_div, row_const)

    @pl.when(j == 0)
    def _():
        y = _rms(x_ref[...], gpre_ref[...])
        h = y * (1.0 + sc_ref[pl.ds(row, 1), :]) + sh_ref[pl.ds(row, 1), :]
        h_scr[...] = h.astype(BF)
        o_ref[...] = jnp.zeros_like(o_ref)

    h = h_scr[...]
    a = _dot(h, wg_ref[...].astype(BF))
    b = _dot(h, wu_ref[...].astype(BF))
    act = (a * jax.nn.sigmoid(a) * b).astype(BF)
    o_ref[...] += _dot(act, wd_ref[...].astype(BF))

    @pl.when(j == n_f - 1)
    def _():
        r = _rms(o_ref[...], gpost_ref[...])
        o_ref[...] = x_ref[...] + MACARON * gt_ref[pl.ds(row, 1), :] * r


def _ffn_call(x, mod_l, mod_k, gpre, gpost, w_gu, w_down, l, *, tm, tf, row_div, row_const):
    m, d = x.shape
    d_ff = w_down.shape[1]
    n_f = d_ff // tf
    kern = functools.partial(_ffn_kernel, row_div=row_div, row_const=row_const, n_f=n_f)
    mod_spec = lambda k: pl.BlockSpec((8, d), lambda i, j: (0, k))
    return pl.pallas_call(
        kern,
        grid=(m // tm, n_f),
        in_specs=[
            pl.BlockSpec((tm, d), lambda i, j: (i, 0), pipeline_mode=pl.Buffered(1)),
            mod_spec(mod_k), mod_spec(mod_k + 1), mod_spec(mod_k + 2),
            pl.BlockSpec((1, d), lambda i, j: (0, 0)),
            pl.BlockSpec((1, d), lambda i, j: (0, 0)),
            pl.BlockSpec((None, d, tf), lambda i, j: (l, 0, j)),
            pl.BlockSpec((None, d, tf), lambda i, j: (l, 0, n_f + j)),
            pl.BlockSpec((None, tf, d), lambda i, j: (l, j, 0)),
        ],
        out_specs=pl.BlockSpec((tm, d), lambda i, j: (i, 0)),
        out_shape=jax.ShapeDtypeStruct((m, d), F32),
        scratch_shapes=[pltpu.VMEM((tm, d), BF)],
        compiler_params=_params(("parallel", "arbitrary")),
        name="ffn",
    )(x, mod_l, mod_l, mod_l, gpre, gpost, w_gu, w_gu, w_down)


WIN_TN = 512


def _gelu_tanh(x):
    return 0.5 * x * (1.0 + jnp.tanh(0.7978845608028654 * (x + 0.044715 * x * x * x)))


def _win_kernel(x_ref, sh_ref, sc_ref, gpre_ref, w_ref, lb_ref, p16_ref, pf_ref, h_scr,
                *, row_div, row_const, kinds, layer):
    j = pl.program_id(1)
    row = _mod_row(row_div, row_const)

    @pl.when(j == 0)
    def _():
        y = _rms(x_ref[...], gpre_ref[...])
        h = y * (1.0 + sc_ref[pl.ds(row, 1), :]) + sh_ref[pl.ds(row, 1), :]
        h_scr[...] = h.astype(BF)

    z = _dot(h_scr[...], w_ref[...].astype(BF))

    def in_kind(kind):
        cond = None
        t = 0
        while t < len(kinds):
            t1 = t
            while t1 + 1 < len(kinds) and kinds[t1 + 1] == kinds[t]:
                t1 += 1
            if kinds[t] == kind:
                run = (j >= t) & (j <= t1)
                cond = run if cond is None else cond | run
            t = t1 + 1
        return cond

    for kind, fn in (("gelu", _gelu_tanh), ("id", lambda t: t),
                     ("silu", lambda t: t * jax.nn.sigmoid(t)), ("sigmoid", jax.nn.sigmoid)):
        cond = in_kind(kind)
        if cond is not None:
            @pl.when(cond)
            def _(fn=fn):
                p16_ref[...] = fn(z).astype(BF)

    cond = in_kind("logf")
    if cond is not None:
        @pl.when(cond)
        def _():
            ll = lb_ref[...]
            e = jnp.exp(ll - jnp.max(ll, axis=0, keepdims=True))
            p = e / jnp.sum(e, axis=0, keepdims=True)
            lb = jnp.zeros_like(p[0:1])
            for r in range(1, layer + 1):
                lb = lb + p[r:r + 1]
            f = lb + (1.0 - lb) * jax.nn.sigmoid(z)
            pf_ref[...] = jnp.log(jnp.maximum(f, F_FLOOR))


def _win_call(x, mod_l, mod_k, gpre, w_in, lb_logits2, l, *, tm, row_div, row_const, col0, kinds):
    m, d = x.shape
    tn = WIN_TN
    nj = len(kinds)
    is_f = np.array([k == "logf" for k in kinds])
    n16, nf = int((~is_f).sum()), int(is_f.sum())
    c16 = np.maximum(np.cumsum(~is_f) - 1, 0)
    cf = np.maximum(np.cumsum(is_f) - 1, 0)
    f0 = int(np.argmax(is_f))

    def sel(table):
        def f(j):
            out = jnp.int32(int(table[0]))
            for t in range(1, nj):
                if table[t] != table[t - 1]:
                    out = jnp.where(j >= t, jnp.int32(int(table[t])), out)
            return out
        return f

    s16, sf = sel(c16), sel(cf)
    kern = functools.partial(_win_kernel, row_div=row_div, row_const=row_const, kinds=tuple(kinds), layer=l)
    mod_spec = lambda k: pl.BlockSpec((8, d), lambda i, j: (0, k))
    depth = lb_logits2.shape[0]
    return pl.pallas_call(
        kern,
        grid=(m // tm, nj),
        in_specs=[
            pl.BlockSpec((tm, d), lambda i, j: (i, 0)),
            mod_spec(mod_k), mod_spec(mod_k + 1),
            pl.BlockSpec((1, d), lambda i, j: (0, 0)),
            pl.BlockSpec((None, d, tn), lambda i, j: (l, 0, col0 + j)),
            pl.BlockSpec((depth, tn), lambda i, j: (0, jnp.clip(j - f0, 0, nf - 1))),
        ],
        out_specs=[
            pl.BlockSpec((tm, tn), lambda i, j: (i, s16(j))),
            pl.BlockSpec((tm, tn), lambda i, j: (i, sf(j))),
        ],
        out_shape=[jax.ShapeDtypeStruct((m, n16 * tn), BF), jax.ShapeDtypeStruct((m, nf * tn), F32)],
        scratch_shapes=[pltpu.VMEM((tm, d), BF)],
        compiler_params=_params(("parallel", "arbitrary")),
        name="win",
    )(x, mod_l, mod_l, gpre, w_in, lb_logits2)


def _cmlp_kernel(u_ref, v_ref, g_ref, ws_ref, bs_ref, o_ref, *, n_chunks):
    v = v_ref[...].astype(F32)
    vc = v - jnp.mean(v, axis=-1, keepdims=True)
    vn = (vc * lax.rsqrt(jnp.mean(vc * vc, axis=-1, keepdims=True) + EPS) * g_ref[...]).astype(BF)
    dg = vn.shape[1] // GROUPS_A
    for g in range(GROUPS_A):
        w = ws_ref[g].astype(BF)
        bias = bs_ref[g]
        for c in range(n_chunks):
            rows = slice(c * CHUNK_A, (c + 1) * CHUNK_A)
            cols = slice(g * dg, (g + 1) * dg)
            sv = _dot(w, vn[rows, cols]) + bias
            o_ref[rows, cols] = (u_ref[rows, cols].astype(F32) * sv).astype(BF)


def _cmlp_call(p16, chunk_g, w_s, b_s, d_a, *, tm):
    m = p16.shape[0]
    n_chunks = tm // CHUNK_A
    return pl.pallas_call(
        functools.partial(_cmlp_kernel, n_chunks=n_chunks),
        grid=(m // tm,),
        in_specs=[
            pl.BlockSpec((tm, d_a), lambda i: (i, 0)),
            pl.BlockSpec((tm, d_a), lambda i: (i, 1)),
            pl.BlockSpec((1, d_a), lambda i: (0, 0)),
            pl.BlockSpec((GROUPS_A, CHUNK_A, CHUNK_A), lambda i: (0, 0, 0)),
            pl.BlockSpec((GROUPS_A, CHUNK_A, 1), lambda i: (0, 0, 0)),
        ],
        out_specs=pl.BlockSpec((tm, d_a), lambda i: (i, 0)),
        out_shape=jax.ShapeDtypeStruct((m, d_a), BF),
        compiler_params=_params(("parallel",)),
        name="cmlp",
    )(p16, p16, chunk_g, w_s, b_s[..., None])


def _hgrn_consts(fwd):
    c = HGRN_CHUNK
    t = np.arange(c)[:, None]
    u = np.arange(c)[None, :]

    def level_sum(b):
        half = b // 2
        mid = (t // b) * b + half
        upper = (t % b) >= half
        if fwd:
            return np.where(upper, (u >= mid) & (u <= t), (u > t) & (u < mid))
        return np.where(upper, (u >= mid) & (u < t), (u >= t) & (u < mid))

    def level_mask(b):
        half = b // 2
        upper = (t % b) >= half
        same = (t // b) == (u // b)
        if fwd:
            return same & upper & ((u % b) < half)
        return same & ~upper & ((u % b) >= half)

    lsum = np.concatenate([(u <= t) if fwd else (u >= t), level_sum(4), level_sum(8)], 0)
    masks = [t == u] + [level_mask(b) for b in HGRN_LEVELS] + [np.zeros((c, c), bool)]
    return (np.concatenate([lsum, lsum], 1).astype(np.float32),
            np.concatenate(masks, 1).astype(np.float32))


def _ref_rows(x, rows, span):
    return jnp.concatenate([jnp.broadcast_to(x[r:r + 1, :], (span, x.shape[1])) for r in rows], axis=0)


def _hgrn_stage1(g, l2_ref):
    g_hi = g.astype(BF)
    g_lo = (g - g_hi.astype(F32)).astype(BF)
    return _dot(l2_ref[...], jnp.concatenate([g_hi, g_lo], axis=0))


def _hgrn_stage2(q, g, sums, mask_ref, fwd, with_out):
    c = HGRN_CHUNK
    dk = g.shape[1]
    f = jnp.exp(g)
    k = 1.0 - f
    cum = sums[0:c]
    e_cum = jnp.exp(cum)
    off = -1 if fwd else 0
    e_end = jnp.exp(-jnp.abs(cum - _ref_rows(cum, (c - 1 if fwd else 0,), c)))
    kend = (k * e_end).astype(BF)
    tot = e_cum[c - 1:c] if fwd else e_cum[0:1]
    if not with_out:
        return None, None, kend, tot

    row = lax.broadcasted_iota(jnp.int32, (c, dk), 0)
    e2 = jnp.where((row % 2) == (1 if fwd else 0), f, 1.0)
    es = [e2, jnp.exp(sums[c:2 * c]), jnp.exp(sums[2 * c:3 * c])]
    for b in HGRN_LEVELS[3:]:
        ref = _ref_rows(cum, tuple(m * b + b // 2 + off for m in range(c // b)), b)
        es.append(jnp.exp(-jnp.abs(cum - ref)))
    qf = q.astype(F32)
    units = [(q, k.astype(BF))] + [((qf * e).astype(BF), (k * e).astype(BF)) for e in es]
    z = jnp.zeros((c, dk), BF)
    res = []
    for a in range(0, len(units) - 1, 2):
        (qa, ka), (qb, kb) = units[a], units[a + 1]
        kbd = jnp.concatenate([jnp.concatenate([ka, z], axis=1), jnp.concatenate([z, kb], axis=1)], axis=0)
        res.append(_dot_nt(jnp.concatenate([qa, qb], axis=1), kbd))
    q6, k6 = units[-1]
    res.append(_dot_nt(q6, jnp.concatenate([k6, z], axis=0)))
    p = (jnp.concatenate(res, axis=1) * mask_ref[...]).astype(BF)
    return p, (qf * e_cum).astype(BF), kend, tot


def _hgrn_stage3(p, qe, kend, tot, v, st, with_out):
    st_new = st * tot + _dot_tn(v, kend)
    if not with_out:
        return None, st_new
    o = _dot(p, jnp.concatenate([v] * (p.shape[1] // v.shape[0]), axis=0)) + _dot_nt(qe, st.astype(BF))
    return o, st_new


def _hgrn_kernel(*refs, n, with_out, with_init, with_final):
    it = iter(refs)
    q_ref = next(it) if with_out else None
    v_ref = next(it)
    gs_ref = next(it) if with_out else None
    gf_ref, gb_ref = next(it), next(it)
    gain_ref = next(it) if with_out else None
    sf0_ref, sb0_ref = (next(it), next(it)) if with_init else (None, None)
    l2_refs = (next(it), next(it))
    mask_refs = (next(it), next(it))
    y_ref = next(it) if with_out else None
    sf_ref, sb_ref = (next(it), next(it)) if with_final else (None, None)
    sums_scr, kend_scr, tot_scr, st_scr = next(it), next(it), next(it), next(it)
    p_scr, qe_scr, o_scr = (next(it), next(it), next(it)) if with_out else (None, None, None)

    c = HGRN_CHUNK
    nc = n // c
    dk = v_ref.shape[1]
    g_refs = (gf_ref, gb_ref)

    st_scr[0] = sf0_ref[...] if with_init else jnp.zeros((dk, dk), F32)
    st_scr[1] = sb0_ref[...] if with_init else jnp.zeros((dk, dk), F32)

    def rows_of(d, ci):
        start = ci * c if d == 0 else (nc - 1 - ci) * c
        return pl.ds(start if isinstance(start, int) else pl.multiple_of(start, c), c)

    def step(i, par, do1, do2, do3):
        for d in (0, 1):
            if do3:
                r = rows_of(d, i - 2)
                o, st = _hgrn_stage3(p_scr[d, par] if with_out else None,
                                     qe_scr[d, par] if with_out else None,
                                     kend_scr[d, par], tot_scr[d, par, 0:1, :],
                                     v_ref[r, :], st_scr[d], with_out)
                st_scr[d] = st
                if with_out:
                    o_scr[d, r, :] = o
            if do2:
                r = rows_of(d, i - 1)
                p, qe, kend, tot = _hgrn_stage2(q_ref[r, :] if with_out else None, g_refs[d][r, :],
                                                sums_scr[d, 1 - par], mask_refs[d], d == 0, with_out)
                kend_scr[d, 1 - par] = kend
                tot_scr[d, 1 - par, 0:1, :] = tot
                if with_out:
                    p_scr[d, 1 - par] = p
                    qe_scr[d, 1 - par] = qe
            if do1:
                sums_scr[d, par] = _hgrn_stage1(g_refs[d][rows_of(d, i), :], l2_refs[d])

    step(0, 0, True, False, False)
    step(1, 1, True, True, False)

    def body(ii, _):
        step(2 + 2 * ii, 0, True, True, True)
        step(3 + 2 * ii, 1, True, True, True)
        return 0

    lax.fori_loop(0, (nc - 2) // 2, body, 0)
    step(nc, nc % 2, False, True, True)
    step(nc + 1, (nc + 1) % 2, False, False, True)

    if with_final:
        sf_ref[...] = st_scr[0]
        sb_ref[...] = st_scr[1]

    if with_out:
        rows = 256

        def readout(ri, _):
            r = pl.ds(pl.multiple_of(ri * rows, rows), rows)
            o = o_scr[0, r, :] + o_scr[1, r, :]
            o = o * lax.rsqrt(jnp.mean(o * o, axis=-1, keepdims=True) + EPS) * gain_ref[...]
            y_ref[r, :] = (o * gs_ref[r, :].astype(F32)).astype(BF)
            return 0

        lax.fori_loop(0, n // rows, readout, 0)


def _hgrn_call(p16, pf, gain, init, n, cols, *, with_out, with_final):
    m = p16.shape[0]
    bsz = m // n
    dk = pf.shape[1] // (2 * HEADS_B)
    with_init = init is not None
    cq, cv, cg = cols
    tok = lambda col: pl.BlockSpec((n, dk), lambda b, h: (b, col + h))
    st_spec = pl.BlockSpec((None, None, dk, dk), lambda b, h: (b, h, 0, 0))
    full = lambda a: pl.BlockSpec(a.shape, lambda b, h: (0,) * a.ndim)

    assert (n // HGRN_CHUNK) % 2 == 0 and n % HGRN_CHUNK == 0
    lf, mf = _hgrn_consts(True)
    lbw, mb = _hgrn_consts(False)
    consts = [jnp.asarray(lf, BF), jnp.asarray(lbw, BF), jnp.asarray(mf, F32), jnp.asarray(mb, F32)]

    args, specs = [], []
    if with_out:
        args.append(p16); specs.append(tok(cq))
    args.append(p16); specs.append(tok(cv))
    if with_out:
        args.append(p16); specs.append(tok(cg))
    args += [pf, pf]; specs += [tok(0), tok(HEADS_B)]
    if with_out:
        args.append(gain); specs.append(pl.BlockSpec((1, dk), lambda b, h: (0, h)))
    if with_init:
        args += list(init); specs += [st_spec, st_spec]
    args += consts; specs += [full(a) for a in consts]

    out_shape, out_specs = [], []
    if with_out:
        out_shape.append(jax.ShapeDtypeStruct((m, HEADS_B * dk), BF)); out_specs.append(tok(0))
    if with_final:
        st_shape = jax.ShapeDtypeStruct((bsz, HEADS_B, dk, dk), F32)
        out_shape += [st_shape, st_shape]; out_specs += [st_spec, st_spec]
    c = HGRN_CHUNK
    scratch = [pltpu.VMEM((2, 2, lf.shape[0], dk), F32), pltpu.VMEM((2, 2, c, dk), BF),
               pltpu.VMEM((2, 2, 8, dk), F32), pltpu.VMEM((2, dk, dk), F32)]
    if with_out:
        scratch += [pltpu.VMEM((2, 2, c, mf.shape[1]), BF), pltpu.VMEM((2, 2, c, dk), BF),
                    pltpu.VMEM((2, n, dk), F32)]

    kern = functools.partial(_hgrn_kernel, n=n, with_out=with_out, with_init=with_init, with_final=with_final)
    return pl.pallas_call(
        kern,
        grid=(bsz, HEADS_B),
        in_specs=specs,
        out_specs=out_specs,
        out_shape=out_shape,
        scratch_shapes=scratch,
        compiler_params=_params(("parallel", "parallel")),
        name="hgrn",
    )(*args)


def _merge_kernel(ya_ref, yb_ref, ga_ref, gb_ref, wa_ref, wb_ref, wo_ref, x_ref, gt_ref, gpost_ref,
                  o_ref, *, row_div, row_const, n_j):
    j = pl.program_id(1)
    row = _mod_row(row_div, row_const)
    ma = _dot(ya_ref[...], wa_ref[...].astype(BF))
    mb = _dot(yb_ref[...], wb_ref[...].astype(BF))
    mm = (ga_ref[...].astype(F32) * ma + gb_ref[...].astype(F32) * mb).astype(BF)

    @pl.when(j == 0)
    def _():
        o_ref[...] = jnp.zeros_like(o_ref)

    o_ref[...] += _dot(mm, wo_ref[...].astype(BF))

    @pl.when(j == n_j - 1)
    def _():
        r = _rms(o_ref[...], gpost_ref[...])
        o_ref[...] = x_ref[...] + gt_ref[pl.ds(row, 1), :] * r


def _merge_call(x, ya, yb, p16, gate_cols, mod_l, mod_k, gpost, w_up_a, w_up_b, w_out, l,
                *, tm, row_div, row_const):
    m, d = x.shape
    da, db = ya.shape[1], yb.shape[1]
    tn = WIN_TN
    n_j = d // tn
    ca, cb = gate_cols
    kern = functools.partial(_merge_kernel, row_div=row_div, row_const=row_const, n_j=n_j)
    return pl.pallas_call(
        kern,
        grid=(m // tm, n_j),
        in_specs=[
            pl.BlockSpec((tm, da), lambda i, j: (i, 0)),
            pl.BlockSpec((tm, db), lambda i, j: (i, 0)),
            pl.BlockSpec((tm, tn), lambda i, j: (i, ca + j)),
            pl.BlockSpec((tm, tn), lambda i, j: (i, cb + j)),
            pl.BlockSpec((None, da, tn), lambda i, j: (l, 0, j)),
            pl.BlockSpec((None, db, tn), lambda i, j: (l, 0, j)),
            pl.BlockSpec((None, tn, d), lambda i, j: (l, j, 0)),
            pl.BlockSpec((tm, d), lambda i, j: (i, 0)),
            pl.BlockSpec((8, d), lambda i, j: (0, mod_k)),
            pl.BlockSpec((1, d), lambda i, j: (0, 0)),
        ],
        out_specs=pl.BlockSpec((tm, d), lambda i, j: (i, 0)),
        out_shape=jax.ShapeDtypeStruct((m, d), F32),
        compiler_params=_params(("parallel", "arbitrary")),
        name="merge",
    )(ya, yb, p16, p16, w_up_a, w_up_b, w_out, x, mod_l, gpost)


_KINDS_FULL = (["gelu"] * 4 + ["id"] * 2 + ["logf"] * 4 + ["id"] * 2 + ["silu"] * 2 + ["sigmoid"] * 8)
_KINDS_STATE = ["logf"] * 4 + ["id"] * 2


def kernel(x, c, ctx, c_ctx, w_mod, b_mod, norm_g, ffn1_w_gu, ffn1_w_down, ffn2_w_gu, ffn2_w_down,
           w_in, chunk_norm_g, w_spatial, b_spatial, lb_logits, hgrn_norm_g, w_up_a, w_up_b, w_out):
    bsz, n, d = x.shape
    n_ctx = ctx.shape[1]
    depth = w_mod.shape[0]
    d_a = chunk_norm_g.shape[1]
    d_b = hgrn_norm_g.shape[1]
    ctx_row = bsz

    cs = jnp.concatenate([c, c_ctx[None, :], jnp.zeros((8 - bsz - 1, d), F32)], axis=0)
    mod = _mod_call(cs, w_mod, b_mod)
    lb2 = lb_logits.reshape(depth, -1)

    xl = x.reshape(bsz * n, d)
    xc = ctx.reshape(bsz * n_ctx, d)
    tm = 1024
    lat = dict(tm=tm, row_div=n // tm, row_const=None)
    cx = dict(tm=bsz * n_ctx, row_div=None, row_const=ctx_row)
    lat_m = dict(tm=512, row_div=n // 512, row_const=None)
    cx_m = dict(tm=512, row_div=None, row_const=ctx_row)
    col_q, col_i, col_g = 2 * d_a // 128, (2 * d_a + d_b) // 128, (2 * d_a + 2 * d_b) // 128
    gate_cols = ((2 * d_a + 3 * d_b) // WIN_TN, (2 * d_a + 3 * d_b + d) // WIN_TN)

    for l in range(depth):
        last = l == depth - 1
        ml = mod[l]
        g = [norm_g[l, k][None, :] for k in range(norm_g.shape[1])]
        ffn1 = functools.partial(_ffn_call, mod_l=ml, mod_k=0, gpre=g[0], gpost=g[1],
                                 w_gu=ffn1_w_gu, w_down=ffn1_w_down, l=l, tf=256)
        ffn2 = functools.partial(_ffn_call, mod_l=ml, mod_k=6, gpre=g[4], gpost=g[5],
                                 w_gu=ffn2_w_gu, w_down=ffn2_w_down, l=l, tf=256)
        xl = ffn1(xl, **lat)
        xc = ffn1(xc, **cx)

        if not last:
            pc16, pcf = _win_call(xc, ml, 3, g[2], w_in, lb2, l, col0=0, kinds=_KINDS_FULL, **cx)
            ybc, s_f, s_b = _hgrn_call(pc16, pcf, hgrn_norm_g[l][None, :], None, n_ctx,
                                       (col_q, col_i, col_g), with_out=True, with_final=True)
        else:
            pc16, pcf = _win_call(xc, ml, 3, g[2], w_in, lb2, l, col0=(2 * d_a + d_b) // WIN_TN,
                                  kinds=_KINDS_STATE, **cx)
            s_f, s_b = _hgrn_call(pc16, pcf, None, None, n_ctx, (0, 0, 0), with_out=False, with_final=True)

        p16, pf = _win_call(xl, ml, 3, g[2], w_in, lb2, l, col0=0, kinds=_KINDS_FULL, **lat)
        (yb,) = _hgrn_call(p16, pf, hgrn_norm_g[l][None, :], (s_f, s_b), n,
                           (col_q, col_i, col_g), with_out=True, with_final=False)
        ya = _cmlp_call(p16, chunk_norm_g[l][None, :], w_spatial[l], b_spatial[l], d_a, tm=512)
        xl = _merge_call(xl, ya, yb, p16, gate_cols, ml, 5, g[3], w_up_a, w_up_b, w_out, l, **lat_m)
        xl = ffn2(xl, **lat)

        if not last:
            yac = _cmlp_call(pc16, chunk_norm_g[l][None, :], w_spatial[l], b_spatial[l], d_a, tm=512)
            xc = _merge_call(xc, yac, ybc, pc16, gate_cols, ml, 5, g[3], w_up_a, w_up_b, w_out, l, **cx_m)
            xc = ffn2(xc, **cx)
    return xl.reshape(bsz, n, d)
```

```python
import functools

import numpy as np
import jax
import jax.numpy as jnp
from jax import lax
from jax.experimental import pallas as pl
from jax.experimental.pallas import tpu as pltpu

BF = jnp.bfloat16
F32 = jnp.float32

EPS = 1e-6
MACARON = 0.5
F_FLOOR = 1e-30
N_MOD = 9
GROUPS_A = 8
CHUNK_A = 128
HEADS_B = 8
HGRN_DK = 128
HGRN_CHUNK = 64
HGRN_LEVELS = (2, 4, 8, 16, 32, 64)

VMEM_LIMIT = 56 * 1024 * 1024


def _dot(a, b):
    return jnp.dot(a, b, preferred_element_type=F32)


def _dot_nt(a, b):
    return lax.dot_general(a, b, (((1,), (1,)), ((), ())), preferred_element_type=F32)


def _dot_tn(a, b):
    return lax.dot_general(a, b, (((0,), (0,)), ((), ())), preferred_element_type=F32)


def _rms(x, g):
    return x * lax.rsqrt(jnp.mean(x * x, axis=-1, keepdims=True) + EPS) * g


def _params(sem):
    return pltpu.CompilerParams(dimension_semantics=sem, vmem_limit_bytes=VMEM_LIMIT)


def _mod_kernel(cs_ref, w_ref, b_ref, o_ref):
    cs = cs_ref[...]
    s = (cs * jax.nn.sigmoid(cs)).astype(BF)
    o_ref[...] = _dot(s, w_ref[...].astype(BF)) + b_ref[...]


def _mod_call(cs, w_mod, b_mod):
    depth, d, n = w_mod.shape
    tn = 1024
    return pl.pallas_call(
        _mod_kernel,
        grid=(depth, n // tn),
        in_specs=[
            pl.BlockSpec((8, d), lambda l, j: (0, 0)),
            pl.BlockSpec((None, d, tn), lambda l, j: (l, 0, j)),
            pl.BlockSpec((None, 1, tn), lambda l, j: (l, 0, j)),
        ],
        out_specs=pl.BlockSpec((None, 8, tn), lambda l, j: (l, 0, j)),
        out_shape=jax.ShapeDtypeStruct((depth, 8, n), F32),
        compiler_params=_params(("parallel", "parallel")),
        name="mod",
    )(cs, w_mod, b_mod.reshape(depth, 1, n))


def _mod_row(row_div, row_const):
    if row_div is None:
        return row_const
    return pl.program_id(0) // row_div


ROW_CHUNK = 64


def _row_loop(n_rows, fn):
    def body(r, _):
        fn(pl.ds(pl.multiple_of(r * ROW_CHUNK, ROW_CHUNK), ROW_CHUNK))
        return 0

    lax.fori_loop(0, n_rows // ROW_CHUNK, body, 0)


def _modulate_into(h_scr, x_ref, gpre_ref, sc_ref, sh_ref, row):
    gain = gpre_ref[...] * (1.0 + sc_ref[pl.ds(row, 1), :])
    shift = sh_ref[pl.ds(row, 1), :]

    def rows(sl):
        x = x_ref[sl, :]
        rs = lax.rsqrt(jnp.mean(x * x, axis=-1, keepdims=True) + EPS)
        h_scr[sl, :] = (x * rs * gain + shift).astype(BF)

    _row_loop(x_ref.shape[0], rows)


def _residual_into(o_ref, x_ref, gpost_ref, gt_ref, row, weight):
    gain = gpost_ref[...] * gt_ref[pl.ds(row, 1), :] * weight

    def rows(sl):
        y = o_ref[sl, :]
        rs = lax.rsqrt(jnp.mean(y * y, axis=-1, keepdims=True) + EPS)
        o_ref[sl, :] = x_ref[sl, :] + y * rs * gain

    _row_loop(o_ref.shape[0], rows)


def _ffn_kernel(x_ref, sh_ref, sc_ref, gt_ref, gpre_ref, gpost_ref, wg_ref, wu_ref, wd_ref,
                o_ref, h_scr, *, row_div, row_const, n_f):
    j = pl.program_id(1)
    row = _mod_row(row_div, row_const)

    @pl.when(j == 0)
    def _():
        _modulate_into(h_scr, x_ref, gpre_ref, sc_ref, sh_ref, row)
        o_ref[...] = jnp.zeros_like(o_ref)

    h = h_scr[...]
    a = _dot(h, wg_ref[...].astype(BF))
    b = _dot(h, wu_ref[...].astype(BF))
    act = (a * jax.nn.sigmoid(a) * b).astype(BF)
    o_ref[...] += _dot(act, wd_ref[...].astype(BF))

    @pl.when(j == n_f - 1)
    def _():
        _residual_into(o_ref, x_ref, gpost_ref, gt_ref, row, MACARON)


def _ffn_call(x, mod_l, mod_k, gpre, gpost, w_gu, w_down, l, *, tm, tf, row_div, row_const):
    m, d = x.shape
    d_ff = w_down.shape[1]
    n_f = d_ff // tf
    kern = functools.partial(_ffn_kernel, row_div=row_div, row_const=row_const, n_f=n_f)
    mod_spec = lambda k: pl.BlockSpec((8, d), lambda i, j: (0, k))
    return pl.pallas_call(
        kern,
        grid=(m // tm, n_f),
        in_specs=[
            pl.BlockSpec((tm, d), lambda i, j: (i, 0), pipeline_mode=pl.Buffered(1)),
            mod_spec(mod_k), mod_spec(mod_k + 1), mod_spec(mod_k + 2),
            pl.BlockSpec((1, d), lambda i, j: (0, 0)),
            pl.BlockSpec((1, d), lambda i, j: (0, 0)),
            pl.BlockSpec((None, d, tf), lambda i, j: (l, 0, j)),
            pl.BlockSpec((None, d, tf), lambda i, j: (l, 0, n_f + j)),
            pl.BlockSpec((None, tf, d), lambda i, j: (l, j, 0)),
        ],
        out_specs=pl.BlockSpec((tm, d), lambda i, j: (i, 0)),
        out_shape=jax.ShapeDtypeStruct((m, d), F32),
        scratch_shapes=[pltpu.VMEM((tm, d), BF)],
        compiler_params=_params(("parallel", "arbitrary")),
        name="ffn",
    )(x, mod_l, mod_l, mod_l, gpre, gpost, w_gu, w_gu, w_down)


WIN_TN = 512
WIN_SLABS = 4
MERGE_TN = 256


def _gelu_tanh(z):
    return z * jax.nn.sigmoid((2.0 * 0.7978845608028654) * (z + 0.044715 * (z * z * z)))


def _win_kernel(x_ref, sh_ref, sc_ref, gpre_ref, w_ref, lb_ref, p16_ref, pf_ref, h_scr,
                *, row_div, row_const, kinds, layer):
    j = pl.program_id(1)
    row = _mod_row(row_div, row_const)

    @pl.when(j == 0)
    def _():
        _modulate_into(h_scr, x_ref, gpre_ref, sc_ref, sh_ref, row)

    def in_kind(kind):
        cond = jnp.bool_(False)
        t = 0
        while t < len(kinds):
            t1 = t
            while t1 + 1 < len(kinds) and kinds[t1 + 1] == kinds[t]:
                t1 += 1
            if kinds[t] == kind:
                cond = cond | ((j >= t) & (j <= t1))
            t = t1 + 1
        return cond

    def slabs(out_ref, fn):
        w = w_ref[...].astype(BF)
        tm = h_scr.shape[0]
        for r0 in range(0, tm, tm // WIN_SLABS):
            sl = slice(r0, r0 + tm // WIN_SLABS)
            out_ref[sl, :] = fn(_dot(h_scr[sl, :], w)).astype(out_ref.dtype)

    def logf(z):
        ll = lb_ref[...]
        e = jnp.exp(ll - jnp.max(ll, axis=0, keepdims=True))
        p = e / jnp.sum(e, axis=0, keepdims=True)
        lb = jnp.zeros_like(p[0:1])
        for r in range(1, layer + 1):
            lb = lb + p[r:r + 1]
        return jnp.log2(jnp.maximum(lb + (1.0 - lb) * jax.nn.sigmoid(z), F_FLOOR))

    for kind, out_ref, fn in (("gelu", p16_ref, _gelu_tanh), ("id", p16_ref, lambda z: z),
                              ("silu", p16_ref, lambda z: z * jax.nn.sigmoid(z)),
                              ("sigmoid", p16_ref, jax.nn.sigmoid), ("logf", pf_ref, logf)):
        if kind in kinds:
            pl.when(in_kind(kind))(functools.partial(slabs, out_ref, fn))


def _win_call(x, mod_l, mod_k, gpre, w_in, lb_logits2, l, *, tm, row_div, row_const, col0, kinds):
    m, d = x.shape
    tn = WIN_TN
    nj = len(kinds)
    is_f = np.array([k == "logf" for k in kinds])
    n16, nf = int((~is_f).sum()), int(is_f.sum())
    c16 = np.maximum(np.cumsum(~is_f) - 1, 0)
    cf = np.maximum(np.cumsum(is_f) - 1, 0)
    f0 = int(np.argmax(is_f))

    def sel(table):
        def f(j):
            out = jnp.int32(int(table[0]))
            for t in range(1, nj):
                if table[t] != table[t - 1]:
                    out = jnp.where(j >= t, jnp.int32(int(table[t])), out)
            return out
        return f

    s16, sf = sel(c16), sel(cf)
    kern = functools.partial(_win_kernel, row_div=row_div, row_const=row_const, kinds=tuple(kinds), layer=l)
    mod_spec = lambda k: pl.BlockSpec((8, d), lambda i, j: (0, k))
    depth = lb_logits2.shape[0]
    return pl.pallas_call(
        kern,
        grid=(m // tm, nj),
        in_specs=[
            pl.BlockSpec((tm, d), lambda i, j: (i, 0)),
            mod_spec(mod_k), mod_spec(mod_k + 1),
            pl.BlockSpec((1, d), lambda i, j: (0, 0)),
            pl.BlockSpec((None, d, tn), lambda i, j: (l, 0, col0 + j)),
            pl.BlockSpec((depth, tn), lambda i, j: (0, jnp.clip(j - f0, 0, nf - 1))),
        ],
        out_specs=[
            pl.BlockSpec((tm, tn), lambda i, j: (i, s16(j))),
            pl.BlockSpec((tm, tn), lambda i, j: (i, sf(j))),
        ],
        out_shape=[jax.ShapeDtypeStruct((m, n16 * tn), BF), jax.ShapeDtypeStruct((m, nf * tn), F32)],
        scratch_shapes=[pltpu.VMEM((tm, d), BF)],
        compiler_params=_params(("parallel", "arbitrary")),
        name="win",
    )(x, mod_l, mod_l, gpre, w_in, lb_logits2)


def _cmlp_kernel(u_ref, v_ref, g_ref, ws_ref, bs_ref, o_ref, *, n_chunks):
    v = v_ref[...].astype(F32)
    vc = v - jnp.mean(v, axis=-1, keepdims=True)
    vn = (vc * lax.rsqrt(jnp.mean(vc * vc, axis=-1, keepdims=True) + EPS) * g_ref[...]).astype(BF)
    dg = vn.shape[1] // GROUPS_A
    for g in range(GROUPS_A):
        w = ws_ref[g].astype(BF)
        bias = bs_ref[g]
        for c in range(n_chunks):
            rows = slice(c * CHUNK_A, (c + 1) * CHUNK_A)
            cols = slice(g * dg, (g + 1) * dg)
            sv = _dot(w, vn[rows, cols]) + bias
            o_ref[rows, cols] = (u_ref[rows, cols].astype(F32) * sv).astype(BF)


def _cmlp_call(p16, chunk_g, w_s, b_s, d_a, *, tm):
    m = p16.shape[0]
    n_chunks = tm // CHUNK_A
    return pl.pallas_call(
        functools.partial(_cmlp_kernel, n_chunks=n_chunks),
        grid=(m // tm,),
        in_specs=[
            pl.BlockSpec((tm, d_a), lambda i: (i, 0)),
            pl.BlockSpec((tm, d_a), lambda i: (i, 1)),
            pl.BlockSpec((1, d_a), lambda i: (0, 0)),
            pl.BlockSpec((GROUPS_A, CHUNK_A, CHUNK_A), lambda i: (0, 0, 0)),
            pl.BlockSpec((GROUPS_A, CHUNK_A, 1), lambda i: (0, 0, 0)),
        ],
        out_specs=pl.BlockSpec((tm, d_a), lambda i: (i, 0)),
        out_shape=jax.ShapeDtypeStruct((m, d_a), BF),
        compiler_params=_params(("parallel",)),
        name="cmlp",
    )(p16, p16, chunk_g, w_s, b_s[..., None])


def _hgrn_consts(fwd):
    c = HGRN_CHUNK
    t = np.arange(c)[:, None]
    u = np.arange(c)[None, :]

    def level_sum(b):
        half = b // 2
        mid = (t // b) * b + half
        upper = (t % b) >= half
        if fwd:
            return np.where(upper, (u >= mid) & (u <= t), (u > t) & (u < mid))
        return np.where(upper, (u >= mid) & (u < t), (u >= t) & (u < mid))

    def level_mask(b):
        half = b // 2
        upper = (t % b) >= half
        same = (t // b) == (u // b)
        if fwd:
            return same & upper & ((u % b) < half)
        return same & ~upper & ((u % b) >= half)

    lsum = np.concatenate([(u <= t) if fwd else (u >= t), level_sum(4), level_sum(8)], 0)
    masks = [t == u] + [level_mask(b) for b in HGRN_LEVELS] + [np.zeros((c, c), bool)]
    return (np.concatenate([lsum, lsum], 1).astype(np.float32),
            np.concatenate(masks, 1).astype(np.float32))


def _to_midpoint(cum, b, fwd):
    c, dk = cum.shape
    half = b // 2
    pieces = []
    for m in range(c // b):
        lo, mid = m * b, m * b + half
        ref = jnp.broadcast_to(cum[mid - 1:mid, :] if fwd else cum[mid:mid + 1, :], (half, dk))
        lower, upper = cum[lo:mid], cum[mid:lo + b]
        pieces += [ref - lower, upper - ref] if fwd else [lower - ref, ref - upper]
    return jnp.concatenate(pieces, axis=0)


def _hgrn_stage1(g, l2_ref):
    g_hi = g.astype(BF)
    g_lo = (g - g_hi.astype(F32)).astype(BF)
    return _dot(l2_ref[...], jnp.concatenate([g_hi, g_lo], axis=0))


def _hgrn_stage2(q, g, sums, mask_ref, fwd, with_out):
    c = HGRN_CHUNK
    dk = g.shape[1]
    f = jnp.exp2(g)
    k = 1.0 - f
    cum = sums[0:c]
    e_cum = jnp.exp2(cum)
    total = cum[c - 1:c] if fwd else cum[0:1]
    kend = (k * jnp.exp2(total - cum)).astype(BF)
    tot = e_cum[c - 1:c] if fwd else e_cum[0:1]
    if not with_out:
        return None, None, kend, tot

    row = lax.broadcasted_iota(jnp.int32, (c, dk), 0)
    e2 = jnp.where((row % 2) == (1 if fwd else 0), f, 1.0)
    es = [e2, jnp.exp2(sums[c:2 * c]), jnp.exp2(sums[2 * c:3 * c])]
    es += [jnp.exp2(_to_midpoint(cum, b, fwd)) for b in HGRN_LEVELS[3:]]
    qf = q.astype(F32)
    units = [(q, k.astype(BF))] + [((qf * e).astype(BF), (k * e).astype(BF)) for e in es]
    z = jnp.zeros((c, dk), BF)
    res = []
    for a in range(0, len(units) - 1, 2):
        (qa, ka), (qb, kb) = units[a], units[a + 1]
        kbd = jnp.concatenate([jnp.concatenate([ka, z], axis=1), jnp.concatenate([z, kb], axis=1)], axis=0)
        res.append(_dot_nt(jnp.concatenate([qa, qb], axis=1), kbd))
    q6, k6 = units[-1]
    res.append(_dot_nt(q6, jnp.concatenate([k6, z], axis=0)))
    p = sum(r * mask_ref[:, i * 2 * c:(i + 1) * 2 * c] for i, r in enumerate(res)).astype(BF)
    return p, (qf * e_cum).astype(BF), kend, tot


def _hgrn_stage3(p, qe, kend, tot, v, st, with_out):
    st_new = st * tot + _dot_tn(v, kend)
    if not with_out:
        return None, st_new
    o = _dot(p, jnp.concatenate([v, v], axis=0)) + _dot_nt(qe, st.astype(BF))
    return o, st_new


def _hgrn_kernel(*refs, n, with_out, with_init, with_final):
    it = iter(refs)
    q_ref = next(it) if with_out else None
    v_ref = next(it)
    gs_ref = next(it) if with_out else None
    gf_ref, gb_ref = next(it), next(it)
    gain_ref = next(it) if with_out else None
    sf0_ref, sb0_ref = (next(it), next(it)) if with_init else (None, None)
    l2_refs = (next(it), next(it))
    mask_refs = (next(it), next(it))
    y_ref = next(it) if with_out else None
    sf_ref, sb_ref = (next(it), next(it)) if with_final else (None, None)
    sums_scr, kend_scr, tot_scr, st_scr = next(it), next(it), next(it), next(it)
    p_scr, qe_scr, o_scr = (next(it), next(it), next(it)) if with_out else (None, None, None)

    c = HGRN_CHUNK
    nc = n // c
    dk = v_ref.shape[1]
    g_refs = (gf_ref, gb_ref)

    st_scr[0] = sf0_ref[...] if with_init else jnp.zeros((dk, dk), F32)
    st_scr[1] = sb0_ref[...] if with_init else jnp.zeros((dk, dk), F32)

    def rows_of(d, ci):
        start = ci * c if d == 0 else (nc - 1 - ci) * c
        return pl.ds(start if isinstance(start, int) else pl.multiple_of(start, c), c)

    def step(i, par, do1, do2, do3):
        for d in (0, 1):
            if do3:
                r = rows_of(d, i - 2)
                o, st = _hgrn_stage3(p_scr[d, par] if with_out else None,
                                     qe_scr[d, par] if with_out else None,
                                     kend_scr[d, par], tot_scr[d, par, 0:1, :],
                                     v_ref[r, :], st_scr[d], with_out)
                st_scr[d] = st
                if with_out:
                    o_scr[d, r, :] = o
            if do2:
                r = rows_of(d, i - 1)
                p, qe, kend, tot = _hgrn_stage2(q_ref[r, :] if with_out else None, g_refs[d][r, :],
                                                sums_scr[d, 1 - par], mask_refs[d], d == 0, with_out)
                kend_scr[d, 1 - par] = kend
                tot_scr[d, 1 - par, 0:1, :] = tot
                if with_out:
                    p_scr[d, 1 - par] = p
                    qe_scr[d, 1 - par] = qe
            if do1:
                sums_scr[d, par] = _hgrn_stage1(g_refs[d][rows_of(d, i), :], l2_refs[d])

    step(0, 0, True, False, False)
    step(1, 1, True, True, False)

    def body(ii, _):
        step(2 + 2 * ii, 0, True, True, True)
        step(3 + 2 * ii, 1, True, True, True)
        return 0

    lax.fori_loop(0, (nc - 2) // 2, body, 0)
    step(nc, nc % 2, False, True, True)
    step(nc + 1, (nc + 1) % 2, False, False, True)

    if with_final:
        sf_ref[...] = st_scr[0]
        sb_ref[...] = st_scr[1]

    if with_out:
        rows = 256

        def readout(ri, _):
            r = pl.ds(pl.multiple_of(ri * rows, rows), rows)
            o = o_scr[0, r, :] + o_scr[1, r, :]
            o = o * lax.rsqrt(jnp.mean(o * o, axis=-1, keepdims=True) + EPS) * gain_ref[...]
            y_ref[r, :] = (o * gs_ref[r, :].astype(F32)).astype(BF)
            return 0

        lax.fori_loop(0, n // rows, readout, 0)


def _hgrn_call(p16, pf, gain, init, n, cols, *, with_out, with_final):
    m = p16.shape[0]
    bsz = m // n
    dk = HGRN_DK
    with_init = init is not None
    cq, cv, cg = cols
    tok = lambda col: pl.BlockSpec((n, dk), lambda b, h: (b, col + h))
    st_spec = pl.BlockSpec((None, None, dk, dk), lambda b, h: (b, h, 0, 0))
    full = lambda a: pl.BlockSpec(a.shape, lambda b, h: (0,) * a.ndim)

    assert (n // HGRN_CHUNK) % 2 == 0 and n % HGRN_CHUNK == 0
    lf, mf = _hgrn_consts(True)
    lbw, mb = _hgrn_consts(False)
    consts = [jnp.asarray(lf, BF), jnp.asarray(lbw, BF), jnp.asarray(mf, F32), jnp.asarray(mb, F32)]

    args, specs = [], []
    if with_out:
        args.append(p16); specs.append(tok(cq))
    args.append(p16); specs.append(tok(cv))
    if with_out:
        args.append(p16); specs.append(tok(cg))
    args += [pf, pf]; specs += [tok(0), tok(HEADS_B)]
    if with_out:
        args.append(gain); specs.append(pl.BlockSpec((1, dk), lambda b, h: (0, h)))
    if with_init:
        args += list(init); specs += [st_spec, st_spec]
    args += consts; specs += [full(a) for a in consts]

    out_shape, out_specs = [], []
    if with_out:
        out_shape.append(jax.ShapeDtypeStruct((m, HEADS_B * dk), BF)); out_specs.append(tok(0))
    if with_final:
        st_shape = jax.ShapeDtypeStruct((bsz, HEADS_B, dk, dk), F32)
        out_shape += [st_shape, st_shape]; out_specs += [st_spec, st_spec]
    c = HGRN_CHUNK
    scratch = [pltpu.VMEM((2, 2, lf.shape[0], dk), F32), pltpu.VMEM((2, 2, c, dk), BF),
               pltpu.VMEM((2, 2, 8, dk), F32), pltpu.VMEM((2, dk, dk), F32)]
    if with_out:
        scratch += [pltpu.VMEM((2, 2, c, 2 * c), BF), pltpu.VMEM((2, 2, c, dk), BF),
                    pltpu.VMEM((2, n, dk), F32)]

    kern = functools.partial(_hgrn_kernel, n=n, with_out=with_out, with_init=with_init, with_final=with_final)
    return pl.pallas_call(
        kern,
        grid=(bsz, HEADS_B),
        in_specs=specs,
        out_specs=out_specs,
        out_shape=out_shape,
        scratch_shapes=scratch,
        compiler_params=_params(("parallel", "parallel")),
        name="hgrn",
    )(*args)


def _merge_kernel(ya_ref, yb_ref, ga_ref, gb_ref, wa_ref, wb_ref, wo_ref, x_ref, gt_ref, gpost_ref,
                  o_ref, *, row_div, row_const, n_j):
    j = pl.program_id(1)
    row = _mod_row(row_div, row_const)
    ma = _dot(ya_ref[...], wa_ref[...].astype(BF))
    mb = _dot(yb_ref[...], wb_ref[...].astype(BF))
    mm = (ga_ref[...].astype(F32) * ma + gb_ref[...].astype(F32) * mb).astype(BF)

    @pl.when(j == 0)
    def _():
        o_ref[...] = jnp.zeros_like(o_ref)

    o_ref[...] += _dot(mm, wo_ref[...].astype(BF))

    @pl.when(j == n_j - 1)
    def _():
        _residual_into(o_ref, x_ref, gpost_ref, gt_ref, row, 1.0)


def _merge_call(x, ya, yb, p16, gate_cols, mod_l, mod_k, gpost, w_up_a, w_up_b, w_out, l,
                *, tm, row_div, row_const):
    m, d = x.shape
    da, db = ya.shape[1], yb.shape[1]
    tn = MERGE_TN
    n_j = d // tn
    ca, cb = (c // tn for c in gate_cols)
    kern = functools.partial(_merge_kernel, row_div=row_div, row_const=row_const, n_j=n_j)
    return pl.pallas_call(
        kern,
        grid=(m // tm, n_j),
        in_specs=[
            pl.BlockSpec((tm, da), lambda i, j: (i, 0)),
            pl.BlockSpec((tm, db), lambda i, j: (i, 0)),
            pl.BlockSpec((tm, tn), lambda i, j: (i, ca + j)),
            pl.BlockSpec((tm, tn), lambda i, j: (i, cb + j)),
            pl.BlockSpec((None, da, tn), lambda i, j: (l, 0, j)),
            pl.BlockSpec((None, db, tn), lambda i, j: (l, 0, j)),
            pl.BlockSpec((None, tn, d), lambda i, j: (l, j, 0)),
            pl.BlockSpec((tm, d), lambda i, j: (i, 0), pipeline_mode=pl.Buffered(1)),
            pl.BlockSpec((8, d), lambda i, j: (0, mod_k)),
            pl.BlockSpec((1, d), lambda i, j: (0, 0)),
        ],
        out_specs=pl.BlockSpec((tm, d), lambda i, j: (i, 0)),
        out_shape=jax.ShapeDtypeStruct((m, d), F32),
        compiler_params=_params(("parallel", "arbitrary")),
        name="merge",
    )(ya, yb, p16, p16, w_up_a, w_up_b, w_out, x, mod_l, gpost)


_KINDS_FULL = (["gelu"] * 4 + ["id"] * 2 + ["logf"] * 4 + ["id"] * 2 + ["silu"] * 2 + ["sigmoid"] * 8)
_KINDS_STATE = ["logf"] * 4 + ["id"] * 2


def kernel(x, c, ctx, c_ctx, w_mod, b_mod, norm_g, ffn1_w_gu, ffn1_w_down, ffn2_w_gu, ffn2_w_down,
           w_in, chunk_norm_g, w_spatial, b_spatial, lb_logits, hgrn_norm_g, w_up_a, w_up_b, w_out):
    bsz, n, d = x.shape
    n_ctx = ctx.shape[1]
    depth = w_mod.shape[0]
    d_a = chunk_norm_g.shape[1]
    d_b = hgrn_norm_g.shape[1]
    ctx_row = bsz

    cs = jnp.concatenate([c, c_ctx[None, :], jnp.zeros((8 - bsz - 1, d), F32)], axis=0)
    mod = _mod_call(cs, w_mod, b_mod)
    lb2 = lb_logits.reshape(depth, -1)

    xl = x.reshape(bsz * n, d)
    xc = ctx.reshape(bsz * n_ctx, d)
    tm = 1024
    lat = dict(tm=tm, row_div=n // tm, row_const=None)
    cx = dict(tm=bsz * n_ctx, row_div=None, row_const=ctx_row)
    col_q, col_i, col_g = 2 * d_a // 128, (2 * d_a + d_b) // 128, (2 * d_a + 2 * d_b) // 128
    gate_cols = (2 * d_a + 3 * d_b, 2 * d_a + 3 * d_b + d)

    for l in range(depth):
        last = l == depth - 1
        ml = mod[l]
        g = [norm_g[l, k][None, :] for k in range(norm_g.shape[1])]
        ffn1 = functools.partial(_ffn_call, mod_l=ml, mod_k=0, gpre=g[0], gpost=g[1],
                                 w_gu=ffn1_w_gu, w_down=ffn1_w_down, l=l, tf=256)
        ffn2 = functools.partial(_ffn_call, mod_l=ml, mod_k=6, gpre=g[4], gpost=g[5],
                                 w_gu=ffn2_w_gu, w_down=ffn2_w_down, l=l, tf=256)
        xl = ffn1(xl, **lat)
        xc = ffn1(xc, **cx)

        if not last:
            pc16, pcf = _win_call(xc, ml, 3, g[2], w_in, lb2, l, col0=0, kinds=_KINDS_FULL, **cx)
            ybc, s_f, s_b = _hgrn_call(pc16, pcf, hgrn_norm_g[l][None, :], None, n_ctx,
                                       (col_q, col_i, col_g), with_out=True, with_final=True)
        else:
            pc16, pcf = _win_call(xc, ml, 3, g[2], w_in, lb2, l, col0=(2 * d_a + d_b) // WIN_TN,
                                  kinds=_KINDS_STATE, **cx)
            s_f, s_b = _hgrn_call(pc16, pcf, None, None, n_ctx, (0, 0, 0), with_out=False, with_final=True)

        p16, pf = _win_call(xl, ml, 3, g[2], w_in, lb2, l, col0=0, kinds=_KINDS_FULL, **lat)
        (yb,) = _hgrn_call(p16, pf, hgrn_norm_g[l][None, :], (s_f, s_b), n,
                           (col_q, col_i, col_g), with_out=True, with_final=False)
        ya = _cmlp_call(p16, chunk_norm_g[l][None, :], w_spatial[l], b_spatial[l], d_a, tm=512)
        xl = _merge_call(xl, ya, yb, p16, gate_cols, ml, 5, g[3], w_up_a, w_up_b, w_out, l, **lat)
        xl = ffn2(xl, **lat)

        if not last:
            yac = _cmlp_call(pc16, chunk_norm_g[l][None, :], w_spatial[l], b_spatial[l], d_a, tm=512)
            xc = _merge_call(xc, yac, ybc, pc16, gate_cols, ml, 5, g[3], w_up_a, w_up_b, w_out, l, **cx)
            xc = ffn2(xc, **cx)
    return xl.reshape(bsz, n, d)
```

```python
import functools

import numpy as np
import jax
import jax.numpy as jnp
from jax import lax
from jax.experimental import pallas as pl
from jax.experimental.pallas import tpu as pltpu

BF = jnp.bfloat16
F32 = jnp.float32

EPS = 1e-6
MACARON = 0.5
F_FLOOR = 1e-30
N_MOD = 9
GROUPS_A = 8
CHUNK_A = 128
HEADS_B = 8
HGRN_DK = 128
HGRN_CHUNK = 64
HGRN_LEVELS = (2, 4, 8, 16, 32, 64)

VMEM_LIMIT = 56 * 1024 * 1024


def _dot(a, b):
    return jnp.dot(a, b, preferred_element_type=F32)


def _dot_nt(a, b):
    return lax.dot_general(a, b, (((1,), (1,)), ((), ())), preferred_element_type=F32)


def _dot_tn(a, b):
    return lax.dot_general(a, b, (((0,), (0,)), ((), ())), preferred_element_type=F32)


def _rms(x, g):
    return x * lax.rsqrt(jnp.mean(x * x, axis=-1, keepdims=True) + EPS) * g


def _params(sem):
    return pltpu.CompilerParams(dimension_semantics=sem, vmem_limit_bytes=VMEM_LIMIT)


def _mod_kernel(cs_ref, w_ref, b_ref, o_ref):
    cs = cs_ref[...]
    s = (cs * jax.nn.sigmoid(cs)).astype(BF)
    o_ref[...] = _dot(s, w_ref[...].astype(BF)) + b_ref[...]


def _mod_call(cs, w_mod, b_mod):
    depth, d, n = w_mod.shape
    tn = 1024
    return pl.pallas_call(
        _mod_kernel,
        grid=(depth, n // tn),
        in_specs=[
            pl.BlockSpec((8, d), lambda l, j: (0, 0)),
            pl.BlockSpec((None, d, tn), lambda l, j: (l, 0, j)),
            pl.BlockSpec((None, 1, tn), lambda l, j: (l, 0, j)),
        ],
        out_specs=pl.BlockSpec((None, 8, tn), lambda l, j: (l, 0, j)),
        out_shape=jax.ShapeDtypeStruct((depth, 8, n), F32),
        compiler_params=_params(("parallel", "parallel")),
        name="mod",
    )(cs, w_mod, b_mod.reshape(depth, 1, n))


def _mod_row(row_div, row_const):
    if row_div is None:
        return row_const
    return pl.program_id(0) // row_div


ROW_CHUNK = 64


def _row_loop(n_rows, fn):
    def body(r, _):
        fn(pl.ds(pl.multiple_of(r * ROW_CHUNK, ROW_CHUNK), ROW_CHUNK))
        return 0

    lax.fori_loop(0, n_rows // ROW_CHUNK, body, 0)


def _modulate_into(h_scr, x_ref, gpre_ref, sc_ref, sh_ref, row):
    gain = gpre_ref[...] * (1.0 + sc_ref[pl.ds(row, 1), :])
    shift = sh_ref[pl.ds(row, 1), :]

    def rows(sl):
        x = x_ref[sl, :]
        rs = lax.rsqrt(jnp.mean(x * x, axis=-1, keepdims=True) + EPS)
        h_scr[sl, :] = (x * rs * gain + shift).astype(BF)

    _row_loop(x_ref.shape[0], rows)


def _residual_into(o_ref, x_ref, gpost_ref, gt_ref, row, weight):
    gain = gpost_ref[...] * gt_ref[pl.ds(row, 1), :] * weight

    def rows(sl):
        y = o_ref[sl, :]
        rs = lax.rsqrt(jnp.mean(y * y, axis=-1, keepdims=True) + EPS)
        o_ref[sl, :] = x_ref[sl, :] + y * rs * gain

    _row_loop(o_ref.shape[0], rows)


def _x_copy(x_hbm, row0, r, xbuf, xsem, slot):
    rows = pl.ds(pl.multiple_of(row0 + r * ROW_CHUNK, ROW_CHUNK), ROW_CHUNK)
    return pltpu.make_async_copy(x_hbm.at[rows, :], xbuf.at[slot], xsem.at[slot])


def _modulate_stream(h_scr, acc_scr, x_hbm, row0, xbuf, xsem, gpre_ref, sc_ref, sh_ref, row):
    n = h_scr.shape[0] // ROW_CHUNK
    gain = gpre_ref[...] * (1.0 + sc_ref[pl.ds(row, 1), :])
    shift = sh_ref[pl.ds(row, 1), :]
    _x_copy(x_hbm, row0, 0, xbuf, xsem, 0).start()

    def body(r, _):
        slot = r % 2

        @pl.when(r + 1 < n)
        def _():
            _x_copy(x_hbm, row0, r + 1, xbuf, xsem, 1 - slot).start()

        _x_copy(x_hbm, row0, r, xbuf, xsem, slot).wait()
        x = xbuf[slot]
        rs = lax.rsqrt(jnp.mean(x * x, axis=-1, keepdims=True) + EPS)
        sl = pl.ds(pl.multiple_of(r * ROW_CHUNK, ROW_CHUNK), ROW_CHUNK)
        h_scr[sl, :] = (x * rs * gain + shift).astype(BF)
        acc_scr[sl, :] = jnp.zeros((ROW_CHUNK, acc_scr.shape[1]), F32)
        return 0

    lax.fori_loop(0, n, body, 0)


def _residual_stream(o_hbm, acc_scr, x_hbm, row0, xbuf, xsem, obuf, osem, gpost_ref, gt_ref, row, weight):
    n = acc_scr.shape[0] // ROW_CHUNK
    gain = gpost_ref[...] * gt_ref[pl.ds(row, 1), :] * weight

    def o_copy(r, slot):
        rows = pl.ds(pl.multiple_of(row0 + r * ROW_CHUNK, ROW_CHUNK), ROW_CHUNK)
        return pltpu.make_async_copy(obuf.at[slot], o_hbm.at[rows, :], osem.at[slot])

    _x_copy(x_hbm, row0, 0, xbuf, xsem, 0).start()

    def body(r, _):
        slot = r % 2

        @pl.when(r + 1 < n)
        def _():
            _x_copy(x_hbm, row0, r + 1, xbuf, xsem, 1 - slot).start()

        _x_copy(x_hbm, row0, r, xbuf, xsem, slot).wait()

        @pl.when(r >= 2)
        def _():
            o_copy(r - 2, slot).wait()

        y = acc_scr[pl.ds(pl.multiple_of(r * ROW_CHUNK, ROW_CHUNK), ROW_CHUNK), :]
        rs = lax.rsqrt(jnp.mean(y * y, axis=-1, keepdims=True) + EPS)
        obuf[slot] = xbuf[slot] + y * rs * gain
        o_copy(r, slot).start()
        return 0

    lax.fori_loop(0, n, body, 0)
    o_copy(n - 2, n % 2).wait()
    o_copy(n - 1, (n - 1) % 2).wait()


FFN_SLAB = 512


def _ffn_kernel(x_hbm, sh_ref, sc_ref, gt_ref, gpre_ref, gpost_ref, wg_ref, wu_ref, wd_ref,
                o_hbm, h_scr, acc_scr, xbuf, obuf, xsem, osem, *, row_div, row_const, n_f):
    j = pl.program_id(1)
    row = _mod_row(row_div, row_const)
    tm = h_scr.shape[0]
    row0 = pl.program_id(0) * tm

    @pl.when(j == 0)
    def _():
        _modulate_stream(h_scr, acc_scr, x_hbm, row0, xbuf, xsem, gpre_ref, sc_ref, sh_ref, row)

    wg, wu, wd = wg_ref[...].astype(BF), wu_ref[...].astype(BF), wd_ref[...].astype(BF)
    for r0 in range(0, tm, FFN_SLAB):
        sl = slice(r0, r0 + FFN_SLAB)
        h = h_scr[sl, :]
        a = _dot(h, wg)
        b = _dot(h, wu)
        acc_scr[sl, :] += _dot((a * jax.nn.sigmoid(a) * b).astype(BF), wd)

    @pl.when(j == n_f - 1)
    def _():
        _residual_stream(o_hbm, acc_scr, x_hbm, row0, xbuf, xsem, obuf, osem, gpost_ref, gt_ref, row, MACARON)


def _ffn_call(x, mod_l, mod_k, gpre, gpost, w_gu, w_down, l, *, tm, tf, row_div, row_const):
    m, d = x.shape
    d_ff = w_down.shape[1]
    n_f = d_ff // tf
    assert tm % FFN_SLAB == 0 and tm // ROW_CHUNK >= 2
    kern = functools.partial(_ffn_kernel, row_div=row_div, row_const=row_const, n_f=n_f)
    mod_spec = lambda k: pl.BlockSpec((8, d), lambda i, j: (0, k))
    return pl.pallas_call(
        kern,
        grid=(m // tm, n_f),
        in_specs=[
            pl.BlockSpec(memory_space=pl.ANY),
            mod_spec(mod_k), mod_spec(mod_k + 1), mod_spec(mod_k + 2),
            pl.BlockSpec((1, d), lambda i, j: (0, 0)),
            pl.BlockSpec((1, d), lambda i, j: (0, 0)),
            pl.BlockSpec((None, d, tf), lambda i, j: (l, 0, j)),
            pl.BlockSpec((None, d, tf), lambda i, j: (l, 0, n_f + j)),
            pl.BlockSpec((None, tf, d), lambda i, j: (l, j, 0)),
        ],
        out_specs=pl.BlockSpec(memory_space=pl.ANY),
        out_shape=jax.ShapeDtypeStruct((m, d), F32),
        scratch_shapes=[pltpu.VMEM((tm, d), BF), pltpu.VMEM((tm, d), F32),
                        pltpu.VMEM((2, ROW_CHUNK, d), F32), pltpu.VMEM((2, ROW_CHUNK, d), F32),
                        pltpu.SemaphoreType.DMA((2,)), pltpu.SemaphoreType.DMA((2,))],
        compiler_params=_params(("parallel", "arbitrary")),
        name="ffn",
    )(x, mod_l, mod_l, mod_l, gpre, gpost, w_gu, w_gu, w_down)


WIN_TN = 512
WIN_SLABS = 4
MERGE_TN = 256


def _gelu_tanh(z):
    return z * jax.nn.sigmoid((2.0 * 0.7978845608028654) * (z + 0.044715 * (z * z * z)))


def _win_kernel(x_ref, sh_ref, sc_ref, gpre_ref, w_ref, lb_ref, p16_ref, pf_ref, h_scr,
                *, row_div, row_const, kinds, layer):
    j = pl.program_id(1)
    row = _mod_row(row_div, row_const)

    @pl.when(j == 0)
    def _():
        _modulate_into(h_scr, x_ref, gpre_ref, sc_ref, sh_ref, row)

    def in_kind(kind):
        cond = jnp.bool_(False)
        t = 0
        while t < len(kinds):
            t1 = t
            while t1 + 1 < len(kinds) and kinds[t1 + 1] == kinds[t]:
                t1 += 1
            if kinds[t] == kind:
                cond = cond | ((j >= t) & (j <= t1))
            t = t1 + 1
        return cond

    def slabs(out_ref, fn):
        w = w_ref[...].astype(BF)
        tm = h_scr.shape[0]
        for r0 in range(0, tm, tm // WIN_SLABS):
            sl = slice(r0, r0 + tm // WIN_SLABS)
            out_ref[sl, :] = fn(_dot(h_scr[sl, :], w)).astype(out_ref.dtype)

    def logf(z):
        ll = lb_ref[...]
        e = jnp.exp(ll - jnp.max(ll, axis=0, keepdims=True))
        p = e / jnp.sum(e, axis=0, keepdims=True)
        lb = jnp.zeros_like(p[0:1])
        for r in range(1, layer + 1):
            lb = lb + p[r:r + 1]
        return jnp.log2(jnp.maximum(lb + (1.0 - lb) * jax.nn.sigmoid(z), F_FLOOR))

    for kind, out_ref, fn in (("gelu", p16_ref, _gelu_tanh), ("id", p16_ref, lambda z: z),
                              ("silu", p16_ref, lambda z: z * jax.nn.sigmoid(z)),
                              ("sigmoid", p16_ref, jax.nn.sigmoid), ("logf", pf_ref, logf)):
        if kind in kinds:
            pl.when(in_kind(kind))(functools.partial(slabs, out_ref, fn))


def _win_call(x, mod_l, mod_k, gpre, w_in, lb_logits2, l, *, tm, row_div, row_const, col0, kinds):
    m, d = x.shape
    tn = WIN_TN
    nj = len(kinds)
    is_f = np.array([k == "logf" for k in kinds])
    n16, nf = int((~is_f).sum()), int(is_f.sum())
    c16 = np.maximum(np.cumsum(~is_f) - 1, 0)
    cf = np.maximum(np.cumsum(is_f) - 1, 0)
    f0 = int(np.argmax(is_f))

    def sel(table):
        def f(j):
            out = jnp.int32(int(table[0]))
            for t in range(1, nj):
                if table[t] != table[t - 1]:
                    out = jnp.where(j >= t, jnp.int32(int(table[t])), out)
            return out
        return f

    s16, sf = sel(c16), sel(cf)
    kern = functools.partial(_win_kernel, row_div=row_div, row_const=row_const, kinds=tuple(kinds), layer=l)
    mod_spec = lambda k: pl.BlockSpec((8, d), lambda i, j: (0, k))
    depth = lb_logits2.shape[0]
    return pl.pallas_call(
        kern,
        grid=(m // tm, nj),
        in_specs=[
            pl.BlockSpec((tm, d), lambda i, j: (i, 0)),
            mod_spec(mod_k), mod_spec(mod_k + 1),
            pl.BlockSpec((1, d), lambda i, j: (0, 0)),
            pl.BlockSpec((None, d, tn), lambda i, j: (l, 0, col0 + j)),
            pl.BlockSpec((depth, tn), lambda i, j: (0, jnp.clip(j - f0, 0, nf - 1))),
        ],
        out_specs=[
            pl.BlockSpec((tm, tn), lambda i, j: (i, s16(j))),
            pl.BlockSpec((tm, tn), lambda i, j: (i, sf(j))),
        ],
        out_shape=[jax.ShapeDtypeStruct((m, n16 * tn), BF), jax.ShapeDtypeStruct((m, nf * tn), F32)],
        scratch_shapes=[pltpu.VMEM((tm, d), BF)],
        compiler_params=_params(("parallel", "arbitrary")),
        name="win",
    )(x, mod_l, mod_l, gpre, w_in, lb_logits2)


def _cmlp_kernel(u_ref, v_ref, g_ref, ws_ref, bs_ref, o_ref, *, n_chunks):
    v = v_ref[...].astype(F32)
    vc = v - jnp.mean(v, axis=-1, keepdims=True)
    vn = (vc * lax.rsqrt(jnp.mean(vc * vc, axis=-1, keepdims=True) + EPS) * g_ref[...]).astype(BF)
    dg = vn.shape[1] // GROUPS_A
    for g in range(GROUPS_A):
        w = ws_ref[g].astype(BF)
        bias = bs_ref[g]
        for c in range(n_chunks):
            rows = slice(c * CHUNK_A, (c + 1) * CHUNK_A)
            cols = slice(g * dg, (g + 1) * dg)
            sv = _dot(w, vn[rows, cols]) + bias
            o_ref[rows, cols] = (u_ref[rows, cols].astype(F32) * sv).astype(BF)


def _cmlp_call(p16, chunk_g, w_s, b_s, d_a, *, tm):
    m = p16.shape[0]
    n_chunks = tm // CHUNK_A
    return pl.pallas_call(
        functools.partial(_cmlp_kernel, n_chunks=n_chunks),
        grid=(m // tm,),
        in_specs=[
            pl.BlockSpec((tm, d_a), lambda i: (i, 0)),
            pl.BlockSpec((tm, d_a), lambda i: (i, 1)),
            pl.BlockSpec((1, d_a), lambda i: (0, 0)),
            pl.BlockSpec((GROUPS_A, CHUNK_A, CHUNK_A), lambda i: (0, 0, 0)),
            pl.BlockSpec((GROUPS_A, CHUNK_A, 1), lambda i: (0, 0, 0)),
        ],
        out_specs=pl.BlockSpec((tm, d_a), lambda i: (i, 0)),
        out_shape=jax.ShapeDtypeStruct((m, d_a), BF),
        compiler_params=_params(("parallel",)),
        name="cmlp",
    )(p16, p16, chunk_g, w_s, b_s[..., None])


def _hgrn_consts(fwd):
    c = HGRN_CHUNK
    t = np.arange(c)[:, None]
    u = np.arange(c)[None, :]

    def level_sum(b):
        half = b // 2
        mid = (t // b) * b + half
        upper = (t % b) >= half
        if fwd:
            return np.where(upper, (u >= mid) & (u <= t), (u > t) & (u < mid))
        return np.where(upper, (u >= mid) & (u < t), (u >= t) & (u < mid))

    def level_mask(b):
        half = b // 2
        upper = (t % b) >= half
        same = (t // b) == (u // b)
        if fwd:
            return same & upper & ((u % b) < half)
        return same & ~upper & ((u % b) >= half)

    lsum = np.concatenate([(u <= t) if fwd else (u >= t), level_sum(4), level_sum(8)], 0)
    masks = [t == u] + [level_mask(b) for b in HGRN_LEVELS] + [np.zeros((c, c), bool)]
    return (np.concatenate([lsum, lsum], 1).astype(np.float32),
            np.concatenate(masks, 1).astype(np.float32))


def _to_midpoint(cum, b, fwd):
    c, dk = cum.shape
    half = b // 2
    pieces = []
    for m in range(c // b):
        lo, mid = m * b, m * b + half
        ref = jnp.broadcast_to(cum[mid - 1:mid, :] if fwd else cum[mid:mid + 1, :], (half, dk))
        lower, upper = cum[lo:mid], cum[mid:lo + b]
        pieces += [ref - lower, upper - ref] if fwd else [lower - ref, ref - upper]
    return jnp.concatenate(pieces, axis=0)


def _hgrn_stage1(g, l2_ref):
    g_hi = g.astype(BF)
    g_lo = (g - g_hi.astype(F32)).astype(BF)
    return _dot(l2_ref[...], jnp.concatenate([g_hi, g_lo], axis=0))


def _hgrn_stage2(q, g, sums, mask_ref, fwd, with_out):
    c = HGRN_CHUNK
    dk = g.shape[1]
    f = jnp.exp2(g)
    k = 1.0 - f
    cum = sums[0:c]
    e_cum = jnp.exp2(cum)
    total = cum[c - 1:c] if fwd else cum[0:1]
    kend = (k * jnp.exp2(total - cum)).astype(BF)
    tot = e_cum[c - 1:c] if fwd else e_cum[0:1]
    if not with_out:
        return None, None, kend, tot

    row = lax.broadcasted_iota(jnp.int32, (c, dk), 0)
    e2 = jnp.where((row % 2) == (1 if fwd else 0), f, 1.0)
    es = [e2, jnp.exp2(sums[c:2 * c]), jnp.exp2(sums[2 * c:3 * c])]
    es += [jnp.exp2(_to_midpoint(cum, b, fwd)) for b in HGRN_LEVELS[3:]]
    qf = q.astype(F32)
    units = [(q, k.astype(BF))] + [((qf * e).astype(BF), (k * e).astype(BF)) for e in es]
    z = jnp.zeros((c, dk), BF)
    res = []
    for a in range(0, len(units) - 1, 2):
        (qa, ka), (qb, kb) = units[a], units[a + 1]
        kbd = jnp.concatenate([jnp.concatenate([ka, z], axis=1), jnp.concatenate([z, kb], axis=1)], axis=0)
        res.append(_dot_nt(jnp.concatenate([qa, qb], axis=1), kbd))
    q6, k6 = units[-1]
    res.append(_dot_nt(q6, jnp.concatenate([k6, z], axis=0)))
    p = sum(r * mask_ref[:, i * 2 * c:(i + 1) * 2 * c] for i, r in enumerate(res)).astype(BF)
    return p, (qf * e_cum).astype(BF), kend, tot


def _hgrn_stage3(p, qe, kend, tot, v, st, with_out):
    st_new = st * tot + _dot_tn(v, kend)
    if not with_out:
        return None, st_new
    o = _dot(p, jnp.concatenate([v, v], axis=0)) + _dot_nt(qe, st.astype(BF))
    return o, st_new


def _hgrn_kernel(*refs, n, with_out, with_init, with_final):
    it = iter(refs)
    q_ref = next(it) if with_out else None
    v_ref = next(it)
    gs_ref = next(it) if with_out else None
    gf_ref, gb_ref = next(it), next(it)
    gain_ref = next(it) if with_out else None
    sf0_ref, sb0_ref = (next(it), next(it)) if with_init else (None, None)
    l2_refs = (next(it), next(it))
    mask_refs = (next(it), next(it))
    y_ref = next(it) if with_out else None
    sf_ref, sb_ref = (next(it), next(it)) if with_final else (None, None)
    sums_scr, kend_scr, tot_scr, st_scr = next(it), next(it), next(it), next(it)
    p_scr, qe_scr, o_scr = (next(it), next(it), next(it)) if with_out else (None, None, None)

    c = HGRN_CHUNK
    nc = n // c
    dk = v_ref.shape[1]
    g_refs = (gf_ref, gb_ref)

    st_scr[0] = sf0_ref[...] if with_init else jnp.zeros((dk, dk), F32)
    st_scr[1] = sb0_ref[...] if with_init else jnp.zeros((dk, dk), F32)

    def rows_of(d, ci):
        start = ci * c if d == 0 else (nc - 1 - ci) * c
        return pl.ds(start if isinstance(start, int) else pl.multiple_of(start, c), c)

    def step(i, par, do1, do2, do3):
        for d in (0, 1):
            if do3:
                r = rows_of(d, i - 2)
                o, st = _hgrn_stage3(p_scr[d, par] if with_out else None,
                                     qe_scr[d, par] if with_out else None,
                                     kend_scr[d, par], tot_scr[d, par, 0:1, :],
                                     v_ref[r, :], st_scr[d], with_out)
                st_scr[d] = st
                if with_out:
                    o_scr[d, r, :] = o
            if do2:
                r = rows_of(d, i - 1)
                p, qe, kend, tot = _hgrn_stage2(q_ref[r, :] if with_out else None, g_refs[d][r, :],
                                                sums_scr[d, 1 - par], mask_refs[d], d == 0, with_out)
                kend_scr[d, 1 - par] = kend
                tot_scr[d, 1 - par, 0:1, :] = tot
                if with_out:
                    p_scr[d, 1 - par] = p
                    qe_scr[d, 1 - par] = qe
            if do1:
                sums_scr[d, par] = _hgrn_stage1(g_refs[d][rows_of(d, i), :], l2_refs[d])

    step(0, 0, True, False, False)
    step(1, 1, True, True, False)

    def body(ii, _):
        step(2 + 2 * ii, 0, True, True, True)
        step(3 + 2 * ii, 1, True, True, True)
        return 0

    lax.fori_loop(0, (nc - 2) // 2, body, 0)
    step(nc, nc % 2, False, True, True)
    step(nc + 1, (nc + 1) % 2, False, False, True)

    if with_final:
        sf_ref[...] = st_scr[0]
        sb_ref[...] = st_scr[1]

    if with_out:
        rows = 256

        def readout(ri, _):
            r = pl.ds(pl.multiple_of(ri * rows, rows), rows)
            o = o_scr[0, r, :] + o_scr[1, r, :]
            o = o * lax.rsqrt(jnp.mean(o * o, axis=-1, keepdims=True) + EPS) * gain_ref[...]
            y_ref[r, :] = (o * gs_ref[r, :].astype(F32)).astype(BF)
            return 0

        lax.fori_loop(0, n // rows, readout, 0)


def _hgrn_call(p16, pf, gain, init, n, cols, *, with_out, with_final):
    m = p16.shape[0]
    bsz = m // n
    dk = HGRN_DK
    with_init = init is not None
    cq, cv, cg = cols
    tok = lambda col: pl.BlockSpec((n, dk), lambda b, h: (b, col + h))
    st_spec = pl.BlockSpec((None, None, dk, dk), lambda b, h: (b, h, 0, 0))
    full = lambda a: pl.BlockSpec(a.shape, lambda b, h: (0,) * a.ndim)

    assert (n // HGRN_CHUNK) % 2 == 0 and n % HGRN_CHUNK == 0
    lf, mf = _hgrn_consts(True)
    lbw, mb = _hgrn_consts(False)
    consts = [jnp.asarray(lf, BF), jnp.asarray(lbw, BF), jnp.asarray(mf, F32), jnp.asarray(mb, F32)]

    args, specs = [], []
    if with_out:
        args.append(p16); specs.append(tok(cq))
    args.append(p16); specs.append(tok(cv))
    if with_out:
        args.append(p16); specs.append(tok(cg))
    args += [pf, pf]; specs += [tok(0), tok(HEADS_B)]
    if with_out:
        args.append(gain); specs.append(pl.BlockSpec((1, dk), lambda b, h: (0, h)))
    if with_init:
        args += list(init); specs += [st_spec, st_spec]
    args += consts; specs += [full(a) for a in consts]

    out_shape, out_specs = [], []
    if with_out:
        out_shape.append(jax.ShapeDtypeStruct((m, HEADS_B * dk), BF)); out_specs.append(tok(0))
    if with_final:
        st_shape = jax.ShapeDtypeStruct((bsz, HEADS_B, dk, dk), F32)
        out_shape += [st_shape, st_shape]; out_specs += [st_spec, st_spec]
    c = HGRN_CHUNK
    scratch = [pltpu.VMEM((2, 2, lf.shape[0], dk), F32), pltpu.VMEM((2, 2, c, dk), BF),
               pltpu.VMEM((2, 2, 8, dk), F32), pltpu.VMEM((2, dk, dk), F32)]
    if with_out:
        scratch += [pltpu.VMEM((2, 2, c, 2 * c), BF), pltpu.VMEM((2, 2, c, dk), BF),
                    pltpu.VMEM((2, n, dk), F32)]

    kern = functools.partial(_hgrn_kernel, n=n, with_out=with_out, with_init=with_init, with_final=with_final)
    return pl.pallas_call(
        kern,
        grid=(bsz, HEADS_B),
        in_specs=specs,
        out_specs=out_specs,
        out_shape=out_shape,
        scratch_shapes=scratch,
        compiler_params=_params(("parallel", "parallel")),
        name="hgrn",
    )(*args)


def _merge_kernel(ya_ref, yb_ref, ga_ref, gb_ref, wa_ref, wb_ref, wo_ref, x_ref, gt_ref, gpost_ref,
                  o_ref, *, row_div, row_const, n_j):
    j = pl.program_id(1)
    row = _mod_row(row_div, row_const)
    ma = _dot(ya_ref[...], wa_ref[...].astype(BF))
    mb = _dot(yb_ref[...], wb_ref[...].astype(BF))
    mm = (ga_ref[...].astype(F32) * ma + gb_ref[...].astype(F32) * mb).astype(BF)

    @pl.when(j == 0)
    def _():
        o_ref[...] = jnp.zeros_like(o_ref)

    o_ref[...] += _dot(mm, wo_ref[...].astype(BF))

    @pl.when(j == n_j - 1)
    def _():
        _residual_into(o_ref, x_ref, gpost_ref, gt_ref, row, 1.0)


def _merge_call(x, ya, yb, p16, gate_cols, mod_l, mod_k, gpost, w_up_a, w_up_b, w_out, l,
                *, tm, row_div, row_const):
    m, d = x.shape
    da, db = ya.shape[1], yb.shape[1]
    tn = MERGE_TN
    n_j = d // tn
    ca, cb = (c // tn for c in gate_cols)
    kern = functools.partial(_merge_kernel, row_div=row_div, row_const=row_const, n_j=n_j)
    return pl.pallas_call(
        kern,
        grid=(m // tm, n_j),
        in_specs=[
            pl.BlockSpec((tm, da), lambda i, j: (i, 0)),
            pl.BlockSpec((tm, db), lambda i, j: (i, 0)),
            pl.BlockSpec((tm, tn), lambda i, j: (i, ca + j)),
            pl.BlockSpec((tm, tn), lambda i, j: (i, cb + j)),
            pl.BlockSpec((None, da, tn), lambda i, j: (l, 0, j)),
            pl.BlockSpec((None, db, tn), lambda i, j: (l, 0, j)),
            pl.BlockSpec((None, tn, d), lambda i, j: (l, j, 0)),
            pl.BlockSpec((tm, d), lambda i, j: (i, 0), pipeline_mode=pl.Buffered(1)),
            pl.BlockSpec((8, d), lambda i, j: (0, mod_k)),
            pl.BlockSpec((1, d), lambda i, j: (0, 0)),
        ],
        out_specs=pl.BlockSpec((tm, d), lambda i, j: (i, 0)),
        out_shape=jax.ShapeDtypeStruct((m, d), F32),
        compiler_params=_params(("parallel", "arbitrary")),
        name="merge",
    )(ya, yb, p16, p16, w_up_a, w_up_b, w_out, x, mod_l, gpost)


_KINDS_FULL = (["gelu"] * 4 + ["id"] * 2 + ["logf"] * 4 + ["id"] * 2 + ["silu"] * 2 + ["sigmoid"] * 8)
_KINDS_STATE = ["logf"] * 4 + ["id"] * 2


def kernel(x, c, ctx, c_ctx, w_mod, b_mod, norm_g, ffn1_w_gu, ffn1_w_down, ffn2_w_gu, ffn2_w_down,
           w_in, chunk_norm_g, w_spatial, b_spatial, lb_logits, hgrn_norm_g, w_up_a, w_up_b, w_out):
    bsz, n, d = x.shape
    n_ctx = ctx.shape[1]
    depth = w_mod.shape[0]
    d_a = chunk_norm_g.shape[1]
    d_b = hgrn_norm_g.shape[1]
    ctx_row = bsz

    cs = jnp.concatenate([c, c_ctx[None, :], jnp.zeros((8 - bsz - 1, d), F32)], axis=0)
    mod = _mod_call(cs, w_mod, b_mod)
    lb2 = lb_logits.reshape(depth, -1)

    xl = x.reshape(bsz * n, d)
    xc = ctx.reshape(bsz * n_ctx, d)
    tm = 1024
    lat = dict(tm=tm, row_div=n // tm, row_const=None)
    cx = dict(tm=bsz * n_ctx, row_div=None, row_const=ctx_row)
    lat_ffn = dict(tm=n, row_div=1, row_const=None)
    col_q, col_i, col_g = 2 * d_a // 128, (2 * d_a + d_b) // 128, (2 * d_a + 2 * d_b) // 128
    gate_cols = (2 * d_a + 3 * d_b, 2 * d_a + 3 * d_b + d)

    for l in range(depth):
        last = l == depth - 1
        ml = mod[l]
        g = [norm_g[l, k][None, :] for k in range(norm_g.shape[1])]
        ffn1 = functools.partial(_ffn_call, mod_l=ml, mod_k=0, gpre=g[0], gpost=g[1],
                                 w_gu=ffn1_w_gu, w_down=ffn1_w_down, l=l, tf=256)
        ffn2 = functools.partial(_ffn_call, mod_l=ml, mod_k=6, gpre=g[4], gpost=g[5],
                                 w_gu=ffn2_w_gu, w_down=ffn2_w_down, l=l, tf=256)
        xl = ffn1(xl, **lat_ffn)
        xc = ffn1(xc, **cx)

        if not last:
            pc16, pcf = _win_call(xc, ml, 3, g[2], w_in, lb2, l, col0=0, kinds=_KINDS_FULL, **cx)
            ybc, s_f, s_b = _hgrn_call(pc16, pcf, hgrn_norm_g[l][None, :], None, n_ctx,
                                       (col_q, col_i, col_g), with_out=True, with_final=True)
        else:
            pc16, pcf = _win_call(xc, ml, 3, g[2], w_in, lb2, l, col0=(2 * d_a + d_b) // WIN_TN,
                                  kinds=_KINDS_STATE, **cx)
            s_f, s_b = _hgrn_call(pc16, pcf, None, None, n_ctx, (0, 0, 0), with_out=False, with_final=True)

        p16, pf = _win_call(xl, ml, 3, g[2], w_in, lb2, l, col0=0, kinds=_KINDS_FULL, **lat)
        (yb,) = _hgrn_call(p16, pf, hgrn_norm_g[l][None, :], (s_f, s_b), n,
                           (col_q, col_i, col_g), with_out=True, with_final=False)
        ya = _cmlp_call(p16, chunk_norm_g[l][None, :], w_spatial[l], b_spatial[l], d_a, tm=512)
        xl = _merge_call(xl, ya, yb, p16, gate_cols, ml, 5, g[3], w_up_a, w_up_b, w_out, l, **lat)
        xl = ffn2(xl, **lat_ffn)

        if not last:
            yac = _cmlp_call(pc16, chunk_norm_g[l][None, :], w_spatial[l], b_spatial[l], d_a, tm=512)
            xc = _merge_call(xc, yac, ybc, pc16, gate_cols, ml, 5, g[3], w_up_a, w_up_b, w_out, l, **cx)
            xc = ffn2(xc, **cx)
    return xl.reshape(bsz, n, d)
```

```python
import functools

import numpy as np
import jax
import jax.numpy as jnp
from jax import lax
from jax.experimental import pallas as pl
from jax.experimental.pallas import tpu as pltpu

BF = jnp.bfloat16
F32 = jnp.float32

EPS = 1e-6
MACARON = 0.5
F_FLOOR = 1e-30
N_MOD = 9
GROUPS_A = 8
CHUNK_A = 128
HEADS_B = 8
HGRN_DK = 128
HGRN_CHUNK = 64
HGRN_UNROLL = 4
HGRN_LEVELS = (2, 4, 8, 16, 32, 64)

VMEM_LIMIT = 56 * 1024 * 1024


def _dot(a, b):
    return jnp.dot(a, b, preferred_element_type=F32)


def _dot_nt(a, b):
    return lax.dot_general(a, b, (((1,), (1,)), ((), ())), preferred_element_type=F32)


def _dot_tn(a, b):
    return lax.dot_general(a, b, (((0,), (0,)), ((), ())), preferred_element_type=F32)


def _rms(x, g):
    return x * lax.rsqrt(jnp.mean(x * x, axis=-1, keepdims=True) + EPS) * g


def _params(sem):
    return pltpu.CompilerParams(dimension_semantics=sem, vmem_limit_bytes=VMEM_LIMIT)


def _mod_kernel(cs_ref, w_ref, b_ref, o_ref):
    cs = cs_ref[...]
    s = (cs * jax.nn.sigmoid(cs)).astype(BF)
    o_ref[...] = _dot(s, w_ref[...].astype(BF)) + b_ref[...]


def _mod_call(cs, w_mod, b_mod):
    depth, d, n = w_mod.shape
    tn = 1024
    return pl.pallas_call(
        _mod_kernel,
        grid=(depth, n // tn),
        in_specs=[
            pl.BlockSpec((8, d), lambda l, j: (0, 0)),
            pl.BlockSpec((None, d, tn), lambda l, j: (l, 0, j)),
            pl.BlockSpec((None, 1, tn), lambda l, j: (l, 0, j)),
        ],
        out_specs=pl.BlockSpec((None, 8, tn), lambda l, j: (l, 0, j)),
        out_shape=jax.ShapeDtypeStruct((depth, 8, n), F32),
        compiler_params=_params(("parallel", "parallel")),
        name="mod",
    )(cs, w_mod, b_mod.reshape(depth, 1, n))


def _mod_row(row_div, row_const):
    if row_div is None:
        return row_const
    return pl.program_id(0) // row_div


ROW_CHUNK = 64


def _row_loop(n_rows, fn):
    def body(r, _):
        fn(pl.ds(pl.multiple_of(r * ROW_CHUNK, ROW_CHUNK), ROW_CHUNK))
        return 0

    lax.fori_loop(0, n_rows // ROW_CHUNK, body, 0)


def _modulate_into(h_scr, x_ref, gpre_ref, sc_ref, sh_ref, row):
    gain = gpre_ref[...] * (1.0 + sc_ref[pl.ds(row, 1), :])
    shift = sh_ref[pl.ds(row, 1), :]

    def rows(sl):
        x = x_ref[sl, :]
        rs = lax.rsqrt(jnp.mean(x * x, axis=-1, keepdims=True) + EPS)
        h_scr[sl, :] = (x * rs * gain + shift).astype(BF)

    _row_loop(x_ref.shape[0], rows)


def _residual_into(o_ref, x_ref, gpost_ref, gt_ref, row, weight):
    gain = gpost_ref[...] * gt_ref[pl.ds(row, 1), :] * weight

    def rows(sl):
        y = o_ref[sl, :]
        rs = lax.rsqrt(jnp.mean(y * y, axis=-1, keepdims=True) + EPS)
        o_ref[sl, :] = x_ref[sl, :] + y * rs * gain

    _row_loop(o_ref.shape[0], rows)


def _x_copy(x_hbm, row0, r, xbuf, xsem, slot):
    rows = pl.ds(pl.multiple_of(row0 + r * ROW_CHUNK, ROW_CHUNK), ROW_CHUNK)
    return pltpu.make_async_copy(x_hbm.at[rows, :], xbuf.at[slot], xsem.at[slot])


def _modulate_stream(h_scr, acc_scr, x_hbm, row0, xbuf, xsem, gpre_ref, sc_ref, sh_ref, row):
    n = h_scr.shape[0] // ROW_CHUNK
    gain = gpre_ref[...] * (1.0 + sc_ref[pl.ds(row, 1), :])
    shift = sh_ref[pl.ds(row, 1), :]
    _x_copy(x_hbm, row0, 0, xbuf, xsem, 0).start()

    def body(r, _):
        slot = r % 2

        @pl.when(r + 1 < n)
        def _():
            _x_copy(x_hbm, row0, r + 1, xbuf, xsem, 1 - slot).start()

        _x_copy(x_hbm, row0, r, xbuf, xsem, slot).wait()
        x = xbuf[slot]
        rs = lax.rsqrt(jnp.mean(x * x, axis=-1, keepdims=True) + EPS)
        sl = pl.ds(pl.multiple_of(r * ROW_CHUNK, ROW_CHUNK), ROW_CHUNK)
        h_scr[sl, :] = (x * rs * gain + shift).astype(BF)
        acc_scr[sl, :] = jnp.zeros((ROW_CHUNK, acc_scr.shape[1]), F32)
        return 0

    lax.fori_loop(0, n, body, 0)


def _residual_stream(o_hbm, acc_scr, x_hbm, row0, xbuf, xsem, obuf, osem, gpost_ref, gt_ref, row, weight):
    n = acc_scr.shape[0] // ROW_CHUNK
    gain = gpost_ref[...] * gt_ref[pl.ds(row, 1), :] * weight

    def o_copy(r, slot):
        rows = pl.ds(pl.multiple_of(row0 + r * ROW_CHUNK, ROW_CHUNK), ROW_CHUNK)
        return pltpu.make_async_copy(obuf.at[slot], o_hbm.at[rows, :], osem.at[slot])

    _x_copy(x_hbm, row0, 0, xbuf, xsem, 0).start()

    def body(r, _):
        slot = r % 2

        @pl.when(r + 1 < n)
        def _():
            _x_copy(x_hbm, row0, r + 1, xbuf, xsem, 1 - slot).start()

        _x_copy(x_hbm, row0, r, xbuf, xsem, slot).wait()

        @pl.when(r >= 2)
        def _():
            o_copy(r - 2, slot).wait()

        y = acc_scr[pl.ds(pl.multiple_of(r * ROW_CHUNK, ROW_CHUNK), ROW_CHUNK), :]
        rs = lax.rsqrt(jnp.mean(y * y, axis=-1, keepdims=True) + EPS)
        obuf[slot] = xbuf[slot] + y * rs * gain
        o_copy(r, slot).start()
        return 0

    lax.fori_loop(0, n, body, 0)
    o_copy(n - 2, n % 2).wait()
    o_copy(n - 1, (n - 1) % 2).wait()


FFN_SLAB = 2048


def _ffn_kernel(x_hbm, sh_ref, sc_ref, gt_ref, gpre_ref, gpost_ref, wg_ref, wu_ref, wd_ref,
                o_hbm, h_scr, acc_scr, xbuf, obuf, xsem, osem, *, row_div, row_const, n_f):
    j = pl.program_id(1)
    row = _mod_row(row_div, row_const)
    tm = h_scr.shape[0]
    row0 = pl.program_id(0) * tm

    @pl.when(j == 0)
    def _():
        _modulate_stream(h_scr, acc_scr, x_hbm, row0, xbuf, xsem, gpre_ref, sc_ref, sh_ref, row)

    wg, wu, wd = wg_ref[...].astype(BF), wu_ref[...].astype(BF), wd_ref[...].astype(BF)
    slab = min(FFN_SLAB, tm)
    for r0 in range(0, tm, slab):
        sl = slice(r0, r0 + slab)
        h = h_scr[sl, :]
        a = _dot(h, wg)
        b = _dot(h, wu)
        acc_scr[sl, :] += _dot((a * jax.nn.sigmoid(a) * b).astype(BF), wd)

    @pl.when(j == n_f - 1)
    def _():
        _residual_stream(o_hbm, acc_scr, x_hbm, row0, xbuf, xsem, obuf, osem, gpost_ref, gt_ref, row, MACARON)


def _ffn_call(x, mod_l, mod_k, gpre, gpost, w_gu, w_down, l, *, tm, tf, row_div, row_const):
    m, d = x.shape
    d_ff = w_down.shape[1]
    n_f = d_ff // tf
    assert tm % min(FFN_SLAB, tm) == 0 and tm // ROW_CHUNK >= 2
    kern = functools.partial(_ffn_kernel, row_div=row_div, row_const=row_const, n_f=n_f)
    mod_spec = lambda k: pl.BlockSpec((8, d), lambda i, j: (0, k))
    return pl.pallas_call(
        kern,
        grid=(m // tm, n_f),
        in_specs=[
            pl.BlockSpec(memory_space=pl.ANY),
            mod_spec(mod_k), mod_spec(mod_k + 1), mod_spec(mod_k + 2),
            pl.BlockSpec((1, d), lambda i, j: (0, 0)),
            pl.BlockSpec((1, d), lambda i, j: (0, 0)),
            pl.BlockSpec((None, d, tf), lambda i, j: (l, 0, j)),
            pl.BlockSpec((None, d, tf), lambda i, j: (l, 0, n_f + j)),
            pl.BlockSpec((None, tf, d), lambda i, j: (l, j, 0)),
        ],
        out_specs=pl.BlockSpec(memory_space=pl.ANY),
        out_shape=jax.ShapeDtypeStruct((m, d), F32),
        scratch_shapes=[pltpu.VMEM((tm, d), BF), pltpu.VMEM((tm, d), F32),
                        pltpu.VMEM((2, ROW_CHUNK, d), F32), pltpu.VMEM((2, ROW_CHUNK, d), F32),
                        pltpu.SemaphoreType.DMA((2,)), pltpu.SemaphoreType.DMA((2,))],
        compiler_params=_params(("parallel", "arbitrary")),
        name="ffn",
    )(x, mod_l, mod_l, mod_l, gpre, gpost, w_gu, w_gu, w_down)


WIN_TN = 512
WIN_SLABS = 2
MERGE_TN = 512


def _gelu_tanh(z):
    return z * jax.nn.sigmoid((2.0 * 0.7978845608028654) * (z + 0.044715 * (z * z * z)))


def _win_kernel(x_ref, sh_ref, sc_ref, gpre_ref, w_ref, lb_ref, p16_ref, pf_ref, h_scr,
                *, row_div, row_const, kinds, layer):
    j = pl.program_id(1)
    row = _mod_row(row_div, row_const)

    @pl.when(j == 0)
    def _():
        _modulate_into(h_scr, x_ref, gpre_ref, sc_ref, sh_ref, row)

    def in_kind(kind):
        cond = jnp.bool_(False)
        t = 0
        while t < len(kinds):
            t1 = t
            while t1 + 1 < len(kinds) and kinds[t1 + 1] == kinds[t]:
                t1 += 1
            if kinds[t] == kind:
                cond = cond | ((j >= t) & (j <= t1))
            t = t1 + 1
        return cond

    def slabs(out_ref, fn):
        w = w_ref[...].astype(BF)
        tm = h_scr.shape[0]
        for r0 in range(0, tm, tm // WIN_SLABS):
            sl = slice(r0, r0 + tm // WIN_SLABS)
            out_ref[sl, :] = fn(_dot(h_scr[sl, :], w)).astype(out_ref.dtype)

    def logf(z):
        ll = lb_ref[...]
        e = jnp.exp(ll - jnp.max(ll, axis=0, keepdims=True))
        p = e / jnp.sum(e, axis=0, keepdims=True)
        lb = jnp.zeros_like(p[0:1])
        for r in range(1, layer + 1):
            lb = lb + p[r:r + 1]
        return jnp.log2(jnp.maximum(lb + (1.0 - lb) * jax.nn.sigmoid(z), F_FLOOR))

    for kind, out_ref, fn in (("gelu", p16_ref, _gelu_tanh), ("id", p16_ref, lambda z: z),
                              ("silu", p16_ref, lambda z: z * jax.nn.sigmoid(z)),
                              ("sigmoid", p16_ref, jax.nn.sigmoid), ("logf", pf_ref, logf)):
        if kind in kinds:
            pl.when(in_kind(kind))(functools.partial(slabs, out_ref, fn))


def _win_call(x, mod_l, mod_k, gpre, w_in, lb_logits2, l, *, tm, row_div, row_const, col0, kinds):
    m, d = x.shape
    tn = WIN_TN
    nj = len(kinds)
    is_f = np.array([k == "logf" for k in kinds])
    n16, nf = int((~is_f).sum()), int(is_f.sum())
    c16 = np.maximum(np.cumsum(~is_f) - 1, 0)
    cf = np.maximum(np.cumsum(is_f) - 1, 0)
    f0 = int(np.argmax(is_f))

    def sel(table):
        def f(j):
            out = jnp.int32(int(table[0]))
            for t in range(1, nj):
                if table[t] != table[t - 1]:
                    out = jnp.where(j >= t, jnp.int32(int(table[t])), out)
            return out
        return f

    s16, sf = sel(c16), sel(cf)
    kern = functools.partial(_win_kernel, row_div=row_div, row_const=row_const, kinds=tuple(kinds), layer=l)
    mod_spec = lambda k: pl.BlockSpec((8, d), lambda i, j: (0, k))
    depth = lb_logits2.shape[0]
    return pl.pallas_call(
        kern,
        grid=(m // tm, nj),
        in_specs=[
            pl.BlockSpec((tm, d), lambda i, j: (i, 0)),
            mod_spec(mod_k), mod_spec(mod_k + 1),
            pl.BlockSpec((1, d), lambda i, j: (0, 0)),
            pl.BlockSpec((None, d, tn), lambda i, j: (l, 0, col0 + j)),
            pl.BlockSpec((depth, tn), lambda i, j: (0, jnp.clip(j - f0, 0, nf - 1))),
        ],
        out_specs=[
            pl.BlockSpec((tm, tn), lambda i, j: (i, s16(j))),
            pl.BlockSpec((tm, tn), lambda i, j: (i, sf(j))),
        ],
        out_shape=[jax.ShapeDtypeStruct((m, n16 * tn), BF), jax.ShapeDtypeStruct((m, nf * tn), F32)],
        scratch_shapes=[pltpu.VMEM((tm, d), BF)],
        compiler_params=_params(("parallel", "arbitrary")),
        name="win",
    )(x, mod_l, mod_l, gpre, w_in, lb_logits2)


def _cmlp_kernel(u_ref, v_ref, g_ref, ws_ref, bs_ref, o_ref, *, n_chunks):
    v = v_ref[...].astype(F32)
    vc = v - jnp.mean(v, axis=-1, keepdims=True)
    vn = (vc * lax.rsqrt(jnp.mean(vc * vc, axis=-1, keepdims=True) + EPS) * g_ref[...]).astype(BF)
    dg = vn.shape[1] // GROUPS_A
    for g in range(GROUPS_A):
        w = ws_ref[g].astype(BF)
        bias = bs_ref[g]
        for c in range(n_chunks):
            rows = slice(c * CHUNK_A, (c + 1) * CHUNK_A)
            cols = slice(g * dg, (g + 1) * dg)
            sv = _dot(w, vn[rows, cols]) + bias
            o_ref[rows, cols] = (u_ref[rows, cols].astype(F32) * sv).astype(BF)


def _cmlp_call(p16, chunk_g, w_s, b_s, d_a, *, tm):
    m = p16.shape[0]
    n_chunks = tm // CHUNK_A
    return pl.pallas_call(
        functools.partial(_cmlp_kernel, n_chunks=n_chunks),
        grid=(m // tm,),
        in_specs=[
            pl.BlockSpec((tm, d_a), lambda i: (i, 0)),
            pl.BlockSpec((tm, d_a), lambda i: (i, 1)),
            pl.BlockSpec((1, d_a), lambda i: (0, 0)),
            pl.BlockSpec((GROUPS_A, CHUNK_A, CHUNK_A), lambda i: (0, 0, 0)),
            pl.BlockSpec((GROUPS_A, CHUNK_A, 1), lambda i: (0, 0, 0)),
        ],
        out_specs=pl.BlockSpec((tm, d_a), lambda i: (i, 0)),
        out_shape=jax.ShapeDtypeStruct((m, d_a), BF),
        compiler_params=_params(("parallel",)),
        name="cmlp",
    )(p16, p16, chunk_g, w_s, b_s[..., None])


def _hgrn_consts(fwd):
    c = HGRN_CHUNK
    t = np.arange(c)[:, None]
    u = np.arange(c)[None, :]

    def level_sum(b):
        half = b // 2
        mid = (t // b) * b + half
        upper = (t % b) >= half
        if fwd:
            return np.where(upper, (u >= mid) & (u <= t), (u > t) & (u < mid))
        return np.where(upper, (u >= mid) & (u < t), (u >= t) & (u < mid))

    def level_mask(b):
        half = b // 2
        upper = (t % b) >= half
        same = (t // b) == (u // b)
        if fwd:
            return same & upper & ((u % b) < half)
        return same & ~upper & ((u % b) >= half)

    lsum = np.concatenate([(u <= t) if fwd else (u >= t), level_sum(4), level_sum(8)], 0)
    masks = [t == u] + [level_mask(b) for b in HGRN_LEVELS] + [np.zeros((c, c), bool)]
    return (np.concatenate([lsum, lsum], 1).astype(np.float32),
            np.concatenate(masks, 1).astype(np.float32))


def _to_midpoint(cum, b, fwd):
    c, dk = cum.shape
    half = b // 2
    pieces = []
    for m in range(c // b):
        lo, mid = m * b, m * b + half
        ref = jnp.broadcast_to(cum[mid - 1:mid, :] if fwd else cum[mid:mid + 1, :], (half, dk))
        lower, upper = cum[lo:mid], cum[mid:lo + b]
        pieces += [ref - lower, upper - ref] if fwd else [lower - ref, ref - upper]
    return jnp.concatenate(pieces, axis=0)


def _hgrn_stage1(g, l2_ref):
    g_hi = g.astype(BF)
    g_lo = (g - g_hi.astype(F32)).astype(BF)
    return _dot(l2_ref[...], jnp.concatenate([g_hi, g_lo], axis=0))


def _hgrn_stage2(q, g, sums, mask_ref, fwd, with_out):
    c = HGRN_CHUNK
    dk = g.shape[1]
    f = jnp.exp2(g)
    k = 1.0 - f
    cum = sums[0:c]
    e_cum = jnp.exp2(cum)
    total = cum[c - 1:c] if fwd else cum[0:1]
    kend = (k * jnp.exp2(total - cum)).astype(BF)
    tot = e_cum[c - 1:c] if fwd else e_cum[0:1]
    if not with_out:
        return None, None, kend, tot

    row = lax.broadcasted_iota(jnp.int32, (c, dk), 0)
    e2 = jnp.where((row % 2) == (1 if fwd else 0), f, 1.0)
    es = [e2, jnp.exp2(sums[c:2 * c]), jnp.exp2(sums[2 * c:3 * c])]
    es += [jnp.exp2(_to_midpoint(cum, b, fwd)) for b in HGRN_LEVELS[3:]]
    qf = q.astype(F32)
    units = [(q, k.astype(BF))] + [((qf * e).astype(BF), (k * e).astype(BF)) for e in es]
    z = jnp.zeros((c, dk), BF)
    res = []
    for a in range(0, len(units) - 1, 2):
        (qa, ka), (qb, kb) = units[a], units[a + 1]
        kbd = jnp.concatenate([jnp.concatenate([ka, z], axis=1), jnp.concatenate([z, kb], axis=1)], axis=0)
        res.append(_dot_nt(jnp.concatenate([qa, qb], axis=1), kbd))
    q6, k6 = units[-1]
    res.append(_dot_nt(q6, jnp.concatenate([k6, z], axis=0)))
    p = sum(r * mask_ref[:, i * 2 * c:(i + 1) * 2 * c] for i, r in enumerate(res)).astype(BF)
    return p, (qf * e_cum).astype(BF), kend, tot


def _hgrn_stage3(p, qe, kend, tot, v, st, with_out):
    st_new = st * tot + _dot_tn(v, kend)
    if not with_out:
        return None, st_new
    o = _dot(p, jnp.concatenate([v, v], axis=0)) + _dot_nt(qe, st.astype(BF))
    return o, st_new


def _hgrn_kernel(*refs, n, with_out, with_init, with_final):
    it = iter(refs)
    q_ref = next(it) if with_out else None
    v_ref = next(it)
    gs_ref = next(it) if with_out else None
    gf_ref, gb_ref = next(it), next(it)
    gain_ref = next(it) if with_out else None
    sf0_ref, sb0_ref = (next(it), next(it)) if with_init else (None, None)
    l2_refs = (next(it), next(it))
    mask_refs = (next(it), next(it))
    y_ref = next(it) if with_out else None
    sf_ref, sb_ref = (next(it), next(it)) if with_final else (None, None)
    sums_scr, kend_scr, tot_scr, st_scr = next(it), next(it), next(it), next(it)
    p_scr, qe_scr, o_scr = (next(it), next(it), next(it)) if with_out else (None, None, None)

    c = HGRN_CHUNK
    nc = n // c
    dk = v_ref.shape[1]
    g_refs = (gf_ref, gb_ref)

    st_scr[0] = sf0_ref[...] if with_init else jnp.zeros((dk, dk), F32)
    st_scr[1] = sb0_ref[...] if with_init else jnp.zeros((dk, dk), F32)

    def rows_of(d, ci):
        start = ci * c if d == 0 else (nc - 1 - ci) * c
        return pl.ds(start if isinstance(start, int) else pl.multiple_of(start, c), c)

    def step(i, par, do1, do2, do3):
        for d in (0, 1):
            if do3:
                r = rows_of(d, i - 2)
                o, st = _hgrn_stage3(p_scr[d, par] if with_out else None,
                                     qe_scr[d, par] if with_out else None,
                                     kend_scr[d, par], tot_scr[d, par, 0:1, :],
                                     v_ref[r, :], st_scr[d], with_out)
                st_scr[d] = st
                if with_out:
                    o_scr[d, r, :] = o
            if do2:
                r = rows_of(d, i - 1)
                p, qe, kend, tot = _hgrn_stage2(q_ref[r, :] if with_out else None, g_refs[d][r, :],
                                                sums_scr[d, 1 - par], mask_refs[d], d == 0, with_out)
                kend_scr[d, 1 - par] = kend
                tot_scr[d, 1 - par, 0:1, :] = tot
                if with_out:
                    p_scr[d, 1 - par] = p
                    qe_scr[d, 1 - par] = qe
            if do1:
                sums_scr[d, par] = _hgrn_stage1(g_refs[d][rows_of(d, i), :], l2_refs[d])

    step(0, 0, True, False, False)
    step(1, 1, True, True, False)

    n_loop = (nc - 2) // HGRN_UNROLL

    def body(ii, _):
        for u in range(HGRN_UNROLL):
            step(2 + HGRN_UNROLL * ii + u, u % 2, True, True, True)
        return 0

    lax.fori_loop(0, n_loop, body, 0)
    for i in range(2 + HGRN_UNROLL * n_loop, nc):
        step(i, i % 2, True, True, True)
    step(nc, nc % 2, False, True, True)
    step(nc + 1, (nc + 1) % 2, False, False, True)

    if with_final:
        sf_ref[...] = st_scr[0]
        sb_ref[...] = st_scr[1]

    if with_out:
        rows = 256

        def readout(ri, _):
            r = pl.ds(pl.multiple_of(ri * rows, rows), rows)
            o = o_scr[0, r, :] + o_scr[1, r, :]
            o = o * lax.rsqrt(jnp.mean(o * o, axis=-1, keepdims=True) + EPS) * gain_ref[...]
            y_ref[r, :] = (o * gs_ref[r, :].astype(F32)).astype(BF)
            return 0

        lax.fori_loop(0, n // rows, readout, 0)


def _hgrn_call(p16, pf, gain, init, n, cols, *, with_out, with_final):
    m = p16.shape[0]
    bsz = m // n
    dk = HGRN_DK
    with_init = init is not None
    cq, cv, cg = cols
    tok = lambda col: pl.BlockSpec((n, dk), lambda b, h: (b, col + h))
    st_spec = pl.BlockSpec((None, None, dk, dk), lambda b, h: (b, h, 0, 0))
    full = lambda a: pl.BlockSpec(a.shape, lambda b, h: (0,) * a.ndim)

    assert (n // HGRN_CHUNK) % 2 == 0 and n % HGRN_CHUNK == 0
    lf, mf = _hgrn_consts(True)
    lbw, mb = _hgrn_consts(False)
    consts = [jnp.asarray(lf, BF), jnp.asarray(lbw, BF), jnp.asarray(mf, F32), jnp.asarray(mb, F32)]

    args, specs = [], []
    if with_out:
        args.append(p16); specs.append(tok(cq))
    args.append(p16); specs.append(tok(cv))
    if with_out:
        args.append(p16); specs.append(tok(cg))
    args += [pf, pf]; specs += [tok(0), tok(HEADS_B)]
    if with_out:
        args.append(gain); specs.append(pl.BlockSpec((1, dk), lambda b, h: (0, h)))
    if with_init:
        args += list(init); specs += [st_spec, st_spec]
    args += consts; specs += [full(a) for a in consts]

    out_shape, out_specs = [], []
    if with_out:
        out_shape.append(jax.ShapeDtypeStruct((m, HEADS_B * dk), BF)); out_specs.append(tok(0))
    if with_final:
        st_shape = jax.ShapeDtypeStruct((bsz, HEADS_B, dk, dk), F32)
        out_shape += [st_shape, st_shape]; out_specs += [st_spec, st_spec]
    c = HGRN_CHUNK
    scratch = [pltpu.VMEM((2, 2, lf.shape[0], dk), F32), pltpu.VMEM((2, 2, c, dk), BF),
               pltpu.VMEM((2, 2, 8, dk), F32), pltpu.VMEM((2, dk, dk), F32)]
    if with_out:
        scratch += [pltpu.VMEM((2, 2, c, 2 * c), BF), pltpu.VMEM((2, 2, c, dk), BF),
                    pltpu.VMEM((2, n, dk), F32)]

    kern = functools.partial(_hgrn_kernel, n=n, with_out=with_out, with_init=with_init, with_final=with_final)
    return pl.pallas_call(
        kern,
        grid=(bsz, HEADS_B),
        in_specs=specs,
        out_specs=out_specs,
        out_shape=out_shape,
        scratch_shapes=scratch,
        compiler_params=_params(("parallel", "parallel")),
        name="hgrn",
    )(*args)


def _merge_kernel(ya_ref, yb_ref, ga_ref, gb_ref, wa_ref, wb_ref, wo_ref, x_hbm, gt_ref, gpost_ref,
                  o_hbm, acc_scr, xbuf, obuf, xsem, osem, *, row_div, row_const, n_j):
    j = pl.program_id(1)
    row = _mod_row(row_div, row_const)
    row0 = pl.program_id(0) * acc_scr.shape[0]
    ma = _dot(ya_ref[...], wa_ref[...].astype(BF))
    mb = _dot(yb_ref[...], wb_ref[...].astype(BF))
    mm = (ga_ref[...].astype(F32) * ma + gb_ref[...].astype(F32) * mb).astype(BF)

    @pl.when(j == 0)
    def _():
        acc_scr[...] = jnp.zeros_like(acc_scr)

    acc_scr[...] += _dot(mm, wo_ref[...].astype(BF))

    @pl.when(j == n_j - 1)
    def _():
        _residual_stream(o_hbm, acc_scr, x_hbm, row0, xbuf, xsem, obuf, osem, gpost_ref, gt_ref, row, 1.0)


def _merge_call(x, ya, yb, p16, gate_cols, mod_l, mod_k, gpost, w_up_a, w_up_b, w_out, l,
                *, tm, row_div, row_const):
    m, d = x.shape
    da, db = ya.shape[1], yb.shape[1]
    tn = MERGE_TN
    n_j = d // tn
    ca, cb = (c // tn for c in gate_cols)
    kern = functools.partial(_merge_kernel, row_div=row_div, row_const=row_const, n_j=n_j)
    return pl.pallas_call(
        kern,
        grid=(m // tm, n_j),
        in_specs=[
            pl.BlockSpec((tm, da), lambda i, j: (i, 0)),
            pl.BlockSpec((tm, db), lambda i, j: (i, 0)),
            pl.BlockSpec((tm, tn), lambda i, j: (i, ca + j)),
            pl.BlockSpec((tm, tn), lambda i, j: (i, cb + j)),
            pl.BlockSpec((None, da, tn), lambda i, j: (l, 0, j)),
            pl.BlockSpec((None, db, tn), lambda i, j: (l, 0, j)),
            pl.BlockSpec((None, tn, d), lambda i, j: (l, j, 0)),
            pl.BlockSpec(memory_space=pl.ANY),
            pl.BlockSpec((8, d), lambda i, j: (0, mod_k)),
            pl.BlockSpec((1, d), lambda i, j: (0, 0)),
        ],
        out_specs=pl.BlockSpec(memory_space=pl.ANY),
        out_shape=jax.ShapeDtypeStruct((m, d), F32),
        scratch_shapes=[pltpu.VMEM((tm, d), F32),
                        pltpu.VMEM((2, ROW_CHUNK, d), F32), pltpu.VMEM((2, ROW_CHUNK, d), F32),
                        pltpu.SemaphoreType.DMA((2,)), pltpu.SemaphoreType.DMA((2,))],
        compiler_params=_params(("parallel", "arbitrary")),
        name="merge",
    )(ya, yb, p16, p16, w_up_a, w_up_b, w_out, x, mod_l, gpost)


_KINDS_FULL = (["gelu"] * 4 + ["id"] * 2 + ["logf"] * 4 + ["id"] * 2 + ["silu"] * 2 + ["sigmoid"] * 8)
_KINDS_STATE = ["logf"] * 4 + ["id"] * 2


def kernel(x, c, ctx, c_ctx, w_mod, b_mod, norm_g, ffn1_w_gu, ffn1_w_down, ffn2_w_gu, ffn2_w_down,
           w_in, chunk_norm_g, w_spatial, b_spatial, lb_logits, hgrn_norm_g, w_up_a, w_up_b, w_out):
    bsz, n, d = x.shape
    n_ctx = ctx.shape[1]
    depth = w_mod.shape[0]
    d_a = chunk_norm_g.shape[1]
    d_b = hgrn_norm_g.shape[1]
    ctx_row = bsz

    cs = jnp.concatenate([c, c_ctx[None, :], jnp.zeros((8 - bsz - 1, d), F32)], axis=0)
    mod = _mod_call(cs, w_mod, b_mod)
    lb2 = lb_logits.reshape(depth, -1)

    xl = x.reshape(bsz * n, d)
    xc = ctx.reshape(bsz * n_ctx, d)
    tm = 1024
    lat = dict(tm=tm, row_div=n // tm, row_const=None)
    cx = dict(tm=bsz * n_ctx, row_div=None, row_const=ctx_row)
    lat_ffn = dict(tm=n, row_div=1, row_const=None)
    col_q, col_i, col_g = 2 * d_a // 128, (2 * d_a + d_b) // 128, (2 * d_a + 2 * d_b) // 128
    gate_cols = (2 * d_a + 3 * d_b, 2 * d_a + 3 * d_b + d)

    for l in range(depth):
        last = l == depth - 1
        ml = mod[l]
        g = [norm_g[l, k][None, :] for k in range(norm_g.shape[1])]
        ffn1 = functools.partial(_ffn_call, mod_l=ml, mod_k=0, gpre=g[0], gpost=g[1],
                                 w_gu=ffn1_w_gu, w_down=ffn1_w_down, l=l, tf=256)
        ffn2 = functools.partial(_ffn_call, mod_l=ml, mod_k=6, gpre=g[4], gpost=g[5],
                                 w_gu=ffn2_w_gu, w_down=ffn2_w_down, l=l, tf=256)
        xl = ffn1(xl, **lat_ffn)
        xc = ffn1(xc, **cx)

        if not last:
            pc16, pcf = _win_call(xc, ml, 3, g[2], w_in, lb2, l, col0=0, kinds=_KINDS_FULL, **cx)
            ybc, s_f, s_b = _hgrn_call(pc16, pcf, hgrn_norm_g[l][None, :], None, n_ctx,
                                       (col_q, col_i, col_g), with_out=True, with_final=True)
        else:
            pc16, pcf = _win_call(xc, ml, 3, g[2], w_in, lb2, l, col0=(2 * d_a + d_b) // WIN_TN,
                                  kinds=_KINDS_STATE, **cx)
            s_f, s_b = _hgrn_call(pc16, pcf, None, None, n_ctx, (0, 0, 0), with_out=False, with_final=True)

        p16, pf = _win_call(xl, ml, 3, g[2], w_in, lb2, l, col0=0, kinds=_KINDS_FULL, **lat)
        (yb,) = _hgrn_call(p16, pf, hgrn_norm_g[l][None, :], (s_f, s_b), n,
                           (col_q, col_i, col_g), with_out=True, with_final=False)
        ya = _cmlp_call(p16, chunk_norm_g[l][None, :], w_spatial[l], b_spatial[l], d_a, tm=512)
        xl = _merge_call(xl, ya, yb, p16, gate_cols, ml, 5, g[3], w_up_a, w_up_b, w_out, l, **lat)
        xl = ffn2(xl, **lat_ffn)

        if not last:
            yac = _cmlp_call(pc16, chunk_norm_g[l][None, :], w_spatial[l], b_spatial[l], d_a, tm=512)
            xc = _merge_call(xc, yac, ybc, pc16, gate_cols, ml, 5, g[3], w_up_a, w_up_b, w_out, l, **cx)
            xc = ffn2(xc, **cx)
    return xl.reshape(bsz, n, d)
```

```python
import functools

import numpy as np
import jax
import jax.numpy as jnp
from jax import lax
from jax.experimental import pallas as pl
from jax.experimental.pallas import tpu as pltpu

BF = jnp.bfloat16
F32 = jnp.float32

EPS = 1e-6
MACARON = 0.5
F_FLOOR = 1e-30
N_MOD = 9
GROUPS_A = 8
CHUNK_A = 128
HEADS_B = 8
HGRN_DK = 128
HGRN_CHUNK = 64
HGRN_UNROLL = 10
HGRN_LEVELS = (2, 4, 8, 16, 32, 64)

VMEM_LIMIT = 56 * 1024 * 1024


def _dot(a, b):
    return jnp.dot(a, b, preferred_element_type=F32)


def _dot_nt(a, b):
    return lax.dot_general(a, b, (((1,), (1,)), ((), ())), preferred_element_type=F32)


def _dot_tn(a, b):
    return lax.dot_general(a, b, (((0,), (0,)), ((), ())), preferred_element_type=F32)


def _rms(x, g):
    return x * lax.rsqrt(jnp.mean(x * x, axis=-1, keepdims=True) + EPS) * g


def _params(sem):
    return pltpu.CompilerParams(dimension_semantics=sem, vmem_limit_bytes=VMEM_LIMIT)


def _mod_kernel(cs_ref, w_ref, b_ref, o_ref):
    cs = cs_ref[...]
    s = (cs * jax.nn.sigmoid(cs)).astype(BF)
    o_ref[...] = _dot(s, w_ref[...].astype(BF)) + b_ref[...]


def _mod_call(cs, w_mod, b_mod):
    depth, d, n = w_mod.shape
    tn = 1024
    return pl.pallas_call(
        _mod_kernel,
        grid=(depth, n // tn),
        in_specs=[
            pl.BlockSpec((8, d), lambda l, j: (0, 0)),
            pl.BlockSpec((None, d, tn), lambda l, j: (l, 0, j)),
            pl.BlockSpec((None, 1, tn), lambda l, j: (l, 0, j)),
        ],
        out_specs=pl.BlockSpec((None, 8, tn), lambda l, j: (l, 0, j)),
        out_shape=jax.ShapeDtypeStruct((depth, 8, n), F32),
        compiler_params=_params(("parallel", "parallel")),
        name="mod",
    )(cs, w_mod, b_mod.reshape(depth, 1, n))


def _mod_row(row_div, row_const):
    if row_div is None:
        return row_const
    return pl.program_id(0) // row_div


ROW_CHUNK = 64


def _row_loop(n_rows, fn):
    def body(r, _):
        fn(pl.ds(pl.multiple_of(r * ROW_CHUNK, ROW_CHUNK), ROW_CHUNK))
        return 0

    lax.fori_loop(0, n_rows // ROW_CHUNK, body, 0, unroll=2)


def _modulate_into(h_scr, x_ref, gpre_ref, sc_ref, sh_ref, row):
    gain = gpre_ref[...] * (1.0 + sc_ref[pl.ds(row, 1), :])
    shift = sh_ref[pl.ds(row, 1), :]

    def rows(sl):
        x = x_ref[sl, :]
        rs = lax.rsqrt(jnp.mean(x * x, axis=-1, keepdims=True) + EPS)
        h_scr[sl, :] = (x * rs * gain + shift).astype(BF)

    _row_loop(x_ref.shape[0], rows)


def _residual_into(o_ref, x_ref, gpost_ref, gt_ref, row, weight):
    gain = gpost_ref[...] * gt_ref[pl.ds(row, 1), :] * weight

    def rows(sl):
        y = o_ref[sl, :]
        rs = lax.rsqrt(jnp.mean(y * y, axis=-1, keepdims=True) + EPS)
        o_ref[sl, :] = x_ref[sl, :] + y * rs * gain

    _row_loop(o_ref.shape[0], rows)


def _ffn_kernel(x_ref, sh_ref, sc_ref, gt_ref, gpre_ref, gpost_ref, wg_ref, wu_ref, wd_ref,
                o_ref, h_scr, *, row_div, row_const, n_f):
    j = pl.program_id(1)
    row = _mod_row(row_div, row_const)

    @pl.when(j == 0)
    def _():
        _modulate_into(h_scr, x_ref, gpre_ref, sc_ref, sh_ref, row)
        o_ref[...] = jnp.zeros_like(o_ref)

    h = h_scr[...]
    a = _dot(h, wg_ref[...].astype(BF))
    b = _dot(h, wu_ref[...].astype(BF))
    act = (a * jax.nn.sigmoid(a) * b).astype(BF)
    o_ref[...] += _dot(act, wd_ref[...].astype(BF))

    @pl.when(j == n_f - 1)
    def _():
        _residual_into(o_ref, x_ref, gpost_ref, gt_ref, row, MACARON)


def _ffn_call(x, mod_l, mod_k, gpre, gpost, w_gu, w_down, l, *, tm, tf, row_div, row_const):
    m, d = x.shape
    d_ff = w_down.shape[1]
    n_f = d_ff // tf
    kern = functools.partial(_ffn_kernel, row_div=row_div, row_const=row_const, n_f=n_f)
    mod_spec = lambda k: pl.BlockSpec((8, d), lambda i, j: (0, k))
    return pl.pallas_call(
        kern,
        grid=(m // tm, n_f),
        in_specs=[
            pl.BlockSpec((tm, d), lambda i, j: (i, 0), pipeline_mode=pl.Buffered(1)),
            mod_spec(mod_k), mod_spec(mod_k + 1), mod_spec(mod_k + 2),
            pl.BlockSpec((1, d), lambda i, j: (0, 0)),
            pl.BlockSpec((1, d), lambda i, j: (0, 0)),
            pl.BlockSpec((None, d, tf), lambda i, j: (l, 0, j)),
            pl.BlockSpec((None, d, tf), lambda i, j: (l, 0, n_f + j)),
            pl.BlockSpec((None, tf, d), lambda i, j: (l, j, 0)),
        ],
        out_specs=pl.BlockSpec((tm, d), lambda i, j: (i, 0)),
        out_shape=jax.ShapeDtypeStruct((m, d), F32),
        scratch_shapes=[pltpu.VMEM((tm, d), BF)],
        compiler_params=_params(("parallel", "arbitrary")),
        name="ffn",
    )(x, mod_l, mod_l, mod_l, gpre, gpost, w_gu, w_gu, w_down)


WIN_TN = 512
WIN_SLABS = 4
MERGE_TN = 256


def _gelu_tanh(z):
    return z * jax.nn.sigmoid((2.0 * 0.7978845608028654) * (z + 0.044715 * (z * z * z)))


def _win_kernel(x_ref, sh_ref, sc_ref, gpre_ref, w_ref, lb_ref, p16_ref, pf_ref, h_scr,
                *, row_div, row_const, kinds, layer):
    j = pl.program_id(1)
    row = _mod_row(row_div, row_const)

    @pl.when(j == 0)
    def _():
        _modulate_into(h_scr, x_ref, gpre_ref, sc_ref, sh_ref, row)

    def in_kind(kind):
        cond = jnp.bool_(False)
        t = 0
        while t < len(kinds):
            t1 = t
            while t1 + 1 < len(kinds) and kinds[t1 + 1] == kinds[t]:
                t1 += 1
            if kinds[t] == kind:
                cond = cond | ((j >= t) & (j <= t1))
            t = t1 + 1
        return cond

    def slabs(out_ref, fn):
        w = w_ref[...].astype(BF)
        tm = h_scr.shape[0]
        for r0 in range(0, tm, tm // WIN_SLABS):
            sl = slice(r0, r0 + tm // WIN_SLABS)
            out_ref[sl, :] = fn(_dot(h_scr[sl, :], w)).astype(out_ref.dtype)

    def logf(z):
        ll = lb_ref[...]
        e = jnp.exp(ll - jnp.max(ll, axis=0, keepdims=True))
        p = e / jnp.sum(e, axis=0, keepdims=True)
        lb = jnp.zeros_like(p[0:1])
        for r in range(1, layer + 1):
            lb = lb + p[r:r + 1]
        return jnp.log2(jnp.maximum(lb + (1.0 - lb) * jax.nn.sigmoid(z), F_FLOOR))

    for kind, out_ref, fn in (("gelu", p16_ref, _gelu_tanh), ("id", p16_ref, lambda z: z),
                              ("silu", p16_ref, lambda z: z * jax.nn.sigmoid(z)),
                              ("sigmoid", p16_ref, jax.nn.sigmoid), ("logf", pf_ref, logf)):
        if kind in kinds:
            pl.when(in_kind(kind))(functools.partial(slabs, out_ref, fn))


def _win_call(x, mod_l, mod_k, gpre, w_in, lb_logits2, l, *, tm, row_div, row_const, col0, kinds):
    m, d = x.shape
    tn = WIN_TN
    nj = len(kinds)
    is_f = np.array([k == "logf" for k in kinds])
    n16, nf = int((~is_f).sum()), int(is_f.sum())
    c16 = np.maximum(np.cumsum(~is_f) - 1, 0)
    cf = np.maximum(np.cumsum(is_f) - 1, 0)
    f0 = int(np.argmax(is_f))

    def sel(table):
        def f(j):
            out = jnp.int32(int(table[0]))
            for t in range(1, nj):
                if table[t] != table[t - 1]:
                    out = jnp.where(j >= t, jnp.int32(int(table[t])), out)
            return out
        return f

    s16, sf = sel(c16), sel(cf)
    kern = functools.partial(_win_kernel, row_div=row_div, row_const=row_const, kinds=tuple(kinds), layer=l)
    mod_spec = lambda k: pl.BlockSpec((8, d), lambda i, j: (0, k))
    depth = lb_logits2.shape[0]
    return pl.pallas_call(
        kern,
        grid=(m // tm, nj),
        in_specs=[
            pl.BlockSpec((tm, d), lambda i, j: (i, 0)),
            mod_spec(mod_k), mod_spec(mod_k + 1),
            pl.BlockSpec((1, d), lambda i, j: (0, 0)),
            pl.BlockSpec((None, d, tn), lambda i, j: (l, 0, col0 + j)),
            pl.BlockSpec((depth, tn), lambda i, j: (0, jnp.clip(j - f0, 0, nf - 1))),
        ],
        out_specs=[
            pl.BlockSpec((tm, tn), lambda i, j: (i, s16(j))),
            pl.BlockSpec((tm, tn), lambda i, j: (i, sf(j))),
        ],
        out_shape=[jax.ShapeDtypeStruct((m, n16 * tn), BF), jax.ShapeDtypeStruct((m, nf * tn), F32)],
        scratch_shapes=[pltpu.VMEM((tm, d), BF)],
        compiler_params=_params(("parallel", "arbitrary")),
        name="win",
    )(x, mod_l, mod_l, gpre, w_in, lb_logits2)


def _cmlp_kernel(u_ref, v_ref, g_ref, ws_ref, bs_ref, o_ref, *, n_chunks):
    v = v_ref[...].astype(F32)
    vc = v - jnp.mean(v, axis=-1, keepdims=True)
    vn = (vc * lax.rsqrt(jnp.mean(vc * vc, axis=-1, keepdims=True) + EPS) * g_ref[...]).astype(BF)
    dg = vn.shape[1] // GROUPS_A
    for g in range(GROUPS_A):
        w = ws_ref[g].astype(BF)
        bias = bs_ref[g]
        for c in range(n_chunks):
            rows = slice(c * CHUNK_A, (c + 1) * CHUNK_A)
            cols = slice(g * dg, (g + 1) * dg)
            sv = _dot(w, vn[rows, cols]) + bias
            o_ref[rows, cols] = (u_ref[rows, cols].astype(F32) * sv).astype(BF)


def _cmlp_call(p16, chunk_g, w_s, b_s, d_a, *, tm):
    m = p16.shape[0]
    n_chunks = tm // CHUNK_A
    return pl.pallas_call(
        functools.partial(_cmlp_kernel, n_chunks=n_chunks),
        grid=(m // tm,),
        in_specs=[
            pl.BlockSpec((tm, d_a), lambda i: (i, 0)),
            pl.BlockSpec((tm, d_a), lambda i: (i, 1)),
            pl.BlockSpec((1, d_a), lambda i: (0, 0)),
            pl.BlockSpec((GROUPS_A, CHUNK_A, CHUNK_A), lambda i: (0, 0, 0)),
            pl.BlockSpec((GROUPS_A, CHUNK_A, 1), lambda i: (0, 0, 0)),
        ],
        out_specs=pl.BlockSpec((tm, d_a), lambda i: (i, 0)),
        out_shape=jax.ShapeDtypeStruct((m, d_a), BF),
        compiler_params=_params(("parallel",)),
        name="cmlp",
    )(p16, p16, chunk_g, w_s, b_s[..., None])


def _hgrn_consts(fwd):
    c = HGRN_CHUNK
    t = np.arange(c)[:, None]
    u = np.arange(c)[None, :]

    def level_sum(b):
        half = b // 2
        mid = (t // b) * b + half
        upper = (t % b) >= half
        if fwd:
            return np.where(upper, (u >= mid) & (u <= t), (u > t) & (u < mid))
        return np.where(upper, (u >= mid) & (u < t), (u >= t) & (u < mid))

    def level_mask(b):
        half = b // 2
        upper = (t % b) >= half
        same = (t // b) == (u // b)
        if fwd:
            return same & upper & ((u % b) < half)
        return same & ~upper & ((u % b) >= half)

    lsum = np.concatenate([(u <= t) if fwd else (u >= t), level_sum(4), level_sum(8)], 0)
    masks = [t == u] + [level_mask(b) for b in HGRN_LEVELS] + [np.zeros((c, c), bool)]
    return (np.concatenate([lsum, lsum], 1).astype(np.float32),
            np.concatenate(masks, 1).astype(np.float32))


def _to_midpoint(cum, b, fwd):
    c, dk = cum.shape
    half = b // 2
    pieces = []
    for m in range(c // b):
        lo, mid = m * b, m * b + half
        ref = jnp.broadcast_to(cum[mid - 1:mid, :] if fwd else cum[mid:mid + 1, :], (half, dk))
        lower, upper = cum[lo:mid], cum[mid:lo + b]
        pieces += [ref - lower, upper - ref] if fwd else [lower - ref, ref - upper]
    return jnp.concatenate(pieces, axis=0)


def _hgrn_stage1(g, l2_ref):
    g_hi = g.astype(BF)
    g_lo = (g - g_hi.astype(F32)).astype(BF)
    return _dot(l2_ref[...], jnp.concatenate([g_hi, g_lo], axis=0))


def _hgrn_stage2(q, g, sums, mask_ref, fwd, with_out):
    c = HGRN_CHUNK
    dk = g.shape[1]
    f = jnp.exp2(g)
    k = 1.0 - f
    cum = sums[0:c]
    e_cum = jnp.exp2(cum)
    total = cum[c - 1:c] if fwd else cum[0:1]
    kend = (k * jnp.exp2(total - cum)).astype(BF)
    tot = e_cum[c - 1:c] if fwd else e_cum[0:1]
    if not with_out:
        return None, None, kend, tot

    row = lax.broadcasted_iota(jnp.int32, (c, dk), 0)
    e2 = jnp.where((row % 2) == (1 if fwd else 0), f, 1.0)
    es = [e2, jnp.exp2(sums[c:2 * c]), jnp.exp2(sums[2 * c:3 * c])]
    es += [jnp.exp2(_to_midpoint(cum, b, fwd)) for b in HGRN_LEVELS[3:]]
    qf = q.astype(F32)
    units = [(q, k.astype(BF))] + [((qf * e).astype(BF), (k * e).astype(BF)) for e in es]
    z = jnp.zeros((c, dk), BF)
    res = []
    for a in range(0, len(units) - 1, 2):
        (qa, ka), (qb, kb) = units[a], units[a + 1]
        kbd = jnp.concatenate([jnp.concatenate([ka, z], axis=1), jnp.concatenate([z, kb], axis=1)], axis=0)
        res.append(_dot_nt(jnp.concatenate([qa, qb], axis=1), kbd))
    q6, k6 = units[-1]
    res.append(_dot_nt(q6, jnp.concatenate([k6, z], axis=0)))
    p = sum(r * mask_ref[:, i * 2 * c:(i + 1) * 2 * c] for i, r in enumerate(res)).astype(BF)
    return p, (qf * e_cum).astype(BF), kend, tot


def _hgrn_stage3(p, qe, kend, tot, v, st, with_out):
    st_new = st * tot + _dot_tn(v, kend)
    if not with_out:
        return None, st_new
    o = _dot(p, jnp.concatenate([v, v], axis=0)) + _dot_nt(qe, st.astype(BF))
    return o, st_new


def _hgrn_kernel(*refs, n, with_out, with_init, with_final):
    it = iter(refs)
    q_ref = next(it) if with_out else None
    v_ref = next(it)
    gs_ref = next(it) if with_out else None
    gf_ref, gb_ref = next(it), next(it)
    gain_ref = next(it) if with_out else None
    sf0_ref, sb0_ref = (next(it), next(it)) if with_init else (None, None)
    l2_refs = (next(it), next(it))
    mask_refs = (next(it), next(it))
    y_ref = next(it) if with_out else None
    sf_ref, sb_ref = (next(it), next(it)) if with_final else (None, None)
    sums_scr, kend_scr, tot_scr, st_scr = next(it), next(it), next(it), next(it)
    p_scr, qe_scr, o_scr = (next(it), next(it), next(it)) if with_out else (None, None, None)

    c = HGRN_CHUNK
    nc = n // c
    dk = v_ref.shape[1]
    g_refs = (gf_ref, gb_ref)

    st_scr[0] = sf0_ref[...] if with_init else jnp.zeros((dk, dk), F32)
    st_scr[1] = sb0_ref[...] if with_init else jnp.zeros((dk, dk), F32)

    def rows_of(d, ci):
        start = ci * c if d == 0 else (nc - 1 - ci) * c
        return pl.ds(start if isinstance(start, int) else pl.multiple_of(start, c), c)

    def step(i, par, do1, do2, do3):
        for d in (0, 1):
            if do3:
                r = rows_of(d, i - 2)
                o, st = _hgrn_stage3(p_scr[d, par] if with_out else None,
                                     qe_scr[d, par] if with_out else None,
                                     kend_scr[d, par], tot_scr[d, par, 0:1, :],
                                     v_ref[r, :], st_scr[d], with_out)
                st_scr[d] = st
                if with_out:
                    o_scr[d, r, :] = o
            if do2:
                r = rows_of(d, i - 1)
                p, qe, kend, tot = _hgrn_stage2(q_ref[r, :] if with_out else None, g_refs[d][r, :],
                                                sums_scr[d, 1 - par], mask_refs[d], d == 0, with_out)
                kend_scr[d, 1 - par] = kend
                tot_scr[d, 1 - par, 0:1, :] = tot
                if with_out:
                    p_scr[d, 1 - par] = p
                    qe_scr[d, 1 - par] = qe
            if do1:
                sums_scr[d, par] = _hgrn_stage1(g_refs[d][rows_of(d, i), :], l2_refs[d])

    step(0, 0, True, False, False)
    step(1, 1, True, True, False)

    n_loop = (nc - 2) // HGRN_UNROLL

    def body(ii, _):
        for u in range(HGRN_UNROLL):
            step(2 + HGRN_UNROLL * ii + u, u % 2, True, True, True)
        return 0

    lax.fori_loop(0, n_loop, body, 0)
    for i in range(2 + HGRN_UNROLL * n_loop, nc):
        step(i, i % 2, True, True, True)
    step(nc, nc % 2, False, True, True)
    step(nc + 1, (nc + 1) % 2, False, False, True)

    if with_final:
        sf_ref[...] = st_scr[0]
        sb_ref[...] = st_scr[1]

    if with_out:
        rows = 256

        def readout(ri, _):
            r = pl.ds(pl.multiple_of(ri * rows, rows), rows)
            o = o_scr[0, r, :] + o_scr[1, r, :]
            o = o * lax.rsqrt(jnp.mean(o * o, axis=-1, keepdims=True) + EPS) * gain_ref[...]
            y_ref[r, :] = (o * gs_ref[r, :].astype(F32)).astype(BF)
            return 0

        lax.fori_loop(0, n // rows, readout, 0)


def _hgrn_call(p16, pf, gain, init, n, cols, *, with_out, with_final):
    m = p16.shape[0]
    bsz = m // n
    dk = HGRN_DK
    with_init = init is not None
    cq, cv, cg = cols
    tok = lambda col: pl.BlockSpec((n, dk), lambda b, h: (b, col + h))
    st_spec = pl.BlockSpec((None, None, dk, dk), lambda b, h: (b, h, 0, 0))
    full = lambda a: pl.BlockSpec(a.shape, lambda b, h: (0,) * a.ndim)

    assert (n // HGRN_CHUNK) % 2 == 0 and n % HGRN_CHUNK == 0
    lf, mf = _hgrn_consts(True)
    lbw, mb = _hgrn_consts(False)
    consts = [jnp.asarray(lf, BF), jnp.asarray(lbw, BF), jnp.asarray(mf, F32), jnp.asarray(mb, F32)]

    args, specs = [], []
    if with_out:
        args.append(p16); specs.append(tok(cq))
    args.append(p16); specs.append(tok(cv))
    if with_out:
        args.append(p16); specs.append(tok(cg))
    args += [pf, pf]; specs += [tok(0), tok(HEADS_B)]
    if with_out:
        args.append(gain); specs.append(pl.BlockSpec((1, dk), lambda b, h: (0, h)))
    if with_init:
        args += list(init); specs += [st_spec, st_spec]
    args += consts; specs += [full(a) for a in consts]

    out_shape, out_specs = [], []
    if with_out:
        out_shape.append(jax.ShapeDtypeStruct((m, HEADS_B * dk), BF)); out_specs.append(tok(0))
    if with_final:
        st_shape = jax.ShapeDtypeStruct((bsz, HEADS_B, dk, dk), F32)
        out_shape += [st_shape, st_shape]; out_specs += [st_spec, st_spec]
    c = HGRN_CHUNK
    scratch = [pltpu.VMEM((2, 2, lf.shape[0], dk), F32), pltpu.VMEM((2, 2, c, dk), BF),
               pltpu.VMEM((2, 2, 8, dk), F32), pltpu.VMEM((2, dk, dk), F32)]
    if with_out:
        scratch += [pltpu.VMEM((2, 2, c, 2 * c), BF), pltpu.VMEM((2, 2, c, dk), BF),
                    pltpu.VMEM((2, n, dk), F32)]

    kern = functools.partial(_hgrn_kernel, n=n, with_out=with_out, with_init=with_init, with_final=with_final)
    return pl.pallas_call(
        kern,
        grid=(bsz, HEADS_B),
        in_specs=specs,
        out_specs=out_specs,
        out_shape=out_shape,
        scratch_shapes=scratch,
        compiler_params=_params(("parallel", "parallel")),
        name="hgrn",
    )(*args)


def _merge_kernel(ya_ref, yb_ref, ga_ref, gb_ref, wa_ref, wb_ref, wo_ref, x_ref, gt_ref, gpost_ref,
                  o_ref, *, row_div, row_const, n_j):
    j = pl.program_id(1)
    row = _mod_row(row_div, row_const)
    ma = _dot(ya_ref[...], wa_ref[...].astype(BF))
    mb = _dot(yb_ref[...], wb_ref[...].astype(BF))
    mm = (ga_ref[...].astype(F32) * ma + gb_ref[...].astype(F32) * mb).astype(BF)

    @pl.when(j == 0)
    def _():
        o_ref[...] = jnp.zeros_like(o_ref)

    o_ref[...] += _dot(mm, wo_ref[...].astype(BF))

    @pl.when(j == n_j - 1)
    def _():
        _residual_into(o_ref, x_ref, gpost_ref, gt_ref, row, 1.0)


def _merge_call(x, ya, yb, p16, gate_cols, mod_l, mod_k, gpost, w_up_a, w_up_b, w_out, l,
                *, tm, row_div, row_const):
    m, d = x.shape
    da, db = ya.shape[1], yb.shape[1]
    tn = MERGE_TN
    n_j = d // tn
    ca, cb = (c // tn for c in gate_cols)
    kern = functools.partial(_merge_kernel, row_div=row_div, row_const=row_const, n_j=n_j)
    return pl.pallas_call(
        kern,
        grid=(m // tm, n_j),
        in_specs=[
            pl.BlockSpec((tm, da), lambda i, j: (i, 0)),
            pl.BlockSpec((tm, db), lambda i, j: (i, 0)),
            pl.BlockSpec((tm, tn), lambda i, j: (i, ca + j)),
            pl.BlockSpec((tm, tn), lambda i, j: (i, cb + j)),
            pl.BlockSpec((None, da, tn), lambda i, j: (l, 0, j)),
            pl.BlockSpec((None, db, tn), lambda i, j: (l, 0, j)),
            pl.BlockSpec((None, tn, d), lambda i, j: (l, j, 0)),
            pl.BlockSpec((tm, d), lambda i, j: (i, 0), pipeline_mode=pl.Buffered(1)),
            pl.BlockSpec((8, d), lambda i, j: (0, mod_k)),
            pl.BlockSpec((1, d), lambda i, j: (0, 0)),
        ],
        out_specs=pl.BlockSpec((tm, d), lambda i, j: (i, 0)),
        out_shape=jax.ShapeDtypeStruct((m, d), F32),
        compiler_params=_params(("parallel", "arbitrary")),
        name="merge",
    )(ya, yb, p16, p16, w_up_a, w_up_b, w_out, x, mod_l, gpost)


_KINDS_FULL = (["gelu"] * 4 + ["id"] * 2 + ["logf"] * 4 + ["id"] * 2 + ["silu"] * 2 + ["sigmoid"] * 8)
_KINDS_STATE = ["logf"] * 4 + ["id"] * 2


def kernel(x, c, ctx, c_ctx, w_mod, b_mod, norm_g, ffn1_w_gu, ffn1_w_down, ffn2_w_gu, ffn2_w_down,
           w_in, chunk_norm_g, w_spatial, b_spatial, lb_logits, hgrn_norm_g, w_up_a, w_up_b, w_out):
    bsz, n, d = x.shape
    n_ctx = ctx.shape[1]
    depth = w_mod.shape[0]
    d_a = chunk_norm_g.shape[1]
    d_b = hgrn_norm_g.shape[1]
    ctx_row = bsz

    cs = jnp.concatenate([c, c_ctx[None, :], jnp.zeros((8 - bsz - 1, d), F32)], axis=0)
    mod = _mod_call(cs, w_mod, b_mod)
    lb2 = lb_logits.reshape(depth, -1)

    xl = x.reshape(bsz * n, d)
    xc = ctx.reshape(bsz * n_ctx, d)
    tm = 1024
    lat = dict(tm=tm, row_div=n // tm, row_const=None)
    cx = dict(tm=bsz * n_ctx, row_div=None, row_const=ctx_row)
    col_q, col_i, col_g = 2 * d_a // 128, (2 * d_a + d_b) // 128, (2 * d_a + 2 * d_b) // 128
    gate_cols = (2 * d_a + 3 * d_b, 2 * d_a + 3 * d_b + d)

    for l in range(depth):
        last = l == depth - 1
        ml = mod[l]
        g = [norm_g[l, k][None, :] for k in range(norm_g.shape[1])]
        ffn1 = functools.partial(_ffn_call, mod_l=ml, mod_k=0, gpre=g[0], gpost=g[1],
                                 w_gu=ffn1_w_gu, w_down=ffn1_w_down, l=l, tf=256)
        ffn2 = functools.partial(_ffn_call, mod_l=ml, mod_k=6, gpre=g[4], gpost=g[5],
                                 w_gu=ffn2_w_gu, w_down=ffn2_w_down, l=l, tf=256)
        xl = ffn1(xl, **lat)
        xc = ffn1(xc, **cx)

        if not last:
            pc16, pcf = _win_call(xc, ml, 3, g[2], w_in, lb2, l, col0=0, kinds=_KINDS_FULL, **cx)
            ybc, s_f, s_b = _hgrn_call(pc16, pcf, hgrn_norm_g[l][None, :], None, n_ctx,
                                       (col_q, col_i, col_g), with_out=True, with_final=True)
        else:
            pc16, pcf = _win_call(xc, ml, 3, g[2], w_in, lb2, l, col0=(2 * d_a + d_b) // WIN_TN,
                                  kinds=_KINDS_STATE, **cx)
            s_f, s_b = _hgrn_call(pc16, pcf, None, None, n_ctx, (0, 0, 0), with_out=False, with_final=True)

        p16, pf = _win_call(xl, ml, 3, g[2], w_in, lb2, l, col0=0, kinds=_KINDS_FULL, **lat)
        (yb,) = _hgrn_call(p16, pf, hgrn_norm_g[l][None, :], (s_f, s_b), n,
                           (col_q, col_i, col_g), with_out=True, with_final=False)
        ya = _cmlp_call(p16, chunk_norm_g[l][None, :], w_spatial[l], b_spatial[l], d_a, tm=512)
        xl = _merge_call(xl, ya, yb, p16, gate_cols, ml, 5, g[3], w_up_a, w_up_b, w_out, l, **lat)
        xl = ffn2(xl, **lat)

        if not last:
            yac = _cmlp_call(pc16, chunk_norm_g[l][None, :], w_spatial[l], b_spatial[l], d_a, tm=512)
            xc = _merge_call(xc, yac, ybc, pc16, gate_cols, ml, 5, g[3], w_up_a, w_up_b, w_out, l, **cx)
            xc = ffn2(xc, **cx)
    return xl.reshape(bsz, n, d)
```

```python
import functools

import numpy as np
import jax
import jax.numpy as jnp
from jax import lax
from jax.experimental import pallas as pl
from jax.experimental.pallas import tpu as pltpu

BF = jnp.bfloat16
F32 = jnp.float32

EPS = 1e-6
MACARON = 0.5
F_FLOOR = 1e-30
N_MOD = 9
GROUPS_A = 8
CHUNK_A = 128
HEADS_B = 8
LANES = 128
HGRN_DK = 128
HGRN_CHUNK = 64
HGRN_UNROLL = 10
HGRN_LEVELS = (2, 4, 8, 16, 32, 64)

VMEM_LIMIT = 56 * 1024 * 1024


def _dot(a, b):
    return jnp.dot(a, b, preferred_element_type=F32)


def _dot_nt(a, b):
    return lax.dot_general(a, b, (((1,), (1,)), ((), ())), preferred_element_type=F32)


def _dot_tn(a, b):
    return lax.dot_general(a, b, (((0,), (0,)), ((), ())), preferred_element_type=F32)


def _rms(x, g):
    return x * lax.rsqrt(jnp.mean(x * x, axis=-1, keepdims=True) + EPS) * g


def _params(sem):
    return pltpu.CompilerParams(dimension_semantics=sem, vmem_limit_bytes=VMEM_LIMIT)


def _mod_kernel(cs_ref, w_ref, b_ref, o_ref):
    cs = cs_ref[...]
    s = (cs * jax.nn.sigmoid(cs)).astype(BF)
    o_ref[...] = _dot(s, w_ref[...].astype(BF)) + b_ref[...]


def _mod_call(cs, w_mod, b_mod):
    depth, d, n = w_mod.shape
    tn = 1024
    return pl.pallas_call(
        _mod_kernel,
        grid=(depth, n // tn),
        in_specs=[
            pl.BlockSpec((8, d), lambda l, j: (0, 0)),
            pl.BlockSpec((None, d, tn), lambda l, j: (l, 0, j)),
            pl.BlockSpec((None, 1, tn), lambda l, j: (l, 0, j)),
        ],
        out_specs=pl.BlockSpec((None, 8, tn), lambda l, j: (l, 0, j)),
        out_shape=jax.ShapeDtypeStruct((depth, 8, n), F32),
        compiler_params=_params(("parallel", "parallel")),
        name="mod",
    )(cs, w_mod, b_mod.reshape(depth, 1, n))


def _mod_row(row_div, row_const):
    if row_div is None:
        return row_const
    return pl.program_id(0) // row_div


ROW_CHUNK = 64


def _row_loop(n_rows, fn, unroll=2):
    def body(r, _):
        fn(pl.ds(pl.multiple_of(r * ROW_CHUNK, ROW_CHUNK), ROW_CHUNK))
        return 0

    lax.fori_loop(0, n_rows // ROW_CHUNK, body, 0, unroll=unroll)


def _row_rsqrt_ms(rs_scr, src_ref):
    def rows(sl):
        v = src_ref[sl, :]
        rs = lax.rsqrt(jnp.mean(v * v, axis=-1, keepdims=True) + EPS)
        rs_scr[sl, :] = jnp.broadcast_to(rs, (ROW_CHUNK, rs_scr.shape[1]))

    _row_loop(src_ref.shape[0], rows, unroll=4)


def _lane_tile(v, width):
    return jnp.concatenate([v] * (width // v.shape[1]), axis=1)


def _modulate_into(h_scr, x_ref, gpre_ref, sc_ref, sh_ref, row):
    gain = gpre_ref[...] * (1.0 + sc_ref[pl.ds(row, 1), :])
    shift = sh_ref[pl.ds(row, 1), :]

    def rows(sl):
        x = x_ref[sl, :]
        rs = lax.rsqrt(jnp.mean(x * x, axis=-1, keepdims=True) + EPS)
        h_scr[sl, :] = (x * rs * gain + shift).astype(BF)

    _row_loop(x_ref.shape[0], rows)


def _residual_into(o_ref, rs_scr, x_ref, gpost_ref, gt_ref, row, weight):
    gain = gpost_ref[...] * gt_ref[pl.ds(row, 1), :] * weight
    d = o_ref.shape[1]
    _row_rsqrt_ms(rs_scr, o_ref)

    def rows(sl):
        o_ref[sl, :] = x_ref[sl, :] + o_ref[sl, :] * _lane_tile(rs_scr[sl, :], d) * gain

    _row_loop(o_ref.shape[0], rows)


def _ffn_kernel(x_ref, sh_ref, sc_ref, gt_ref, gpre_ref, gpost_ref, wg_ref, wu_ref, wd_ref,
                o_ref, h_scr, rs_scr, *, row_div, row_const, n_f):
    j = pl.program_id(1)
    row = _mod_row(row_div, row_const)

    @pl.when(j == 0)
    def _():
        _modulate_into(h_scr, x_ref, gpre_ref, sc_ref, sh_ref, row)
        o_ref[...] = jnp.zeros_like(o_ref)

    h = h_scr[...]
    a = _dot(h, wg_ref[...].astype(BF))
    b = _dot(h, wu_ref[...].astype(BF))
    act = (a * jax.nn.sigmoid(a) * b).astype(BF)
    o_ref[...] += _dot(act, wd_ref[...].astype(BF))

    @pl.when(j == n_f - 1)
    def _():
        _residual_into(o_ref, rs_scr, x_ref, gpost_ref, gt_ref, row, MACARON)


def _ffn_call(x, mod_l, mod_k, gpre, gpost, w_gu, w_down, l, *, tm, tf, row_div, row_const):
    m, d = x.shape
    d_ff = w_down.shape[1]
    n_f = d_ff // tf
    kern = functools.partial(_ffn_kernel, row_div=row_div, row_const=row_const, n_f=n_f)
    mod_spec = lambda k: pl.BlockSpec((8, d), lambda i, j: (0, k))
    return pl.pallas_call(
        kern,
        grid=(m // tm, n_f),
        in_specs=[
            pl.BlockSpec((tm, d), lambda i, j: (i, 0), pipeline_mode=pl.Buffered(1)),
            mod_spec(mod_k), mod_spec(mod_k + 1), mod_spec(mod_k + 2),
            pl.BlockSpec((1, d), lambda i, j: (0, 0)),
            pl.BlockSpec((1, d), lambda i, j: (0, 0)),
            pl.BlockSpec((None, d, tf), lambda i, j: (l, 0, j)),
            pl.BlockSpec((None, d, tf), lambda i, j: (l, 0, n_f + j)),
            pl.BlockSpec((None, tf, d), lambda i, j: (l, j, 0)),
        ],
        out_specs=pl.BlockSpec((tm, d), lambda i, j: (i, 0)),
        out_shape=jax.ShapeDtypeStruct((m, d), F32),
        scratch_shapes=[pltpu.VMEM((tm, d), BF), pltpu.VMEM((tm, LANES), F32)],
        compiler_params=_params(("parallel", "arbitrary")),
        name="ffn",
    )(x, mod_l, mod_l, mod_l, gpre, gpost, w_gu, w_gu, w_down)


WIN_TN = 1024
WIN_SLABS = 4
MERGE_TN = 256


def _gelu_tanh(z):
    return z * jax.nn.sigmoid((2.0 * 0.7978845608028654) * (z + 0.044715 * (z * z * z)))


def _win_kernel(x_ref, sh_ref, sc_ref, gpre_ref, w_ref, lb_ref, p16_ref, pf_ref, h_scr,
                *, row_div, row_const, kinds, layer):
    j = pl.program_id(1)
    row = _mod_row(row_div, row_const)

    @pl.when(j == 0)
    def _():
        _modulate_into(h_scr, x_ref, gpre_ref, sc_ref, sh_ref, row)

    def in_kind(kind):
        cond = jnp.bool_(False)
        t = 0
        while t < len(kinds):
            t1 = t
            while t1 + 1 < len(kinds) and kinds[t1 + 1] == kinds[t]:
                t1 += 1
            if kinds[t] == kind:
                cond = cond | ((j >= t) & (j <= t1))
            t = t1 + 1
        return cond

    def slabs(out_ref, fn):
        w = w_ref[...].astype(BF)
        tm = h_scr.shape[0]
        for r0 in range(0, tm, tm // WIN_SLABS):
            sl = slice(r0, r0 + tm // WIN_SLABS)
            out_ref[sl, :] = fn(_dot(h_scr[sl, :], w)).astype(out_ref.dtype)

    def logf(z):
        ll = lb_ref[...]
        e = jnp.exp(ll - jnp.max(ll, axis=0, keepdims=True))
        p = e / jnp.sum(e, axis=0, keepdims=True)
        lb = jnp.zeros_like(p[0:1])
        for r in range(1, layer + 1):
            lb = lb + p[r:r + 1]
        return jnp.log2(jnp.maximum(lb + (1.0 - lb) * jax.nn.sigmoid(z), F_FLOOR))

    for kind, out_ref, fn in (("gelu", p16_ref, _gelu_tanh), ("id", p16_ref, lambda z: z),
                              ("silu", p16_ref, lambda z: z * jax.nn.sigmoid(z)),
                              ("sigmoid", p16_ref, jax.nn.sigmoid), ("logf", pf_ref, logf)):
        if kind in kinds:
            pl.when(in_kind(kind))(functools.partial(slabs, out_ref, fn))


def _win_call(x, mod_l, mod_k, gpre, w_in, lb_logits2, l, *, tm, row_div, row_const, col0, kinds):
    m, d = x.shape
    tn = WIN_TN
    nj = len(kinds)
    is_f = np.array([k == "logf" for k in kinds])
    n16, nf = int((~is_f).sum()), int(is_f.sum())
    c16 = np.maximum(np.cumsum(~is_f) - 1, 0)
    cf = np.maximum(np.cumsum(is_f) - 1, 0)
    f0 = int(np.argmax(is_f))

    def sel(table):
        def f(j):
            out = jnp.int32(int(table[0]))
            for t in range(1, nj):
                if table[t] != table[t - 1]:
                    out = jnp.where(j >= t, jnp.int32(int(table[t])), out)
            return out
        return f

    s16, sf = sel(c16), sel(cf)
    kern = functools.partial(_win_kernel, row_div=row_div, row_const=row_const, kinds=tuple(kinds), layer=l)
    mod_spec = lambda k: pl.BlockSpec((8, d), lambda i, j: (0, k))
    depth = lb_logits2.shape[0]
    return pl.pallas_call(
        kern,
        grid=(m // tm, nj),
        in_specs=[
            pl.BlockSpec((tm, d), lambda i, j: (i, 0), pipeline_mode=pl.Buffered(1)),
            mod_spec(mod_k), mod_spec(mod_k + 1),
            pl.BlockSpec((1, d), lambda i, j: (0, 0)),
            pl.BlockSpec((None, d, tn), lambda i, j: (l, 0, col0 + j)),
            pl.BlockSpec((depth, tn), lambda i, j: (0, jnp.clip(j - f0, 0, nf - 1))),
        ],
        out_specs=[
            pl.BlockSpec((tm, tn), lambda i, j: (i, s16(j))),
            pl.BlockSpec((tm, tn), lambda i, j: (i, sf(j))),
        ],
        out_shape=[jax.ShapeDtypeStruct((m, n16 * tn), BF), jax.ShapeDtypeStruct((m, nf * tn), F32)],
        scratch_shapes=[pltpu.VMEM((tm, d), BF)],
        compiler_params=_params(("parallel", "arbitrary")),
        name="win",
    )(x, mod_l, mod_l, gpre, w_in, lb_logits2)


def _cmlp_kernel(u_ref, v_ref, g_ref, ws_ref, bs_ref, o_ref, *, n_chunks):
    v = v_ref[...].astype(F32)
    vc = v - jnp.mean(v, axis=-1, keepdims=True)
    vn = (vc * lax.rsqrt(jnp.mean(vc * vc, axis=-1, keepdims=True) + EPS) * g_ref[...]).astype(BF)
    dg = vn.shape[1] // GROUPS_A
    for g in range(GROUPS_A):
        w = ws_ref[g].astype(BF)
        bias = bs_ref[g]
        for c in range(n_chunks):
            rows = slice(c * CHUNK_A, (c + 1) * CHUNK_A)
            cols = slice(g * dg, (g + 1) * dg)
            sv = _dot(w, vn[rows, cols]) + bias
            o_ref[rows, cols] = (u_ref[rows, cols].astype(F32) * sv).astype(BF)


def _cmlp_call(p16, chunk_g, w_s, b_s, d_a, *, tm):
    m = p16.shape[0]
    n_chunks = tm // CHUNK_A
    return pl.pallas_call(
        functools.partial(_cmlp_kernel, n_chunks=n_chunks),
        grid=(m // tm,),
        in_specs=[
            pl.BlockSpec((tm, d_a), lambda i: (i, 0)),
            pl.BlockSpec((tm, d_a), lambda i: (i, 1)),
            pl.BlockSpec((1, d_a), lambda i: (0, 0)),
            pl.BlockSpec((GROUPS_A, CHUNK_A, CHUNK_A), lambda i: (0, 0, 0)),
            pl.BlockSpec((GROUPS_A, CHUNK_A, 1), lambda i: (0, 0, 0)),
        ],
        out_specs=pl.BlockSpec((tm, d_a), lambda i: (i, 0)),
        out_shape=jax.ShapeDtypeStruct((m, d_a), BF),
        compiler_params=_params(("parallel",)),
        name="cmlp",
    )(p16, p16, chunk_g, w_s, b_s[..., None])


def _hgrn_consts(fwd):
    c = HGRN_CHUNK
    t = np.arange(c)[:, None]
    u = np.arange(c)[None, :]

    def level_sum(b):
        half = b // 2
        mid = (t // b) * b + half
        upper = (t % b) >= half
        if fwd:
            return np.where(upper, (u >= mid) & (u <= t), (u > t) & (u < mid))
        return np.where(upper, (u >= mid) & (u < t), (u >= t) & (u < mid))

    def level_mask(b):
        half = b // 2
        upper = (t % b) >= half
        same = (t // b) == (u // b)
        if fwd:
            return same & upper & ((u % b) < half)
        return same & ~upper & ((u % b) >= half)

    lsum = np.concatenate([(u <= t) if fwd else (u >= t), level_sum(4), level_sum(8)], 0)
    masks = [t == u] + [level_mask(b) for b in HGRN_LEVELS] + [np.zeros((c, c), bool)]
    return (np.concatenate([lsum, lsum], 1).astype(np.float32),
            np.concatenate(masks, 1).astype(np.float32))


def _to_midpoint(cum, b, fwd):
    c, dk = cum.shape
    half = b // 2
    pieces = []
    for m in range(c // b):
        lo, mid = m * b, m * b + half
        ref = jnp.broadcast_to(cum[mid - 1:mid, :] if fwd else cum[mid:mid + 1, :], (half, dk))
        lower, upper = cum[lo:mid], cum[mid:lo + b]
        pieces += [ref - lower, upper - ref] if fwd else [lower - ref, ref - upper]
    return jnp.concatenate(pieces, axis=0)


def _hgrn_stage1(g, l2_ref):
    g_hi = g.astype(BF)
    g_lo = (g - g_hi.astype(F32)).astype(BF)
    return _dot(l2_ref[...], jnp.concatenate([g_hi, g_lo], axis=0))


def _hgrn_stage2(q, g, sums, mask_ref, fwd, with_out):
    c = HGRN_CHUNK
    dk = g.shape[1]
    f = jnp.exp2(g)
    k = 1.0 - f
    cum = sums[0:c]
    e_cum = jnp.exp2(cum)
    total = cum[c - 1:c] if fwd else cum[0:1]
    kend = (k * jnp.exp2(total - cum)).astype(BF)
    tot = e_cum[c - 1:c] if fwd else e_cum[0:1]
    if not with_out:
        return None, None, kend, tot

    row = lax.broadcasted_iota(jnp.int32, (c, dk), 0)
    e2 = jnp.where((row % 2) == (1 if fwd else 0), f, 1.0)
    es = [e2, jnp.exp2(sums[c:2 * c]), jnp.exp2(sums[2 * c:3 * c])]
    es += [jnp.exp2(_to_midpoint(cum, b, fwd)) for b in HGRN_LEVELS[3:]]
    qf = q.astype(F32)
    units = [(q, k.astype(BF))] + [((qf * e).astype(BF), (k * e).astype(BF)) for e in es]
    z = jnp.zeros((c, dk), BF)
    res = []
    for a in range(0, len(units) - 1, 2):
        (qa, ka), (qb, kb) = units[a], units[a + 1]
        kbd = jnp.concatenate([jnp.concatenate([ka, z], axis=1), jnp.concatenate([z, kb], axis=1)], axis=0)
        res.append(_dot_nt(jnp.concatenate([qa, qb], axis=1), kbd))
    q6, k6 = units[-1]
    res.append(_dot_nt(q6, jnp.concatenate([k6, z], axis=0)))
    p = sum(r * mask_ref[:, i * 2 * c:(i + 1) * 2 * c] for i, r in enumerate(res)).astype(BF)
    return p, (qf * e_cum).astype(BF), kend, tot


def _hgrn_stage3(p, qe, kend, tot, v, st, with_out):
    st_new = st * tot + _dot_tn(v, kend)
    if not with_out:
        return None, st_new
    o = _dot(p, jnp.concatenate([v, v], axis=0)) + _dot_nt(qe, st.astype(BF))
    return o, st_new


def _hgrn_kernel(*refs, n, with_out, with_init, with_final):
    it = iter(refs)
    q_ref = next(it) if with_out else None
    v_ref = next(it)
    gs_ref = next(it) if with_out else None
    gf_ref, gb_ref = next(it), next(it)
    gain_ref = next(it) if with_out else None
    sf0_ref, sb0_ref = (next(it), next(it)) if with_init else (None, None)
    l2_refs = (next(it), next(it))
    mask_refs = (next(it), next(it))
    y_ref = next(it) if with_out else None
    sf_ref, sb_ref = (next(it), next(it)) if with_final else (None, None)
    sums_scr, kend_scr, tot_scr, st_scr = next(it), next(it), next(it), next(it)
    p_scr, qe_scr, o_scr = (next(it), next(it), next(it)) if with_out else (None, None, None)

    c = HGRN_CHUNK
    nc = n // c
    dk = v_ref.shape[1]
    g_refs = (gf_ref, gb_ref)

    st_scr[0] = sf0_ref[...] if with_init else jnp.zeros((dk, dk), F32)
    st_scr[1] = sb0_ref[...] if with_init else jnp.zeros((dk, dk), F32)

    def rows_of(d, ci):
        start = ci * c if d == 0 else (nc - 1 - ci) * c
        return pl.ds(start if isinstance(start, int) else pl.multiple_of(start, c), c)

    def step(i, par, do1, do2, do3):
        for d in (0, 1):
            if do3:
                r = rows_of(d, i - 2)
                o, st = _hgrn_stage3(p_scr[d, par] if with_out else None,
                                     qe_scr[d, par] if with_out else None,
                                     kend_scr[d, par], tot_scr[d, par, 0:1, :],
                                     v_ref[r, :], st_scr[d], with_out)
                st_scr[d] = st
                if with_out:
                    o_scr[d, r, :] = o
            if do2:
                r = rows_of(d, i - 1)
                p, qe, kend, tot = _hgrn_stage2(q_ref[r, :] if with_out else None, g_refs[d][r, :],
                                                sums_scr[d, 1 - par], mask_refs[d], d == 0, with_out)
                kend_scr[d, 1 - par] = kend
                tot_scr[d, 1 - par, 0:1, :] = tot
                if with_out:
                    p_scr[d, 1 - par] = p
                    qe_scr[d, 1 - par] = qe
            if do1:
                sums_scr[d, par] = _hgrn_stage1(g_refs[d][rows_of(d, i), :], l2_refs[d])

    step(0, 0, True, False, False)
    step(1, 1, True, True, False)

    n_loop = (nc - 2) // HGRN_UNROLL

    def body(ii, _):
        for u in range(HGRN_UNROLL):
            step(2 + HGRN_UNROLL * ii + u, u % 2, True, True, True)
        return 0

    lax.fori_loop(0, n_loop, body, 0)
    for i in range(2 + HGRN_UNROLL * n_loop, nc):
        step(i, i % 2, True, True, True)
    step(nc, nc % 2, False, True, True)
    step(nc + 1, (nc + 1) % 2, False, False, True)

    if with_final:
        sf_ref[...] = st_scr[0]
        sb_ref[...] = st_scr[1]

    if with_out:
        rows = 256

        def readout(ri, _):
            r = pl.ds(pl.multiple_of(ri * rows, rows), rows)
            o = o_scr[0, r, :] + o_scr[1, r, :]
            o = o * lax.rsqrt(jnp.mean(o * o, axis=-1, keepdims=True) + EPS) * gain_ref[...]
            y_ref[r, :] = (o * gs_ref[r, :].astype(F32)).astype(BF)
            return 0

        lax.fori_loop(0, n // rows, readout, 0, unroll=2)


def _hgrn_call(p16, pf, gain, init, n, cols, *, with_out, with_final):
    m = p16.shape[0]
    bsz = m // n
    dk = HGRN_DK
    with_init = init is not None
    cq, cv, cg = cols
    tok = lambda col: pl.BlockSpec((n, dk), lambda b, h: (b, col + h))
    st_spec = pl.BlockSpec((None, None, dk, dk), lambda b, h: (b, h, 0, 0))
    full = lambda a: pl.BlockSpec(a.shape, lambda b, h: (0,) * a.ndim)

    assert (n // HGRN_CHUNK) % 2 == 0 and n % HGRN_CHUNK == 0
    lf, mf = _hgrn_consts(True)
    lbw, mb = _hgrn_consts(False)
    consts = [jnp.asarray(lf, BF), jnp.asarray(lbw, BF), jnp.asarray(mf, F32), jnp.asarray(mb, F32)]

    args, specs = [], []
    if with_out:
        args.append(p16); specs.append(tok(cq))
    args.append(p16); specs.append(tok(cv))
    if with_out:
        args.append(p16); specs.append(tok(cg))
    args += [pf, pf]; specs += [tok(0), tok(HEADS_B)]
    if with_out:
        args.append(gain); specs.append(pl.BlockSpec((1, dk), lambda b, h: (0, h)))
    if with_init:
        args += list(init); specs += [st_spec, st_spec]
    args += consts; specs += [full(a) for a in consts]

    out_shape, out_specs = [], []
    if with_out:
        out_shape.append(jax.ShapeDtypeStruct((m, HEADS_B * dk), BF)); out_specs.append(tok(0))
    if with_final:
        st_shape = jax.ShapeDtypeStruct((bsz, HEADS_B, dk, dk), F32)
        out_shape += [st_shape, st_shape]; out_specs += [st_spec, st_spec]
    c = HGRN_CHUNK
    scratch = [pltpu.VMEM((2, 2, lf.shape[0], dk), F32), pltpu.VMEM((2, 2, c, dk), BF),
               pltpu.VMEM((2, 2, 8, dk), F32), pltpu.VMEM((2, dk, dk), F32)]
    if with_out:
        scratch += [pltpu.VMEM((2, 2, c, 2 * c), BF), pltpu.VMEM((2, 2, c, dk), BF),
                    pltpu.VMEM((2, n, dk), F32)]

    kern = functools.partial(_hgrn_kernel, n=n, with_out=with_out, with_init=with_init, with_final=with_final)
    return pl.pallas_call(
        kern,
        grid=(bsz, HEADS_B),
        in_specs=specs,
        out_specs=out_specs,
        out_shape=out_shape,
        scratch_shapes=scratch,
        compiler_params=_params(("parallel", "parallel")),
        name="hgrn",
    )(*args)


def _merge_kernel(ya_ref, yb_ref, ga_ref, gb_ref, wa_ref, wb_ref, wo_ref, x_ref, gt_ref, gpost_ref,
                  o_ref, rs_scr, *, row_div, row_const, n_j):
    j = pl.program_id(1)
    row = _mod_row(row_div, row_const)
    ma = _dot(ya_ref[...], wa_ref[...].astype(BF))
    mb = _dot(yb_ref[...], wb_ref[...].astype(BF))
    mm = (ga_ref[...].astype(F32) * ma + gb_ref[...].astype(F32) * mb).astype(BF)

    @pl.when(j == 0)
    def _():
        o_ref[...] = jnp.zeros_like(o_ref)

    o_ref[...] += _dot(mm, wo_ref[...].astype(BF))

    @pl.when(j == n_j - 1)
    def _():
        _residual_into(o_ref, rs_scr, x_ref, gpost_ref, gt_ref, row, 1.0)


def _merge_call(x, ya, yb, p16, gate_cols, mod_l, mod_k, gpost, w_up_a, w_up_b, w_out, l,
                *, tm, row_div, row_const):
    m, d = x.shape
    da, db = ya.shape[1], yb.shape[1]
    tn = MERGE_TN
    n_j = d // tn
    ca, cb = (c // tn for c in gate_cols)
    kern = functools.partial(_merge_kernel, row_div=row_div, row_const=row_const, n_j=n_j)
    return pl.pallas_call(
        kern,
        grid=(m // tm, n_j),
        in_specs=[
            pl.BlockSpec((tm, da), lambda i, j: (i, 0)),
            pl.BlockSpec((tm, db), lambda i, j: (i, 0)),
            pl.BlockSpec((tm, tn), lambda i, j: (i, ca + j)),
            pl.BlockSpec((tm, tn), lambda i, j: (i, cb + j)),
            pl.BlockSpec((None, da, tn), lambda i, j: (l, 0, j)),
            pl.BlockSpec((None, db, tn), lambda i, j: (l, 0, j)),
            pl.BlockSpec((None, tn, d), lambda i, j: (l, j, 0)),
            pl.BlockSpec((tm, d), lambda i, j: (i, 0), pipeline_mode=pl.Buffered(1)),
            pl.BlockSpec((8, d), lambda i, j: (0, mod_k)),
            pl.BlockSpec((1, d), lambda i, j: (0, 0)),
        ],
        out_specs=pl.BlockSpec((tm, d), lambda i, j: (i, 0)),
        out_shape=jax.ShapeDtypeStruct((m, d), F32),
        scratch_shapes=[pltpu.VMEM((tm, LANES), F32)],
        compiler_params=_params(("parallel", "arbitrary")),
        name="merge",
    )(ya, yb, p16, p16, w_up_a, w_up_b, w_out, x, mod_l, gpost)


def _win_kinds(groups):
    return [kind for kind, width in groups for _ in range(width // WIN_TN)]


def kernel(x, c, ctx, c_ctx, w_mod, b_mod, norm_g, ffn1_w_gu, ffn1_w_down, ffn2_w_gu, ffn2_w_down,
           w_in, chunk_norm_g, w_spatial, b_spatial, lb_logits, hgrn_norm_g, w_up_a, w_up_b, w_out):
    bsz, n, d = x.shape
    n_ctx = ctx.shape[1]
    depth = w_mod.shape[0]
    d_a = chunk_norm_g.shape[1]
    d_b = hgrn_norm_g.shape[1]
    ctx_row = bsz

    cs = jnp.concatenate([c, c_ctx[None, :], jnp.zeros((8 - bsz - 1, d), F32)], axis=0)
    mod = _mod_call(cs, w_mod, b_mod)
    lb2 = lb_logits.reshape(depth, -1)

    xl = x.reshape(bsz * n, d)
    xc = ctx.reshape(bsz * n_ctx, d)
    tm = 1024
    lat = dict(tm=tm, row_div=n // tm, row_const=None)
    cx = dict(tm=bsz * n_ctx, row_div=None, row_const=ctx_row)
    col_q, col_i, col_g = 2 * d_a // 128, (2 * d_a + d_b) // 128, (2 * d_a + 2 * d_b) // 128
    gate_cols = (2 * d_a + 3 * d_b, 2 * d_a + 3 * d_b + d)
    kinds_full = _win_kinds([("gelu", 2 * d_a), ("id", d_b), ("logf", 2 * d_b), ("id", d_b),
                             ("silu", d_b), ("sigmoid", 2 * d)])
    kinds_state = _win_kinds([("logf", 2 * d_b), ("id", d_b)])

    for l in range(depth):
        last = l == depth - 1
        ml = mod[l]
        g = [norm_g[l, k][None, :] for k in range(norm_g.shape[1])]
        ffn1 = functools.partial(_ffn_call, mod_l=ml, mod_k=0, gpre=g[0], gpost=g[1],
                                 w_gu=ffn1_w_gu, w_down=ffn1_w_down, l=l, tf=256)
        ffn2 = functools.partial(_ffn_call, mod_l=ml, mod_k=6, gpre=g[4], gpost=g[5],
                                 w_gu=ffn2_w_gu, w_down=ffn2_w_down, l=l, tf=256)
        xl = ffn1(xl, **lat)
        xc = ffn1(xc, **cx)

        if not last:
            pc16, pcf = _win_call(xc, ml, 3, g[2], w_in, lb2, l, col0=0, kinds=kinds_full, **cx)
            ybc, s_f, s_b = _hgrn_call(pc16, pcf, hgrn_norm_g[l][None, :], None, n_ctx,
                                       (col_q, col_i, col_g), with_out=True, with_final=True)
        else:
            pc16, pcf = _win_call(xc, ml, 3, g[2], w_in, lb2, l, col0=(2 * d_a + d_b) // WIN_TN,
                                  kinds=kinds_state, **cx)
            s_f, s_b = _hgrn_call(pc16, pcf, None, None, n_ctx, (0, 0, 0), with_out=False, with_final=True)

        p16, pf = _win_call(xl, ml, 3, g[2], w_in, lb2, l, col0=0, kinds=kinds_full, **lat)
        (yb,) = _hgrn_call(p16, pf, hgrn_norm_g[l][None, :], (s_f, s_b), n,
                           (col_q, col_i, col_g), with_out=True, with_final=False)
        ya = _cmlp_call(p16, chunk_norm_g[l][None, :], w_spatial[l], b_spatial[l], d_a, tm=512)
        xl = _merge_call(xl, ya, yb, p16, gate_cols, ml, 5, g[3], w_up_a, w_up_b, w_out, l, **lat)
        xl = ffn2(xl, **lat)

        if not last:
            yac = _cmlp_call(pc16, chunk_norm_g[l][None, :], w_spatial[l], b_spatial[l], d_a, tm=512)
            xc = _merge_call(xc, yac, ybc, pc16, gate_cols, ml, 5, g[3], w_up_a, w_up_b, w_out, l, **cx)
            xc = ffn2(xc, **cx)
    return xl.reshape(bsz, n, d)
```

```python
import functools

import numpy as np
import jax
import jax.numpy as jnp
from jax import lax
from jax.experimental import pallas as pl
from jax.experimental.pallas import tpu as pltpu

BF = jnp.bfloat16
F32 = jnp.float32

EPS = 1e-6
MACARON = 0.5
F_FLOOR = 1e-30
N_MOD = 9
GROUPS_A = 8
CHUNK_A = 128
HEADS_B = 8
LANES = 128
HGRN_DK = 128
HGRN_CHUNK = 64
HGRN_UNROLL = 10
HGRN_LEVELS = (2, 4, 8, 16, 32, 64)

VMEM_LIMIT = 56 * 1024 * 1024


def _dot(a, b):
    return jnp.dot(a, b, preferred_element_type=F32)


def _dot_nt(a, b):
    return lax.dot_general(a, b, (((1,), (1,)), ((), ())), preferred_element_type=F32)


def _dot_tn(a, b):
    return lax.dot_general(a, b, (((0,), (0,)), ((), ())), preferred_element_type=F32)


def _rms(x, g):
    return x * lax.rsqrt(jnp.mean(x * x, axis=-1, keepdims=True) + EPS) * g


def _params(sem):
    return pltpu.CompilerParams(dimension_semantics=sem, vmem_limit_bytes=VMEM_LIMIT)


def _mod_kernel(cs_ref, w_ref, b_ref, o_ref):
    cs = cs_ref[...]
    s = (cs * jax.nn.sigmoid(cs)).astype(BF)
    o_ref[...] = _dot(s, w_ref[...].astype(BF)) + b_ref[...]


def _mod_call(cs, w_mod, b_mod):
    depth, d, n = w_mod.shape
    tn = 1024
    return pl.pallas_call(
        _mod_kernel,
        grid=(depth, n // tn),
        in_specs=[
            pl.BlockSpec((8, d), lambda l, j: (0, 0)),
            pl.BlockSpec((None, d, tn), lambda l, j: (l, 0, j)),
            pl.BlockSpec((None, 1, tn), lambda l, j: (l, 0, j)),
        ],
        out_specs=pl.BlockSpec((None, 8, tn), lambda l, j: (l, 0, j)),
        out_shape=jax.ShapeDtypeStruct((depth, 8, n), F32),
        compiler_params=_params(("parallel", "parallel")),
        name="mod",
    )(cs, w_mod, b_mod.reshape(depth, 1, n))


def _mod_row(row_div, row_const):
    if row_div is None:
        return row_const
    return pl.program_id(0) // row_div


ROW_CHUNK = 64
FFN_TF = 256
FFN_TF16 = 512


def _row_loop(n_rows, fn, unroll=2):
    def body(r, _):
        fn(pl.ds(pl.multiple_of(r * ROW_CHUNK, ROW_CHUNK), ROW_CHUNK))
        return 0

    lax.fori_loop(0, n_rows // ROW_CHUNK, body, 0, unroll=unroll)


def _row_rsqrt_ms(rs_scr, src_ref):
    def rows(sl):
        v = src_ref[sl, :]
        rs = lax.rsqrt(jnp.mean(v * v, axis=-1, keepdims=True) + EPS)
        rs_scr[sl, :] = jnp.broadcast_to(rs, (ROW_CHUNK, rs_scr.shape[1]))

    _row_loop(src_ref.shape[0], rows, unroll=4)


def _lane_tile(v, width):
    return jnp.concatenate([v] * (width // v.shape[1]), axis=1)


def _modulate_into(h_scr, x_ref, gpre_ref, sc_ref, sh_ref, row):
    gain = gpre_ref[...] * (1.0 + sc_ref[pl.ds(row, 1), :])
    shift = sh_ref[pl.ds(row, 1), :]

    def rows(sl):
        x = x_ref[sl, :]
        rs = lax.rsqrt(jnp.mean(x * x, axis=-1, keepdims=True) + EPS)
        h_scr[sl, :] = (x * rs * gain + shift).astype(BF)

    _row_loop(x_ref.shape[0], rows)


def _residual_into(o_ref, rs_scr, x_ref, gpost_ref, gt_ref, row, weight):
    gain = gpost_ref[...] * gt_ref[pl.ds(row, 1), :] * weight
    d = o_ref.shape[1]
    _row_rsqrt_ms(rs_scr, o_ref)

    def rows(sl):
        o_ref[sl, :] = x_ref[sl, :] + o_ref[sl, :] * _lane_tile(rs_scr[sl, :], d) * gain

    _row_loop(o_ref.shape[0], rows)


def _ffn_kernel(x_ref, sh_ref, sc_ref, gt_ref, gpre_ref, gpost_ref, wg_ref, wu_ref, wd_ref, *rest,
                row_div, row_const, n_f, emit16):
    if emit16:
        o_ref, wg16_ref, wu16_ref, wd16_ref, h_scr, rs_scr = rest
    else:
        o_ref, h_scr, rs_scr = rest
    j = pl.program_id(1)
    row = _mod_row(row_div, row_const)

    @pl.when(j == 0)
    def _():
        _modulate_into(h_scr, x_ref, gpre_ref, sc_ref, sh_ref, row)
        o_ref[...] = jnp.zeros_like(o_ref)

    wg, wu, wd = wg_ref[...].astype(BF), wu_ref[...].astype(BF), wd_ref[...].astype(BF)
    if emit16:
        wg16_ref[...] = wg
        wu16_ref[...] = wu
        wd16_ref[...] = wd
    h = h_scr[...]
    a = _dot(h, wg)
    b = _dot(h, wu)
    act = (a * jax.nn.sigmoid(a) * b).astype(BF)
    o_ref[...] += _dot(act, wd)

    @pl.when(j == n_f - 1)
    def _():
        _residual_into(o_ref, rs_scr, x_ref, gpost_ref, gt_ref, row, MACARON)


def _ffn_call(x, mod_l, mod_k, gpre, gpost, w_gu, w_down, l, *, tm, tf, row_div, row_const,
              w16=None, emit16=False):
    m, d = x.shape
    d_ff = w_down.shape[1]
    n_f = d_ff // tf
    kern = functools.partial(_ffn_kernel, row_div=row_div, row_const=row_const, n_f=n_f, emit16=emit16)
    mod_spec = lambda k: pl.BlockSpec((8, d), lambda i, j: (0, k))
    col_tile = pl.BlockSpec((d, tf), lambda i, j: (0, j))
    row_tile = pl.BlockSpec((tf, d), lambda i, j: (j, 0))
    if w16 is None:
        weights = (w_gu, w_gu, w_down)
        w_specs = [pl.BlockSpec((None, d, tf), lambda i, j: (l, 0, j)),
                   pl.BlockSpec((None, d, tf), lambda i, j: (l, 0, n_f + j)),
                   pl.BlockSpec((None, tf, d), lambda i, j: (l, j, 0))]
    else:
        weights = w16
        w_specs = [col_tile, col_tile, row_tile]
    out_specs = [pl.BlockSpec((tm, d), lambda i, j: (i, 0))]
    out_shape = [jax.ShapeDtypeStruct((m, d), F32)]
    if emit16:
        assert m == tm
        out_specs += [col_tile, col_tile, row_tile]
        out_shape += [jax.ShapeDtypeStruct((d, d_ff), BF), jax.ShapeDtypeStruct((d, d_ff), BF),
                      jax.ShapeDtypeStruct((d_ff, d), BF)]
    out = pl.pallas_call(
        kern,
        grid=(m // tm, n_f),
        in_specs=[
            pl.BlockSpec((tm, d), lambda i, j: (i, 0), pipeline_mode=pl.Buffered(1)),
            mod_spec(mod_k), mod_spec(mod_k + 1), mod_spec(mod_k + 2),
            pl.BlockSpec((1, d), lambda i, j: (0, 0)),
            pl.BlockSpec((1, d), lambda i, j: (0, 0)),
        ] + w_specs,
        out_specs=out_specs,
        out_shape=out_shape,
        scratch_shapes=[pltpu.VMEM((tm, d), BF), pltpu.VMEM((tm, LANES), F32)],
        compiler_params=_params(("parallel", "arbitrary")),
        name="ffn",
    )(x, mod_l, mod_l, mod_l, gpre, gpost, *weights)
    return (out[0], tuple(out[1:])) if emit16 else out[0]


WIN_TN = 1024
WIN_SLABS = 4
MERGE_TN = 256


def _gelu_tanh(z):
    return z * jax.nn.sigmoid((2.0 * 0.7978845608028654) * (z + 0.044715 * (z * z * z)))


def _win_kernel(x_ref, sh_ref, sc_ref, gpre_ref, w_ref, lb_ref, p16_ref, pf_ref, h_scr,
                *, row_div, row_const, kinds, layer):
    j = pl.program_id(1)
    row = _mod_row(row_div, row_const)

    @pl.when(j == 0)
    def _():
        _modulate_into(h_scr, x_ref, gpre_ref, sc_ref, sh_ref, row)

    def in_kind(kind):
        cond = jnp.bool_(False)
        t = 0
        while t < len(kinds):
            t1 = t
            while t1 + 1 < len(kinds) and kinds[t1 + 1] == kinds[t]:
                t1 += 1
            if kinds[t] == kind:
                cond = cond | ((j >= t) & (j <= t1))
            t = t1 + 1
        return cond

    def slabs(out_ref, fn):
        w = w_ref[...].astype(BF)
        tm = h_scr.shape[0]
        for r0 in range(0, tm, tm // WIN_SLABS):
            sl = slice(r0, r0 + tm // WIN_SLABS)
            out_ref[sl, :] = fn(_dot(h_scr[sl, :], w)).astype(out_ref.dtype)

    def logf(z):
        ll = lb_ref[...]
        e = jnp.exp(ll - jnp.max(ll, axis=0, keepdims=True))
        p = e / jnp.sum(e, axis=0, keepdims=True)
        lb = jnp.zeros_like(p[0:1])
        for r in range(1, layer + 1):
            lb = lb + p[r:r + 1]
        return jnp.log2(jnp.maximum(lb + (1.0 - lb) * jax.nn.sigmoid(z), F_FLOOR))

    for kind, out_ref, fn in (("gelu", p16_ref, _gelu_tanh), ("id", p16_ref, lambda z: z),
                              ("silu", p16_ref, lambda z: z * jax.nn.sigmoid(z)),
                              ("sigmoid", p16_ref, jax.nn.sigmoid), ("logf", pf_ref, logf)):
        if kind in kinds:
            pl.when(in_kind(kind))(functools.partial(slabs, out_ref, fn))


def _win_call(x, mod_l, mod_k, gpre, w_in, lb_logits2, l, *, tm, row_div, row_const, col0, kinds):
    m, d = x.shape
    tn = WIN_TN
    nj = len(kinds)
    is_f = np.array([k == "logf" for k in kinds])
    n16, nf = int((~is_f).sum()), int(is_f.sum())
    c16 = np.maximum(np.cumsum(~is_f) - 1, 0)
    cf = np.maximum(np.cumsum(is_f) - 1, 0)
    f0 = int(np.argmax(is_f))

    def sel(table):
        def f(j):
            out = jnp.int32(int(table[0]))
            for t in range(1, nj):
                if table[t] != table[t - 1]:
                    out = jnp.where(j >= t, jnp.int32(int(table[t])), out)
            return out
        return f

    s16, sf = sel(c16), sel(cf)
    kern = functools.partial(_win_kernel, row_div=row_div, row_const=row_const, kinds=tuple(kinds), layer=l)
    mod_spec = lambda k: pl.BlockSpec((8, d), lambda i, j: (0, k))
    depth = lb_logits2.shape[0]
    return pl.pallas_call(
        kern,
        grid=(m // tm, nj),
        in_specs=[
            pl.BlockSpec((tm, d), lambda i, j: (i, 0), pipeline_mode=pl.Buffered(1)),
            mod_spec(mod_k), mod_spec(mod_k + 1),
            pl.BlockSpec((1, d), lambda i, j: (0, 0)),
            pl.BlockSpec((None, d, tn), lambda i, j: (l, 0, col0 + j)),
            pl.BlockSpec((depth, tn), lambda i, j: (0, jnp.clip(j - f0, 0, nf - 1))),
        ],
        out_specs=[
            pl.BlockSpec((tm, tn), lambda i, j: (i, s16(j))),
            pl.BlockSpec((tm, tn), lambda i, j: (i, sf(j))),
        ],
        out_shape=[jax.ShapeDtypeStruct((m, n16 * tn), BF), jax.ShapeDtypeStruct((m, nf * tn), F32)],
        scratch_shapes=[pltpu.VMEM((tm, d), BF)],
        compiler_params=_params(("parallel", "arbitrary")),
        name="win",
    )(x, mod_l, mod_l, gpre, w_in, lb_logits2)


def _cmlp_kernel(u_ref, v_ref, g_ref, ws_ref, bs_ref, o_ref, *, n_chunks):
    v = v_ref[...].astype(F32)
    vc = v - jnp.mean(v, axis=-1, keepdims=True)
    vn = (vc * lax.rsqrt(jnp.mean(vc * vc, axis=-1, keepdims=True) + EPS) * g_ref[...]).astype(BF)
    dg = vn.shape[1] // GROUPS_A
    for g in range(GROUPS_A):
        w = ws_ref[g].astype(BF)
        bias = bs_ref[g]
        for c in range(n_chunks):
            rows = slice(c * CHUNK_A, (c + 1) * CHUNK_A)
            cols = slice(g * dg, (g + 1) * dg)
            sv = _dot(w, vn[rows, cols]) + bias
            o_ref[rows, cols] = (u_ref[rows, cols].astype(F32) * sv).astype(BF)


def _cmlp_call(p16, chunk_g, w_s, b_s, d_a, *, tm):
    m = p16.shape[0]
    n_chunks = tm // CHUNK_A
    return pl.pallas_call(
        functools.partial(_cmlp_kernel, n_chunks=n_chunks),
        grid=(m // tm,),
        in_specs=[
            pl.BlockSpec((tm, d_a), lambda i: (i, 0)),
            pl.BlockSpec((tm, d_a), lambda i: (i, 1)),
            pl.BlockSpec((1, d_a), lambda i: (0, 0)),
            pl.BlockSpec((GROUPS_A, CHUNK_A, CHUNK_A), lambda i: (0, 0, 0)),
            pl.BlockSpec((GROUPS_A, CHUNK_A, 1), lambda i: (0, 0, 0)),
        ],
        out_specs=pl.BlockSpec((tm, d_a), lambda i: (i, 0)),
        out_shape=jax.ShapeDtypeStruct((m, d_a), BF),
        compiler_params=_params(("parallel",)),
        name="cmlp",
    )(p16, p16, chunk_g, w_s, b_s[..., None])


def _hgrn_consts(fwd):
    c = HGRN_CHUNK
    t = np.arange(c)[:, None]
    u = np.arange(c)[None, :]

    def level_sum(b):
        half = b // 2
        mid = (t // b) * b + half
        upper = (t % b) >= half
        if fwd:
            return np.where(upper, (u >= mid) & (u <= t), (u > t) & (u < mid))
        return np.where(upper, (u >= mid) & (u < t), (u >= t) & (u < mid))

    def level_mask(b):
        half = b // 2
        upper = (t % b) >= half
        same = (t // b) == (u // b)
        if fwd:
            return same & upper & ((u % b) < half)
        return same & ~upper & ((u % b) >= half)

    lsum = np.concatenate([(u <= t) if fwd else (u >= t), level_sum(4), level_sum(8)], 0)
    masks = [t == u] + [level_mask(b) for b in HGRN_LEVELS] + [np.zeros((c, c), bool)]
    return (np.concatenate([lsum, lsum], 1).astype(np.float32),
            np.concatenate(masks, 1).astype(np.float32))


def _to_midpoint(cum, b, fwd):
    c, dk = cum.shape
    half = b // 2
    pieces = []
    for m in range(c // b):
        lo, mid = m * b, m * b + half
        ref = jnp.broadcast_to(cum[mid - 1:mid, :] if fwd else cum[mid:mid + 1, :], (half, dk))
        lower, upper = cum[lo:mid], cum[mid:lo + b]
        pieces += [ref - lower, upper - ref] if fwd else [lower - ref, ref - upper]
    return jnp.concatenate(pieces, axis=0)


def _hgrn_stage1(g, l2_ref):
    g_hi = g.astype(BF)
    g_lo = (g - g_hi.astype(F32)).astype(BF)
    return _dot(l2_ref[...], jnp.concatenate([g_hi, g_lo], axis=0))


def _hgrn_stage2(q, g, sums, mask_ref, fwd, with_out):
    c = HGRN_CHUNK
    dk = g.shape[1]
    f = jnp.exp2(g)
    k = 1.0 - f
    cum = sums[0:c]
    e_cum = jnp.exp2(cum)
    total = cum[c - 1:c] if fwd else cum[0:1]
    kend = (k * jnp.exp2(total - cum)).astype(BF)
    tot = e_cum[c - 1:c] if fwd else e_cum[0:1]
    if not with_out:
        return None, None, kend, tot

    row = lax.broadcasted_iota(jnp.int32, (c, dk), 0)
    e2 = jnp.where((row % 2) == (1 if fwd else 0), f, 1.0)
    es = [e2, jnp.exp2(sums[c:2 * c]), jnp.exp2(sums[2 * c:3 * c])]
    es += [jnp.exp2(_to_midpoint(cum, b, fwd)) for b in HGRN_LEVELS[3:]]
    qf = q.astype(F32)
    units = [(q, k.astype(BF))] + [((qf * e).astype(BF), (k * e).astype(BF)) for e in es]
    z = jnp.zeros((c, dk), BF)
    res = []
    for a in range(0, len(units) - 1, 2):
        (qa, ka), (qb, kb) = units[a], units[a + 1]
        kbd = jnp.concatenate([jnp.concatenate([ka, z], axis=1), jnp.concatenate([z, kb], axis=1)], axis=0)
        res.append(_dot_nt(jnp.concatenate([qa, qb], axis=1), kbd))
    q6, k6 = units[-1]
    res.append(_dot_nt(q6, jnp.concatenate([k6, z], axis=0)))
    p = sum(r * mask_ref[:, i * 2 * c:(i + 1) * 2 * c] for i, r in enumerate(res)).astype(BF)
    return p, (qf * e_cum).astype(BF), kend, tot


def _hgrn_stage3(p, qe, kend, tot, v, st, with_out):
    st_new = st * tot + _dot_tn(v, kend)
    if not with_out:
        return None, st_new
    o = _dot(p, jnp.concatenate([v, v], axis=0)) + _dot_nt(qe, st.astype(BF))
    return o, st_new


def _hgrn_kernel(*refs, n, with_out, with_init, with_final):
    it = iter(refs)
    q_ref = next(it) if with_out else None
    v_ref = next(it)
    gs_ref = next(it) if with_out else None
    gf_ref, gb_ref = next(it), next(it)
    gain_ref = next(it) if with_out else None
    sf0_ref, sb0_ref = (next(it), next(it)) if with_init else (None, None)
    l2_refs = (next(it), next(it))
    mask_refs = (next(it), next(it))
    y_ref = next(it) if with_out else None
    sf_ref, sb_ref = (next(it), next(it)) if with_final else (None, None)
    sums_scr, kend_scr, tot_scr, st_scr = next(it), next(it), next(it), next(it)
    p_scr, qe_scr, o_scr = (next(it), next(it), next(it)) if with_out else (None, None, None)

    c = HGRN_CHUNK
    nc = n // c
    dk = v_ref.shape[1]
    g_refs = (gf_ref, gb_ref)

    st_scr[0] = sf0_ref[...] if with_init else jnp.zeros((dk, dk), F32)
    st_scr[1] = sb0_ref[...] if with_init else jnp.zeros((dk, dk), F32)

    def rows_of(d, ci):
        start = ci * c if d == 0 else (nc - 1 - ci) * c
        return pl.ds(start if isinstance(start, int) else pl.multiple_of(start, c), c)

    def step(i, par, do1, do2, do3):
        for d in (0, 1):
            if do3:
                r = rows_of(d, i - 2)
                o, st = _hgrn_stage3(p_scr[d, par] if with_out else None,
                                     qe_scr[d, par] if with_out else None,
                                     kend_scr[d, par], tot_scr[d, par, 0:1, :],
                                     v_ref[r, :], st_scr[d], with_out)
                st_scr[d] = st
                if with_out:
                    o_scr[d, r, :] = o
            if do2:
                r = rows_of(d, i - 1)
                p, qe, kend, tot = _hgrn_stage2(q_ref[r, :] if with_out else None, g_refs[d][r, :],
                                                sums_scr[d, 1 - par], mask_refs[d], d == 0, with_out)
                kend_scr[d, 1 - par] = kend
                tot_scr[d, 1 - par, 0:1, :] = tot
                if with_out:
                    p_scr[d, 1 - par] = p
                    qe_scr[d, 1 - par] = qe
            if do1:
                sums_scr[d, par] = _hgrn_stage1(g_refs[d][rows_of(d, i), :], l2_refs[d])

    step(0, 0, True, False, False)
    step(1, 1, True, True, False)

    n_loop = (nc - 2) // HGRN_UNROLL

    def body(ii, _):
        for u in range(HGRN_UNROLL):
            step(2 + HGRN_UNROLL * ii + u, u % 2, True, True, True)
        return 0

    lax.fori_loop(0, n_loop, body, 0)
    for i in range(2 + HGRN_UNROLL * n_loop, nc):
        step(i, i % 2, True, True, True)
    step(nc, nc % 2, False, True, True)
    step(nc + 1, (nc + 1) % 2, False, False, True)

    if with_final:
        sf_ref[...] = st_scr[0]
        sb_ref[...] = st_scr[1]

    if with_out:
        rows = 256

        def readout(ri, _):
            r = pl.ds(pl.multiple_of(ri * rows, rows), rows)
            o = o_scr[0, r, :] + o_scr[1, r, :]
            o = o * lax.rsqrt(jnp.mean(o * o, axis=-1, keepdims=True) + EPS) * gain_ref[...]
            y_ref[r, :] = (o * gs_ref[r, :].astype(F32)).astype(BF)
            return 0

        lax.fori_loop(0, n // rows, readout, 0, unroll=2)


def _hgrn_call(p16, pf, gain, init, n, cols, *, with_out, with_final):
    m = p16.shape[0]
    bsz = m // n
    dk = HGRN_DK
    with_init = init is not None
    cq, cv, cg = cols
    tok = lambda col: pl.BlockSpec((n, dk), lambda b, h: (b, col + h))
    st_spec = pl.BlockSpec((None, None, dk, dk), lambda b, h: (b, h, 0, 0))
    full = lambda a: pl.BlockSpec(a.shape, lambda b, h: (0,) * a.ndim)

    assert (n // HGRN_CHUNK) % 2 == 0 and n % HGRN_CHUNK == 0
    lf, mf = _hgrn_consts(True)
    lbw, mb = _hgrn_consts(False)
    consts = [jnp.asarray(lf, BF), jnp.asarray(lbw, BF), jnp.asarray(mf, F32), jnp.asarray(mb, F32)]

    args, specs = [], []
    if with_out:
        args.append(p16); specs.append(tok(cq))
    args.append(p16); specs.append(tok(cv))
    if with_out:
        args.append(p16); specs.append(tok(cg))
    args += [pf, pf]; specs += [tok(0), tok(HEADS_B)]
    if with_out:
        args.append(gain); specs.append(pl.BlockSpec((1, dk), lambda b, h: (0, h)))
    if with_init:
        args += list(init); specs += [st_spec, st_spec]
    args += consts; specs += [full(a) for a in consts]

    out_shape, out_specs = [], []
    if with_out:
        out_shape.append(jax.ShapeDtypeStruct((m, HEADS_B * dk), BF)); out_specs.append(tok(0))
    if with_final:
        st_shape = jax.ShapeDtypeStruct((bsz, HEADS_B, dk, dk), F32)
        out_shape += [st_shape, st_shape]; out_specs += [st_spec, st_spec]
    c = HGRN_CHUNK
    scratch = [pltpu.VMEM((2, 2, lf.shape[0], dk), F32), pltpu.VMEM((2, 2, c, dk), BF),
               pltpu.VMEM((2, 2, 8, dk), F32), pltpu.VMEM((2, dk, dk), F32)]
    if with_out:
        scratch += [pltpu.VMEM((2, 2, c, 2 * c), BF), pltpu.VMEM((2, 2, c, dk), BF),
                    pltpu.VMEM((2, n, dk), F32)]

    kern = functools.partial(_hgrn_kernel, n=n, with_out=with_out, with_init=with_init, with_final=with_final)
    return pl.pallas_call(
        kern,
        grid=(bsz, HEADS_B),
        in_specs=specs,
        out_specs=out_specs,
        out_shape=out_shape,
        scratch_shapes=scratch,
        compiler_params=_params(("parallel", "parallel")),
        name="hgrn",
    )(*args)


def _merge_kernel(ya_ref, yb_ref, ga_ref, gb_ref, wa_ref, wb_ref, wo_ref, x_ref, gt_ref, gpost_ref,
                  o_ref, rs_scr, *, row_div, row_const, n_j):
    j = pl.program_id(1)
    row = _mod_row(row_div, row_const)
    ma = _dot(ya_ref[...], wa_ref[...].astype(BF))
    mb = _dot(yb_ref[...], wb_ref[...].astype(BF))
    mm = (ga_ref[...].astype(F32) * ma + gb_ref[...].astype(F32) * mb).astype(BF)

    @pl.when(j == 0)
    def _():
        o_ref[...] = jnp.zeros_like(o_ref)

    o_ref[...] += _dot(mm, wo_ref[...].astype(BF))

    @pl.when(j == n_j - 1)
    def _():
        _residual_into(o_ref, rs_scr, x_ref, gpost_ref, gt_ref, row, 1.0)


def _merge_call(x, ya, yb, p16, gate_cols, mod_l, mod_k, gpost, w_up_a, w_up_b, w_out, l,
                *, tm, row_div, row_const):
    m, d = x.shape
    da, db = ya.shape[1], yb.shape[1]
    tn = MERGE_TN
    n_j = d // tn
    ca, cb = (c // tn for c in gate_cols)
    kern = functools.partial(_merge_kernel, row_div=row_div, row_const=row_const, n_j=n_j)
    return pl.pallas_call(
        kern,
        grid=(m // tm, n_j),
        in_specs=[
            pl.BlockSpec((tm, da), lambda i, j: (i, 0)),
            pl.BlockSpec((tm, db), lambda i, j: (i, 0)),
            pl.BlockSpec((tm, tn), lambda i, j: (i, ca + j)),
            pl.BlockSpec((tm, tn), lambda i, j: (i, cb + j)),
            pl.BlockSpec((None, da, tn), lambda i, j: (l, 0, j)),
            pl.BlockSpec((None, db, tn), lambda i, j: (l, 0, j)),
            pl.BlockSpec((None, tn, d), lambda i, j: (l, j, 0)),
            pl.BlockSpec((tm, d), lambda i, j: (i, 0), pipeline_mode=pl.Buffered(1)),
            pl.BlockSpec((8, d), lambda i, j: (0, mod_k)),
            pl.BlockSpec((1, d), lambda i, j: (0, 0)),
        ],
        out_specs=pl.BlockSpec((tm, d), lambda i, j: (i, 0)),
        out_shape=jax.ShapeDtypeStruct((m, d), F32),
        scratch_shapes=[pltpu.VMEM((tm, LANES), F32)],
        compiler_params=_params(("parallel", "arbitrary")),
        name="merge",
    )(ya, yb, p16, p16, w_up_a, w_up_b, w_out, x, mod_l, gpost)


def _win_kinds(groups):
    return [kind for kind, width in groups for _ in range(width // WIN_TN)]


def kernel(x, c, ctx, c_ctx, w_mod, b_mod, norm_g, ffn1_w_gu, ffn1_w_down, ffn2_w_gu, ffn2_w_down,
           w_in, chunk_norm_g, w_spatial, b_spatial, lb_logits, hgrn_norm_g, w_up_a, w_up_b, w_out):
    bsz, n, d = x.shape
    n_ctx = ctx.shape[1]
    depth = w_mod.shape[0]
    d_a = chunk_norm_g.shape[1]
    d_b = hgrn_norm_g.shape[1]
    ctx_row = bsz

    cs = jnp.concatenate([c, c_ctx[None, :], jnp.zeros((8 - bsz - 1, d), F32)], axis=0)
    mod = _mod_call(cs, w_mod, b_mod)
    lb2 = lb_logits.reshape(depth, -1)

    xl = x.reshape(bsz * n, d)
    xc = ctx.reshape(bsz * n_ctx, d)
    tm = 1024
    lat = dict(tm=tm, row_div=n // tm, row_const=None)
    cx = dict(tm=bsz * n_ctx, row_div=None, row_const=ctx_row)
    col_q, col_i, col_g = 2 * d_a // 128, (2 * d_a + d_b) // 128, (2 * d_a + 2 * d_b) // 128
    gate_cols = (2 * d_a + 3 * d_b, 2 * d_a + 3 * d_b + d)
    kinds_full = _win_kinds([("gelu", 2 * d_a), ("id", d_b), ("logf", 2 * d_b), ("id", d_b),
                             ("silu", d_b), ("sigmoid", 2 * d)])
    kinds_state = _win_kinds([("logf", 2 * d_b), ("id", d_b)])

    for l in range(depth):
        last = l == depth - 1
        ml = mod[l]
        g = [norm_g[l, k][None, :] for k in range(norm_g.shape[1])]
        ffn1 = functools.partial(_ffn_call, mod_l=ml, mod_k=0, gpre=g[0], gpost=g[1],
                                 w_gu=ffn1_w_gu, w_down=ffn1_w_down, l=l)
        ffn2 = functools.partial(_ffn_call, mod_l=ml, mod_k=6, gpre=g[4], gpost=g[5],
                                 w_gu=ffn2_w_gu, w_down=ffn2_w_down, l=l)

        xc, w16_1 = ffn1(xc, tf=FFN_TF, emit16=True, **cx)
        if not last:
            pc16, pcf = _win_call(xc, ml, 3, g[2], w_in, lb2, l, col0=0, kinds=kinds_full, **cx)
            ybc, s_f, s_b = _hgrn_call(pc16, pcf, hgrn_norm_g[l][None, :], None, n_ctx,
                                       (col_q, col_i, col_g), with_out=True, with_final=True)
            yac = _cmlp_call(pc16, chunk_norm_g[l][None, :], w_spatial[l], b_spatial[l], d_a, tm=512)
            xc = _merge_call(xc, yac, ybc, pc16, gate_cols, ml, 5, g[3], w_up_a, w_up_b, w_out, l, **cx)
            xc, w16_2 = ffn2(xc, tf=FFN_TF, emit16=True, **cx)
        else:
            pc16, pcf = _win_call(xc, ml, 3, g[2], w_in, lb2, l, col0=(2 * d_a + d_b) // WIN_TN,
                                  kinds=kinds_state, **cx)
            s_f, s_b = _hgrn_call(pc16, pcf, None, None, n_ctx, (0, 0, 0), with_out=False, with_final=True)
            w16_2 = None

        xl = ffn1(xl, tf=FFN_TF16, w16=w16_1, **lat)
        p16, pf = _win_call(xl, ml, 3, g[2], w_in, lb2, l, col0=0, kinds=kinds_full, **lat)
        (yb,) = _hgrn_call(p16, pf, hgrn_norm_g[l][None, :], (s_f, s_b), n,
                           (col_q, col_i, col_g), with_out=True, with_final=False)
        ya = _cmlp_call(p16, chunk_norm_g[l][None, :], w_spatial[l], b_spatial[l], d_a, tm=512)
        xl = _merge_call(xl, ya, yb, p16, gate_cols, ml, 5, g[3], w_up_a, w_up_b, w_out, l, **lat)
        if w16_2 is None:
            xl = ffn2(xl, tf=FFN_TF, **lat)
        else:
            xl = ffn2(xl, tf=FFN_TF16, w16=w16_2, **lat)
    return xl.reshape(bsz, n, d)
```

```python
import functools

import numpy as np
import jax
import jax.numpy as jnp
from jax import lax
from jax.experimental import pallas as pl
from jax.experimental.pallas import tpu as pltpu

BF = jnp.bfloat16
F32 = jnp.float32

EPS = 1e-6
MACARON = 0.5
F_FLOOR = 1e-30
N_MOD = 9
GROUPS_A = 8
CHUNK_A = 128
HEADS_B = 8
LANES = 128
HGRN_DK = 128
HGRN_CHUNK = 64
HGRN_UNROLL = 14
HGRN_LEVELS = (2, 4, 8, 16, 32, 64)

VMEM_LIMIT = 56 * 1024 * 1024


def _dot(a, b):
    return jnp.dot(a, b, preferred_element_type=F32)


def _dot_nt(a, b):
    return lax.dot_general(a, b, (((1,), (1,)), ((), ())), preferred_element_type=F32)


def _dot_tn(a, b):
    return lax.dot_general(a, b, (((0,), (0,)), ((), ())), preferred_element_type=F32)


def _rms(x, g):
    return x * lax.rsqrt(jnp.mean(x * x, axis=-1, keepdims=True) + EPS) * g


def _params(sem):
    return pltpu.CompilerParams(dimension_semantics=sem, vmem_limit_bytes=VMEM_LIMIT)


def _mod_kernel(cs_ref, w_ref, b_ref, o_ref):
    cs = cs_ref[...]
    s = (cs * jax.nn.sigmoid(cs)).astype(BF)
    o_ref[...] = _dot(s, w_ref[...].astype(BF)) + b_ref[...]


def _mod_call(cs, w_mod, b_mod):
    depth, d, n = w_mod.shape
    tn = 1024
    return pl.pallas_call(
        _mod_kernel,
        grid=(depth, n // tn),
        in_specs=[
            pl.BlockSpec((8, d), lambda l, j: (0, 0)),
            pl.BlockSpec((None, d, tn), lambda l, j: (l, 0, j)),
            pl.BlockSpec((None, 1, tn), lambda l, j: (l, 0, j)),
        ],
        out_specs=pl.BlockSpec((None, 8, tn), lambda l, j: (l, 0, j)),
        out_shape=jax.ShapeDtypeStruct((depth, 8, n), F32),
        compiler_params=_params(("parallel", "parallel")),
        name="mod",
    )(cs, w_mod, b_mod.reshape(depth, 1, n))


def _mod_row(row_div, row_const):
    if row_div is None:
        return row_const
    return pl.program_id(0) // row_div


ROW_CHUNK = 64
FFN_TF = 256
FFN_TF16 = 512


def _row_loop(n_rows, fn, unroll=2):
    def body(r, _):
        fn(pl.ds(pl.multiple_of(r * ROW_CHUNK, ROW_CHUNK), ROW_CHUNK))
        return 0

    lax.fori_loop(0, n_rows // ROW_CHUNK, body, 0, unroll=unroll)


def _row_rsqrt_ms(rs_scr, src_ref):
    def rows(sl):
        v = src_ref[sl, :]
        rs = lax.rsqrt(jnp.mean(v * v, axis=-1, keepdims=True) + EPS)
        rs_scr[sl, :] = jnp.broadcast_to(rs, (ROW_CHUNK, rs_scr.shape[1]))

    _row_loop(src_ref.shape[0], rows, unroll=4)


def _lane_tile(v, width):
    return jnp.concatenate([v] * (width // v.shape[1]), axis=1)


def _modulate_into(h_scr, x_ref, gpre_ref, sc_ref, sh_ref, row):
    gain = gpre_ref[...] * (1.0 + sc_ref[pl.ds(row, 1), :])
    shift = sh_ref[pl.ds(row, 1), :]

    def rows(sl):
        x = x_ref[sl, :]
        rs = lax.rsqrt(jnp.mean(x * x, axis=-1, keepdims=True) + EPS)
        h_scr[sl, :] = (x * rs * gain + shift).astype(BF)

    _row_loop(x_ref.shape[0], rows)


def _residual_into(o_ref, rs_scr, x_ref, gpost_ref, gt_ref, row, weight):
    gain = gpost_ref[...] * gt_ref[pl.ds(row, 1), :] * weight
    d = o_ref.shape[1]
    _row_rsqrt_ms(rs_scr, o_ref)

    def rows(sl):
        o_ref[sl, :] = x_ref[sl, :] + o_ref[sl, :] * _lane_tile(rs_scr[sl, :], d) * gain

    _row_loop(o_ref.shape[0], rows)


def _ffn_kernel(x_ref, sh_ref, sc_ref, gt_ref, gpre_ref, gpost_ref, wg_ref, wu_ref, wd_ref, *rest,
                row_div, row_const, n_f, emit16):
    if emit16:
        o_ref, wg16_ref, wu16_ref, wd16_ref, h_scr, rs_scr = rest
    else:
        o_ref, h_scr, rs_scr = rest
    j = pl.program_id(1)
    row = _mod_row(row_div, row_const)

    @pl.when(j == 0)
    def _():
        _modulate_into(h_scr, x_ref, gpre_ref, sc_ref, sh_ref, row)
        o_ref[...] = jnp.zeros_like(o_ref)

    wg, wu, wd = wg_ref[...].astype(BF), wu_ref[...].astype(BF), wd_ref[...].astype(BF)
    if emit16:
        wg16_ref[...] = wg
        wu16_ref[...] = wu
        wd16_ref[...] = wd
    h = h_scr[...]
    a = _dot(h, wg)
    b = _dot(h, wu)
    act = (a * jax.nn.sigmoid(a) * b).astype(BF)
    o_ref[...] += _dot(act, wd)

    @pl.when(j == n_f - 1)
    def _():
        _residual_into(o_ref, rs_scr, x_ref, gpost_ref, gt_ref, row, MACARON)


def _ffn_call(x, mod_l, mod_k, gpre, gpost, w_gu, w_down, l, *, tm, tf, row_div, row_const,
              w16=None, emit16=False):
    m, d = x.shape
    d_ff = w_down.shape[1]
    n_f = d_ff // tf
    kern = functools.partial(_ffn_kernel, row_div=row_div, row_const=row_const, n_f=n_f, emit16=emit16)
    mod_spec = lambda k: pl.BlockSpec((8, d), lambda i, j: (0, k))
    col_tile = pl.BlockSpec((d, tf), lambda i, j: (0, j))
    row_tile = pl.BlockSpec((tf, d), lambda i, j: (j, 0))
    if w16 is None:
        weights = (w_gu, w_gu, w_down)
        w_specs = [pl.BlockSpec((None, d, tf), lambda i, j: (l, 0, j)),
                   pl.BlockSpec((None, d, tf), lambda i, j: (l, 0, n_f + j)),
                   pl.BlockSpec((None, tf, d), lambda i, j: (l, j, 0))]
    else:
        weights = w16
        w_specs = [col_tile, col_tile, row_tile]
    out_specs = [pl.BlockSpec((tm, d), lambda i, j: (i, 0))]
    out_shape = [jax.ShapeDtypeStruct((m, d), F32)]
    if emit16:
        assert m == tm
        out_specs += [col_tile, col_tile, row_tile]
        out_shape += [jax.ShapeDtypeStruct((d, d_ff), BF), jax.ShapeDtypeStruct((d, d_ff), BF),
                      jax.ShapeDtypeStruct((d_ff, d), BF)]
    out = pl.pallas_call(
        kern,
        grid=(m // tm, n_f),
        in_specs=[
            pl.BlockSpec((tm, d), lambda i, j: (i, 0), pipeline_mode=pl.Buffered(1)),
            mod_spec(mod_k), mod_spec(mod_k + 1), mod_spec(mod_k + 2),
            pl.BlockSpec((1, d), lambda i, j: (0, 0)),
            pl.BlockSpec((1, d), lambda i, j: (0, 0)),
        ] + w_specs,
        out_specs=out_specs,
        out_shape=out_shape,
        scratch_shapes=[pltpu.VMEM((tm, d), BF), pltpu.VMEM((tm, LANES), F32)],
        compiler_params=_params(("parallel", "arbitrary")),
        name="ffn",
    )(x, mod_l, mod_l, mod_l, gpre, gpost, *weights)
    return (out[0], tuple(out[1:])) if emit16 else out[0]


WIN_TN = 1024
WIN_SLABS = 4
MERGE_TN = 512
MERGE_SLABS = 2


def _gelu_tanh(z):
    return z * jax.nn.sigmoid((2.0 * 0.7978845608028654) * (z + 0.044715 * (z * z * z)))


def _win_kernel(x_ref, sh_ref, sc_ref, gpre_ref, w_ref, lb_ref, p16_ref, pf_ref, h_scr,
                *, row_div, row_const, kinds, layer):
    j = pl.program_id(1)
    row = _mod_row(row_div, row_const)

    @pl.when(j == 0)
    def _():
        _modulate_into(h_scr, x_ref, gpre_ref, sc_ref, sh_ref, row)

    def in_kind(kind):
        cond = jnp.bool_(False)
        t = 0
        while t < len(kinds):
            t1 = t
            while t1 + 1 < len(kinds) and kinds[t1 + 1] == kinds[t]:
                t1 += 1
            if kinds[t] == kind:
                cond = cond | ((j >= t) & (j <= t1))
            t = t1 + 1
        return cond

    def slabs(out_ref, fn):
        w = w_ref[...].astype(BF)
        tm = h_scr.shape[0]
        for r0 in range(0, tm, tm // WIN_SLABS):
            sl = slice(r0, r0 + tm // WIN_SLABS)
            out_ref[sl, :] = fn(_dot(h_scr[sl, :], w)).astype(out_ref.dtype)

    def logf(z):
        ll = lb_ref[...]
        e = jnp.exp(ll - jnp.max(ll, axis=0, keepdims=True))
        p = e / jnp.sum(e, axis=0, keepdims=True)
        lb = jnp.zeros_like(p[0:1])
        for r in range(1, layer + 1):
            lb = lb + p[r:r + 1]
        return jnp.log2(jnp.maximum(lb + (1.0 - lb) * jax.nn.sigmoid(z), F_FLOOR))

    for kind, out_ref, fn in (("gelu", p16_ref, _gelu_tanh), ("id", p16_ref, lambda z: z),
                              ("silu", p16_ref, lambda z: z * jax.nn.sigmoid(z)),
                              ("sigmoid", p16_ref, jax.nn.sigmoid), ("logf", pf_ref, logf)):
        if kind in kinds:
            pl.when(in_kind(kind))(functools.partial(slabs, out_ref, fn))


def _win_call(x, mod_l, mod_k, gpre, w_in, lb_logits2, l, *, tm, row_div, row_const, col0, kinds):
    m, d = x.shape
    tn = WIN_TN
    nj = len(kinds)
    is_f = np.array([k == "logf" for k in kinds])
    n16, nf = int((~is_f).sum()), int(is_f.sum())
    c16 = np.maximum(np.cumsum(~is_f) - 1, 0)
    cf = np.maximum(np.cumsum(is_f) - 1, 0)
    f0 = int(np.argmax(is_f))

    def sel(table):
        def f(j):
            out = jnp.int32(int(table[0]))
            for t in range(1, nj):
                if table[t] != table[t - 1]:
                    out = jnp.where(j >= t, jnp.int32(int(table[t])), out)
            return out
        return f

    s16, sf = sel(c16), sel(cf)
    kern = functools.partial(_win_kernel, row_div=row_div, row_const=row_const, kinds=tuple(kinds), layer=l)
    mod_spec = lambda k: pl.BlockSpec((8, d), lambda i, j: (0, k))
    depth = lb_logits2.shape[0]
    return pl.pallas_call(
        kern,
        grid=(m // tm, nj),
        in_specs=[
            pl.BlockSpec((tm, d), lambda i, j: (i, 0), pipeline_mode=pl.Buffered(1)),
            mod_spec(mod_k), mod_spec(mod_k + 1),
            pl.BlockSpec((1, d), lambda i, j: (0, 0)),
            pl.BlockSpec((None, d, tn), lambda i, j: (l, 0, col0 + j)),
            pl.BlockSpec((depth, tn), lambda i, j: (0, jnp.clip(j - f0, 0, nf - 1))),
        ],
        out_specs=[
            pl.BlockSpec((tm, tn), lambda i, j: (i, s16(j))),
            pl.BlockSpec((tm, tn), lambda i, j: (i, sf(j))),
        ],
        out_shape=[jax.ShapeDtypeStruct((m, n16 * tn), BF), jax.ShapeDtypeStruct((m, nf * tn), F32)],
        scratch_shapes=[pltpu.VMEM((tm, d), BF)],
        compiler_params=_params(("parallel", "arbitrary")),
        name="win",
    )(x, mod_l, mod_l, gpre, w_in, lb_logits2)


def _cmlp_kernel(u_ref, v_ref, g_ref, ws_ref, bs_ref, o_ref, *, n_chunks):
    v = v_ref[...].astype(F32)
    vc = v - jnp.mean(v, axis=-1, keepdims=True)
    vn = (vc * lax.rsqrt(jnp.mean(vc * vc, axis=-1, keepdims=True) + EPS) * g_ref[...]).astype(BF)
    dg = vn.shape[1] // GROUPS_A
    for g in range(GROUPS_A):
        w = ws_ref[g].astype(BF)
        bias = bs_ref[g]
        for c in range(n_chunks):
            rows = slice(c * CHUNK_A, (c + 1) * CHUNK_A)
            cols = slice(g * dg, (g + 1) * dg)
            sv = _dot(w, vn[rows, cols]) + bias
            o_ref[rows, cols] = (u_ref[rows, cols].astype(F32) * sv).astype(BF)


def _cmlp_call(p16, chunk_g, w_s, b_s, d_a, *, tm):
    m = p16.shape[0]
    n_chunks = tm // CHUNK_A
    return pl.pallas_call(
        functools.partial(_cmlp_kernel, n_chunks=n_chunks),
        grid=(m // tm,),
        in_specs=[
            pl.BlockSpec((tm, d_a), lambda i: (i, 0)),
            pl.BlockSpec((tm, d_a), lambda i: (i, 1)),
            pl.BlockSpec((1, d_a), lambda i: (0, 0)),
            pl.BlockSpec((GROUPS_A, CHUNK_A, CHUNK_A), lambda i: (0, 0, 0)),
            pl.BlockSpec((GROUPS_A, CHUNK_A, 1), lambda i: (0, 0, 0)),
        ],
        out_specs=pl.BlockSpec((tm, d_a), lambda i: (i, 0)),
        out_shape=jax.ShapeDtypeStruct((m, d_a), BF),
        compiler_params=_params(("parallel",)),
        name="cmlp",
    )(p16, p16, chunk_g, w_s, b_s[..., None])


def _hgrn_consts(fwd):
    c = HGRN_CHUNK
    t = np.arange(c)[:, None]
    u = np.arange(c)[None, :]

    def level_sum(b):
        half = b // 2
        mid = (t // b) * b + half
        upper = (t % b) >= half
        if fwd:
            return np.where(upper, (u >= mid) & (u <= t), (u > t) & (u < mid))
        return np.where(upper, (u >= mid) & (u < t), (u >= t) & (u < mid))

    def level_mask(b):
        half = b // 2
        upper = (t % b) >= half
        same = (t // b) == (u // b)
        if fwd:
            return same & upper & ((u % b) < half)
        return same & ~upper & ((u % b) >= half)

    lsum = np.concatenate([(u <= t) if fwd else (u >= t), level_sum(4), level_sum(8)], 0)
    masks = [t == u] + [level_mask(b) for b in HGRN_LEVELS] + [np.zeros((c, c), bool)]
    return (np.concatenate([lsum, lsum], 1).astype(np.float32),
            np.concatenate(masks, 1).astype(np.float32))


def _to_midpoint(cum, b, fwd):
    c, dk = cum.shape
    half = b // 2
    pieces = []
    for m in range(c // b):
        lo, mid = m * b, m * b + half
        ref = jnp.broadcast_to(cum[mid - 1:mid, :] if fwd else cum[mid:mid + 1, :], (half, dk))
        lower, upper = cum[lo:mid], cum[mid:lo + b]
        pieces += [ref - lower, upper - ref] if fwd else [lower - ref, ref - upper]
    return jnp.concatenate(pieces, axis=0)


def _hgrn_stage1(g, l2_ref):
    g_hi = g.astype(BF)
    g_lo = (g - g_hi.astype(F32)).astype(BF)
    return _dot(l2_ref[...], jnp.concatenate([g_hi, g_lo], axis=0))


def _hgrn_stage2(q, g, sums, mask_ref, fwd, with_out):
    c = HGRN_CHUNK
    dk = g.shape[1]
    f = jnp.exp2(g)
    k = 1.0 - f
    cum = sums[0:c]
    e_cum = jnp.exp2(cum)
    total = cum[c - 1:c] if fwd else cum[0:1]
    kend = (k * jnp.exp2(total - cum)).astype(BF)
    tot = e_cum[c - 1:c] if fwd else e_cum[0:1]
    if not with_out:
        return None, None, kend, tot

    row = lax.broadcasted_iota(jnp.int32, (c, dk), 0)
    e2 = jnp.where((row % 2) == (1 if fwd else 0), f, 1.0)
    es = [e2, jnp.exp2(sums[c:2 * c]), jnp.exp2(sums[2 * c:3 * c])]
    es += [jnp.exp2(_to_midpoint(cum, b, fwd)) for b in HGRN_LEVELS[3:]]
    qf = q.astype(F32)
    units = [(q, k.astype(BF))] + [((qf * e).astype(BF), (k * e).astype(BF)) for e in es]
    z = jnp.zeros((c, dk), BF)
    res = []
    for a in range(0, len(units) - 1, 2):
        (qa, ka), (qb, kb) = units[a], units[a + 1]
        kbd = jnp.concatenate([jnp.concatenate([ka, z], axis=1), jnp.concatenate([z, kb], axis=1)], axis=0)
        res.append(_dot_nt(jnp.concatenate([qa, qb], axis=1), kbd))
    q6, k6 = units[-1]
    res.append(_dot_nt(q6, jnp.concatenate([k6, z], axis=0)))
    p = sum(r * mask_ref[:, i * 2 * c:(i + 1) * 2 * c] for i, r in enumerate(res)).astype(BF)
    return p, (qf * e_cum).astype(BF), kend, tot


def _hgrn_stage3(p, qe, kend, tot, v, st, with_out):
    st_new = st * tot + _dot_tn(v, kend)
    if not with_out:
        return None, st_new
    o = _dot(p, jnp.concatenate([v, v], axis=0)) + _dot_nt(qe, st.astype(BF))
    return o, st_new


def _hgrn_kernel(*refs, n, with_out, with_init, with_final):
    it = iter(refs)
    q_ref = next(it) if with_out else None
    v_ref = next(it)
    gs_ref = next(it) if with_out else None
    gf_ref, gb_ref = next(it), next(it)
    gain_ref = next(it) if with_out else None
    sf0_ref, sb0_ref = (next(it), next(it)) if with_init else (None, None)
    l2_refs = (next(it), next(it))
    mask_refs = (next(it), next(it))
    y_ref = next(it) if with_out else None
    sf_ref, sb_ref = (next(it), next(it)) if with_final else (None, None)
    sums_scr, kend_scr, tot_scr, st_scr = next(it), next(it), next(it), next(it)
    p_scr, qe_scr, o_scr = (next(it), next(it), next(it)) if with_out else (None, None, None)

    c = HGRN_CHUNK
    nc = n // c
    dk = v_ref.shape[1]
    g_refs = (gf_ref, gb_ref)

    st_scr[0] = sf0_ref[...] if with_init else jnp.zeros((dk, dk), F32)
    st_scr[1] = sb0_ref[...] if with_init else jnp.zeros((dk, dk), F32)

    def rows_of(d, ci):
        start = ci * c if d == 0 else (nc - 1 - ci) * c
        return pl.ds(start if isinstance(start, int) else pl.multiple_of(start, c), c)

    def step(i, par, do1, do2, do3):
        for d in (0, 1):
            if do3:
                r = rows_of(d, i - 2)
                o, st = _hgrn_stage3(p_scr[d, par] if with_out else None,
                                     qe_scr[d, par] if with_out else None,
                                     kend_scr[d, par], tot_scr[d, par, 0:1, :],
                                     v_ref[r, :], st_scr[d], with_out)
                st_scr[d] = st
                if with_out:
                    o_scr[d, r, :] = o
            if do2:
                r = rows_of(d, i - 1)
                p, qe, kend, tot = _hgrn_stage2(q_ref[r, :] if with_out else None, g_refs[d][r, :],
                                                sums_scr[d, 1 - par], mask_refs[d], d == 0, with_out)
                kend_scr[d, 1 - par] = kend
                tot_scr[d, 1 - par, 0:1, :] = tot
                if with_out:
                    p_scr[d, 1 - par] = p
                    qe_scr[d, 1 - par] = qe
            if do1:
                sums_scr[d, par] = _hgrn_stage1(g_refs[d][rows_of(d, i), :], l2_refs[d])

    step(0, 0, True, False, False)
    step(1, 1, True, True, False)

    n_loop = (nc - 2) // HGRN_UNROLL

    def body(ii, _):
        for u in range(HGRN_UNROLL):
            step(2 + HGRN_UNROLL * ii + u, u % 2, True, True, True)
        return 0

    lax.fori_loop(0, n_loop, body, 0)
    for i in range(2 + HGRN_UNROLL * n_loop, nc):
        step(i, i % 2, True, True, True)
    step(nc, nc % 2, False, True, True)
    step(nc + 1, (nc + 1) % 2, False, False, True)

    if with_final:
        sf_ref[...] = st_scr[0]
        sb_ref[...] = st_scr[1]

    if with_out:
        rows = 256

        def readout(ri, _):
            r = pl.ds(pl.multiple_of(ri * rows, rows), rows)
            o = o_scr[0, r, :] + o_scr[1, r, :]
            o = o * lax.rsqrt(jnp.mean(o * o, axis=-1, keepdims=True) + EPS) * gain_ref[...]
            y_ref[r, :] = (o * gs_ref[r, :].astype(F32)).astype(BF)
            return 0

        lax.fori_loop(0, n // rows, readout, 0, unroll=2)


def _hgrn_call(p16, pf, gain, init, n, cols, *, with_out, with_final):
    m = p16.shape[0]
    bsz = m // n
    dk = HGRN_DK
    with_init = init is not None
    cq, cv, cg = cols
    tok = lambda col: pl.BlockSpec((n, dk), lambda b, h: (b, col + h))
    st_spec = pl.BlockSpec((None, None, dk, dk), lambda b, h: (b, h, 0, 0))
    full = lambda a: pl.BlockSpec(a.shape, lambda b, h: (0,) * a.ndim)

    assert (n // HGRN_CHUNK) % 2 == 0 and n % HGRN_CHUNK == 0
    lf, mf = _hgrn_consts(True)
    lbw, mb = _hgrn_consts(False)
    consts = [jnp.asarray(lf, BF), jnp.asarray(lbw, BF), jnp.asarray(mf, F32), jnp.asarray(mb, F32)]

    args, specs = [], []
    if with_out:
        args.append(p16); specs.append(tok(cq))
    args.append(p16); specs.append(tok(cv))
    if with_out:
        args.append(p16); specs.append(tok(cg))
    args += [pf, pf]; specs += [tok(0), tok(HEADS_B)]
    if with_out:
        args.append(gain); specs.append(pl.BlockSpec((1, dk), lambda b, h: (0, h)))
    if with_init:
        args += list(init); specs += [st_spec, st_spec]
    args += consts; specs += [full(a) for a in consts]

    out_shape, out_specs = [], []
    if with_out:
        out_shape.append(jax.ShapeDtypeStruct((m, HEADS_B * dk), BF)); out_specs.append(tok(0))
    if with_final:
        st_shape = jax.ShapeDtypeStruct((bsz, HEADS_B, dk, dk), F32)
        out_shape += [st_shape, st_shape]; out_specs += [st_spec, st_spec]
    c = HGRN_CHUNK
    scratch = [pltpu.VMEM((2, 2, lf.shape[0], dk), F32), pltpu.VMEM((2, 2, c, dk), BF),
               pltpu.VMEM((2, 2, 8, dk), F32), pltpu.VMEM((2, dk, dk), F32)]
    if with_out:
        scratch += [pltpu.VMEM((2, 2, c, 2 * c), BF), pltpu.VMEM((2, 2, c, dk), BF),
                    pltpu.VMEM((2, n, dk), F32)]

    kern = functools.partial(_hgrn_kernel, n=n, with_out=with_out, with_init=with_init, with_final=with_final)
    return pl.pallas_call(
        kern,
        grid=(bsz, HEADS_B),
        in_specs=specs,
        out_specs=out_specs,
        out_shape=out_shape,
        scratch_shapes=scratch,
        compiler_params=_params(("parallel", "parallel")),
        name="hgrn",
    )(*args)


def _cast16_kernel(w_ref, o_ref):
    o_ref[...] = w_ref[...].astype(BF)


def _cast16(w, l, tr=256):
    _, r, c = w.shape
    return pl.pallas_call(
        _cast16_kernel,
        grid=(r // tr,),
        in_specs=[pl.BlockSpec((None, tr, c), lambda i: (l, i, 0))],
        out_specs=pl.BlockSpec((tr, c), lambda i: (i, 0)),
        out_shape=jax.ShapeDtypeStruct((r, c), BF),
        compiler_params=_params(("parallel",)),
        name="cast16",
    )(w)


def _merge_kernel(ya_ref, yb_ref, ga_ref, gb_ref, wa_ref, wb_ref, wo_ref, x_ref, gt_ref, gpost_ref,
                  o_ref, rs_scr, *, row_div, row_const, n_j):
    j = pl.program_id(1)
    row = _mod_row(row_div, row_const)

    @pl.when(j == 0)
    def _():
        o_ref[...] = jnp.zeros_like(o_ref)

    wa, wb, wo = wa_ref[...], wb_ref[...], wo_ref[...]
    tm = o_ref.shape[0]
    for r0 in range(0, tm, tm // MERGE_SLABS):
        sl = slice(r0, r0 + tm // MERGE_SLABS)
        ma = _dot(ya_ref[sl, :], wa)
        mb = _dot(yb_ref[sl, :], wb)
        mm = (ga_ref[sl, :].astype(F32) * ma + gb_ref[sl, :].astype(F32) * mb).astype(BF)
        o_ref[sl, :] += _dot(mm, wo)

    @pl.when(j == n_j - 1)
    def _():
        _residual_into(o_ref, rs_scr, x_ref, gpost_ref, gt_ref, row, 1.0)


def _merge_call(x, ya, yb, p16, gate_cols, mod_l, mod_k, gpost, w16, *, tm, row_div, row_const):
    m, d = x.shape
    da, db = ya.shape[1], yb.shape[1]
    tn = MERGE_TN
    n_j = d // tn
    ca, cb = (c // tn for c in gate_cols)
    kern = functools.partial(_merge_kernel, row_div=row_div, row_const=row_const, n_j=n_j)
    return pl.pallas_call(
        kern,
        grid=(m // tm, n_j),
        in_specs=[
            pl.BlockSpec((tm, da), lambda i, j: (i, 0)),
            pl.BlockSpec((tm, db), lambda i, j: (i, 0)),
            pl.BlockSpec((tm, tn), lambda i, j: (i, ca + j)),
            pl.BlockSpec((tm, tn), lambda i, j: (i, cb + j)),
            pl.BlockSpec((da, tn), lambda i, j: (0, j)),
            pl.BlockSpec((db, tn), lambda i, j: (0, j)),
            pl.BlockSpec((tn, d), lambda i, j: (j, 0)),
            pl.BlockSpec((tm, d), lambda i, j: (i, 0), pipeline_mode=pl.Buffered(1)),
            pl.BlockSpec((8, d), lambda i, j: (0, mod_k)),
            pl.BlockSpec((1, d), lambda i, j: (0, 0)),
        ],
        out_specs=pl.BlockSpec((tm, d), lambda i, j: (i, 0)),
        out_shape=jax.ShapeDtypeStruct((m, d), F32),
        scratch_shapes=[pltpu.VMEM((tm, LANES), F32)],
        compiler_params=_params(("parallel", "arbitrary")),
        name="merge",
    )(ya, yb, p16, p16, *w16, x, mod_l, gpost)


def _win_kinds(groups):
    return [kind for kind, width in groups for _ in range(width // WIN_TN)]


def kernel(x, c, ctx, c_ctx, w_mod, b_mod, norm_g, ffn1_w_gu, ffn1_w_down, ffn2_w_gu, ffn2_w_down,
           w_in, chunk_norm_g, w_spatial, b_spatial, lb_logits, hgrn_norm_g, w_up_a, w_up_b, w_out):
    bsz, n, d = x.shape
    n_ctx = ctx.shape[1]
    depth = w_mod.shape[0]
    d_a = chunk_norm_g.shape[1]
    d_b = hgrn_norm_g.shape[1]
    ctx_row = bsz

    cs = jnp.concatenate([c, c_ctx[None, :], jnp.zeros((8 - bsz - 1, d), F32)], axis=0)
    mod = _mod_call(cs, w_mod, b_mod)
    lb2 = lb_logits.reshape(depth, -1)

    xl = x.reshape(bsz * n, d)
    xc = ctx.reshape(bsz * n_ctx, d)
    tm = 1024
    lat = dict(tm=tm, row_div=n // tm, row_const=None)
    cx = dict(tm=bsz * n_ctx, row_div=None, row_const=ctx_row)
    col_q, col_i, col_g = 2 * d_a // 128, (2 * d_a + d_b) // 128, (2 * d_a + 2 * d_b) // 128
    gate_cols = (2 * d_a + 3 * d_b, 2 * d_a + 3 * d_b + d)
    kinds_full = _win_kinds([("gelu", 2 * d_a), ("id", d_b), ("logf", 2 * d_b), ("id", d_b),
                             ("silu", d_b), ("sigmoid", 2 * d)])
    kinds_state = _win_kinds([("logf", 2 * d_b), ("id", d_b)])

    for l in range(depth):
        last = l == depth - 1
        ml = mod[l]
        g = [norm_g[l, k][None, :] for k in range(norm_g.shape[1])]
        ffn1 = functools.partial(_ffn_call, mod_l=ml, mod_k=0, gpre=g[0], gpost=g[1],
                                 w_gu=ffn1_w_gu, w_down=ffn1_w_down, l=l)
        ffn2 = functools.partial(_ffn_call, mod_l=ml, mod_k=6, gpre=g[4], gpost=g[5],
                                 w_gu=ffn2_w_gu, w_down=ffn2_w_down, l=l)

        wm16 = (_cast16(w_up_a, l), _cast16(w_up_b, l), _cast16(w_out, l))

        xc, w16_1 = ffn1(xc, tf=FFN_TF, emit16=True, **cx)
        if not last:
            pc16, pcf = _win_call(xc, ml, 3, g[2], w_in, lb2, l, col0=0, kinds=kinds_full, **cx)
            ybc, s_f, s_b = _hgrn_call(pc16, pcf, hgrn_norm_g[l][None, :], None, n_ctx,
                                       (col_q, col_i, col_g), with_out=True, with_final=True)
            yac = _cmlp_call(pc16, chunk_norm_g[l][None, :], w_spatial[l], b_spatial[l], d_a, tm=512)
            xc = _merge_call(xc, yac, ybc, pc16, gate_cols, ml, 5, g[3], wm16, **cx)
            xc, w16_2 = ffn2(xc, tf=FFN_TF, emit16=True, **cx)
        else:
            pc16, pcf = _win_call(xc, ml, 3, g[2], w_in, lb2, l, col0=(2 * d_a + d_b) // WIN_TN,
                                  kinds=kinds_state, **cx)
            s_f, s_b = _hgrn_call(pc16, pcf, None, None, n_ctx, (0, 0, 0), with_out=False, with_final=True)
            w16_2 = None

        xl = ffn1(xl, tf=FFN_TF16, w16=w16_1, **lat)
        p16, pf = _win_call(xl, ml, 3, g[2], w_in, lb2, l, col0=0, kinds=kinds_full, **lat)
        (yb,) = _hgrn_call(p16, pf, hgrn_norm_g[l][None, :], (s_f, s_b), n,
                           (col_q, col_i, col_g), with_out=True, with_final=False)
        ya = _cmlp_call(p16, chunk_norm_g[l][None, :], w_spatial[l], b_spatial[l], d_a, tm=512)
        xl = _merge_call(xl, ya, yb, p16, gate_cols, ml, 5, g[3], wm16, **lat)
        if w16_2 is None:
            xl = ffn2(xl, tf=FFN_TF, **lat)
        else:
            xl = ffn2(xl, tf=FFN_TF16, w16=w16_2, **lat)
    return xl.reshape(bsz, n, d)
```

```python
import functools

import numpy as np
import jax
import jax.numpy as jnp
from jax import lax
from jax.experimental import pallas as pl
from jax.experimental.pallas import tpu as pltpu

BF = jnp.bfloat16
F32 = jnp.float32

EPS = 1e-6
MACARON = 0.5
F_FLOOR = 1e-30
N_MOD = 9
GROUPS_A = 8
CHUNK_A = 128
HEADS_B = 8
LANES = 128
HGRN_DK = 128
HGRN_CHUNK = 64
HGRN_UNROLL = 14
HGRN_LEVELS = (2, 4, 8, 16, 32, 64)

VMEM_LIMIT = 56 * 1024 * 1024


def _dot(a, b):
    return jnp.dot(a, b, preferred_element_type=F32)


def _dot_nt(a, b):
    return lax.dot_general(a, b, (((1,), (1,)), ((), ())), preferred_element_type=F32)


def _dot_tn(a, b):
    return lax.dot_general(a, b, (((0,), (0,)), ((), ())), preferred_element_type=F32)


def _rms(x, g):
    return x * lax.rsqrt(jnp.mean(x * x, axis=-1, keepdims=True) + EPS) * g


def _params(sem):
    return pltpu.CompilerParams(dimension_semantics=sem, vmem_limit_bytes=VMEM_LIMIT)


def _mod_kernel(cs_ref, w_ref, b_ref, o_ref):
    cs = cs_ref[...]
    s = (cs * jax.nn.sigmoid(cs)).astype(BF)
    o_ref[...] = _dot(s, w_ref[...].astype(BF)) + b_ref[...]


def _mod_call(cs, w_mod, b_mod):
    depth, d, n = w_mod.shape
    tn = 1024
    return pl.pallas_call(
        _mod_kernel,
        grid=(depth, n // tn),
        in_specs=[
            pl.BlockSpec((8, d), lambda l, j: (0, 0)),
            pl.BlockSpec((None, d, tn), lambda l, j: (l, 0, j)),
            pl.BlockSpec((None, 1, tn), lambda l, j: (l, 0, j)),
        ],
        out_specs=pl.BlockSpec((None, 8, tn), lambda l, j: (l, 0, j)),
        out_shape=jax.ShapeDtypeStruct((depth, 8, n), F32),
        compiler_params=_params(("parallel", "parallel")),
        name="mod",
    )(cs, w_mod, b_mod.reshape(depth, 1, n))


def _mod_row(row_div, row_const):
    if row_div is None:
        return row_const
    return pl.program_id(0) // row_div


ROW_CHUNK = 64
FFN_TF = 256
FFN_TF16 = 512


def _row_loop(n_rows, fn, unroll=2):
    def body(r, _):
        fn(pl.ds(pl.multiple_of(r * ROW_CHUNK, ROW_CHUNK), ROW_CHUNK))
        return 0

    lax.fori_loop(0, n_rows // ROW_CHUNK, body, 0, unroll=unroll)


def _row_rsqrt_ms(rs_scr, src_ref):
    def rows(sl):
        v = src_ref[sl, :]
        rs = lax.rsqrt(jnp.mean(v * v, axis=-1, keepdims=True) + EPS)
        rs_scr[sl, :] = jnp.broadcast_to(rs, (ROW_CHUNK, rs_scr.shape[1]))

    _row_loop(src_ref.shape[0], rows, unroll=4)


def _lane_tile(v, width):
    return jnp.concatenate([v] * (width // v.shape[1]), axis=1)


def _modulate_into(h_scr, x_ref, gpre_ref, sc_ref, sh_ref, row):
    gain = gpre_ref[...] * (1.0 + sc_ref[pl.ds(row, 1), :])
    shift = sh_ref[pl.ds(row, 1), :]

    def rows(sl):
        x = x_ref[sl, :]
        rs = lax.rsqrt(jnp.mean(x * x, axis=-1, keepdims=True) + EPS)
        h_scr[sl, :] = (x * rs * gain + shift).astype(BF)

    _row_loop(x_ref.shape[0], rows)


def _residual_into(o_ref, rs_scr, x_ref, gpost_ref, gt_ref, row, weight):
    gain = gpost_ref[...] * gt_ref[pl.ds(row, 1), :] * weight
    d = o_ref.shape[1]
    _row_rsqrt_ms(rs_scr, o_ref)

    def rows(sl):
        o_ref[sl, :] = x_ref[sl, :] + o_ref[sl, :] * _lane_tile(rs_scr[sl, :], d) * gain

    _row_loop(o_ref.shape[0], rows)


def _ffn_kernel(x_ref, sh_ref, sc_ref, gt_ref, gpre_ref, gpost_ref, wg_ref, wu_ref, wd_ref, *rest,
                row_div, row_const, n_f, emit16):
    if emit16:
        o_ref, wg16_ref, wu16_ref, wd16_ref, h_scr, rs_scr = rest
    else:
        o_ref, h_scr, rs_scr = rest
    j = pl.program_id(1)
    row = _mod_row(row_div, row_const)

    @pl.when(j == 0)
    def _():
        _modulate_into(h_scr, x_ref, gpre_ref, sc_ref, sh_ref, row)
        o_ref[...] = jnp.zeros_like(o_ref)

    wg, wu, wd = wg_ref[...].astype(BF), wu_ref[...].astype(BF), wd_ref[...].astype(BF)
    if emit16:
        wg16_ref[...] = wg
        wu16_ref[...] = wu
        wd16_ref[...] = wd
    h = h_scr[...]
    a = _dot(h, wg)
    b = _dot(h, wu)
    act = (a * jax.nn.sigmoid(a) * b).astype(BF)
    o_ref[...] += _dot(act, wd)

    @pl.when(j == n_f - 1)
    def _():
        _residual_into(o_ref, rs_scr, x_ref, gpost_ref, gt_ref, row, MACARON)


def _ffn_call(x, mod_l, mod_k, gpre, gpost, w_gu, w_down, l, *, tm, tf, row_div, row_const,
              w16=None, emit16=False):
    m, d = x.shape
    d_ff = w_down.shape[1]
    n_f = d_ff // tf
    kern = functools.partial(_ffn_kernel, row_div=row_div, row_const=row_const, n_f=n_f, emit16=emit16)
    mod_spec = lambda k: pl.BlockSpec((8, d), lambda i, j: (0, k))
    col_tile = pl.BlockSpec((d, tf), lambda i, j: (0, j))
    row_tile = pl.BlockSpec((tf, d), lambda i, j: (j, 0))
    if w16 is None:
        weights = (w_gu, w_gu, w_down)
        w_specs = [pl.BlockSpec((None, d, tf), lambda i, j: (l, 0, j)),
                   pl.BlockSpec((None, d, tf), lambda i, j: (l, 0, n_f + j)),
                   pl.BlockSpec((None, tf, d), lambda i, j: (l, j, 0))]
    else:
        weights = w16
        w_specs = [col_tile, col_tile, row_tile]
    out_specs = [pl.BlockSpec((tm, d), lambda i, j: (i, 0))]
    out_shape = [jax.ShapeDtypeStruct((m, d), F32)]
    if emit16:
        assert m == tm
        out_specs += [col_tile, col_tile, row_tile]
        out_shape += [jax.ShapeDtypeStruct((d, d_ff), BF), jax.ShapeDtypeStruct((d, d_ff), BF),
                      jax.ShapeDtypeStruct((d_ff, d), BF)]
    out = pl.pallas_call(
        kern,
        grid=(m // tm, n_f),
        in_specs=[
            pl.BlockSpec((tm, d), lambda i, j: (i, 0), pipeline_mode=None if w16 else pl.Buffered(1)),
            mod_spec(mod_k), mod_spec(mod_k + 1), mod_spec(mod_k + 2),
            pl.BlockSpec((1, d), lambda i, j: (0, 0)),
            pl.BlockSpec((1, d), lambda i, j: (0, 0)),
        ] + w_specs,
        out_specs=out_specs,
        out_shape=out_shape,
        scratch_shapes=[pltpu.VMEM((tm, d), BF), pltpu.VMEM((tm, LANES), F32)],
        compiler_params=_params(("parallel", "arbitrary")),
        name="ffn",
    )(x, mod_l, mod_l, mod_l, gpre, gpost, *weights)
    return (out[0], tuple(out[1:])) if emit16 else out[0]


WIN_TN = 1024
WIN_SLABS = 4
MERGE_TN = 512
MERGE_SLABS = 2


def _gelu_tanh(z):
    return z * jax.nn.sigmoid((2.0 * 0.7978845608028654) * (z + 0.044715 * (z * z * z)))


def _win_kernel(x_ref, sh_ref, sc_ref, gpre_ref, w_ref, lb_ref, p16_ref, pf_ref, h_scr,
                *, row_div, row_const, kinds, layer):
    j = pl.program_id(1)
    row = _mod_row(row_div, row_const)

    @pl.when(j == 0)
    def _():
        _modulate_into(h_scr, x_ref, gpre_ref, sc_ref, sh_ref, row)

    def in_kind(kind):
        cond = jnp.bool_(False)
        t = 0
        while t < len(kinds):
            t1 = t
            while t1 + 1 < len(kinds) and kinds[t1 + 1] == kinds[t]:
                t1 += 1
            if kinds[t] == kind:
                cond = cond | ((j >= t) & (j <= t1))
            t = t1 + 1
        return cond

    def slabs(out_ref, fn):
        w = w_ref[...].astype(BF)
        tm = h_scr.shape[0]
        for r0 in range(0, tm, tm // WIN_SLABS):
            sl = slice(r0, r0 + tm // WIN_SLABS)
            out_ref[sl, :] = fn(_dot(h_scr[sl, :], w)).astype(out_ref.dtype)

    def logf(z):
        ll = lb_ref[...]
        e = jnp.exp(ll - jnp.max(ll, axis=0, keepdims=True))
        p = e / jnp.sum(e, axis=0, keepdims=True)
        lb = jnp.zeros_like(p[0:1])
        for r in range(1, layer + 1):
            lb = lb + p[r:r + 1]
        return jnp.log2(jnp.maximum(lb + (1.0 - lb) * jax.nn.sigmoid(z), F_FLOOR))

    for kind, out_ref, fn in (("gelu", p16_ref, _gelu_tanh), ("id", p16_ref, lambda z: z),
                              ("silu", p16_ref, lambda z: z * jax.nn.sigmoid(z)),
                              ("sigmoid", p16_ref, jax.nn.sigmoid), ("logf", pf_ref, logf)):
        if kind in kinds:
            pl.when(in_kind(kind))(functools.partial(slabs, out_ref, fn))


def _win_call(x, mod_l, mod_k, gpre, w_in, lb_logits2, l, *, tm, row_div, row_const, col0, kinds):
    m, d = x.shape
    tn = WIN_TN
    nj = len(kinds)
    is_f = np.array([k == "logf" for k in kinds])
    n16, nf = int((~is_f).sum()), int(is_f.sum())
    c16 = np.maximum(np.cumsum(~is_f) - 1, 0)
    cf = np.maximum(np.cumsum(is_f) - 1, 0)
    f0 = int(np.argmax(is_f))

    def sel(table):
        def f(j):
            out = jnp.int32(int(table[0]))
            for t in range(1, nj):
                if table[t] != table[t - 1]:
                    out = jnp.where(j >= t, jnp.int32(int(table[t])), out)
            return out
        return f

    s16, sf = sel(c16), sel(cf)
    kern = functools.partial(_win_kernel, row_div=row_div, row_const=row_const, kinds=tuple(kinds), layer=l)
    mod_spec = lambda k: pl.BlockSpec((8, d), lambda i, j: (0, k))
    depth = lb_logits2.shape[0]
    return pl.pallas_call(
        kern,
        grid=(m // tm, nj),
        in_specs=[
            pl.BlockSpec((tm, d), lambda i, j: (i, 0)),
            mod_spec(mod_k), mod_spec(mod_k + 1),
            pl.BlockSpec((1, d), lambda i, j: (0, 0)),
            pl.BlockSpec((None, d, tn), lambda i, j: (l, 0, col0 + j)),
            pl.BlockSpec((depth, tn), lambda i, j: (0, jnp.clip(j - f0, 0, nf - 1))),
        ],
        out_specs=[
            pl.BlockSpec((tm, tn), lambda i, j: (i, s16(j))),
            pl.BlockSpec((tm, tn), lambda i, j: (i, sf(j))),
        ],
        out_shape=[jax.ShapeDtypeStruct((m, n16 * tn), BF), jax.ShapeDtypeStruct((m, nf * tn), F32)],
        scratch_shapes=[pltpu.VMEM((tm, d), BF)],
        compiler_params=_params(("parallel", "arbitrary")),
        name="win",
    )(x, mod_l, mod_l, gpre, w_in, lb_logits2)


def _cmlp_kernel(u_ref, v_ref, g_ref, ws_ref, bs_ref, o_ref, *, n_chunks):
    v = v_ref[...].astype(F32)
    vc = v - jnp.mean(v, axis=-1, keepdims=True)
    vn = (vc * lax.rsqrt(jnp.mean(vc * vc, axis=-1, keepdims=True) + EPS) * g_ref[...]).astype(BF)
    dg = vn.shape[1] // GROUPS_A
    for g in range(GROUPS_A):
        w = ws_ref[g].astype(BF)
        bias = bs_ref[g]
        for c in range(n_chunks):
            rows = slice(c * CHUNK_A, (c + 1) * CHUNK_A)
            cols = slice(g * dg, (g + 1) * dg)
            sv = _dot(w, vn[rows, cols]) + bias
            o_ref[rows, cols] = (u_ref[rows, cols].astype(F32) * sv).astype(BF)


def _cmlp_call(p16, chunk_g, w_s, b_s, d_a, *, tm):
    m = p16.shape[0]
    n_chunks = tm // CHUNK_A
    return pl.pallas_call(
        functools.partial(_cmlp_kernel, n_chunks=n_chunks),
        grid=(m // tm,),
        in_specs=[
            pl.BlockSpec((tm, d_a), lambda i: (i, 0)),
            pl.BlockSpec((tm, d_a), lambda i: (i, 1)),
            pl.BlockSpec((1, d_a), lambda i: (0, 0)),
            pl.BlockSpec((GROUPS_A, CHUNK_A, CHUNK_A), lambda i: (0, 0, 0)),
            pl.BlockSpec((GROUPS_A, CHUNK_A, 1), lambda i: (0, 0, 0)),
        ],
        out_specs=pl.BlockSpec((tm, d_a), lambda i: (i, 0)),
        out_shape=jax.ShapeDtypeStruct((m, d_a), BF),
        compiler_params=_params(("parallel",)),
        name="cmlp",
    )(p16, p16, chunk_g, w_s, b_s[..., None])


def _hgrn_consts(fwd):
    c = HGRN_CHUNK
    t = np.arange(c)[:, None]
    u = np.arange(c)[None, :]

    def level_sum(b):
        half = b // 2
        mid = (t // b) * b + half
        upper = (t % b) >= half
        if fwd:
            return np.where(upper, (u >= mid) & (u <= t), (u > t) & (u < mid))
        return np.where(upper, (u >= mid) & (u < t), (u >= t) & (u < mid))

    def level_mask(b):
        half = b // 2
        upper = (t % b) >= half
        same = (t // b) == (u // b)
        if fwd:
            return same & upper & ((u % b) < half)
        return same & ~upper & ((u % b) >= half)

    lsum = np.concatenate([(u <= t) if fwd else (u >= t), level_sum(4), level_sum(8)], 0)
    masks = [t == u] + [level_mask(b) for b in HGRN_LEVELS] + [np.zeros((c, c), bool)]
    return (np.concatenate([lsum, lsum], 1).astype(np.float32),
            np.concatenate(masks, 1).astype(np.float32))


def _to_midpoint(cum, b, fwd):
    c, dk = cum.shape
    half = b // 2
    pieces = []
    for m in range(c // b):
        lo, mid = m * b, m * b + half
        ref = jnp.broadcast_to(cum[mid - 1:mid, :] if fwd else cum[mid:mid + 1, :], (half, dk))
        lower, upper = cum[lo:mid], cum[mid:lo + b]
        pieces += [ref - lower, upper - ref] if fwd else [lower - ref, ref - upper]
    return jnp.concatenate(pieces, axis=0)


def _hgrn_stage1(g, l2_ref):
    g_hi = g.astype(BF)
    g_lo = (g - g_hi.astype(F32)).astype(BF)
    return _dot(l2_ref[...], jnp.concatenate([g_hi, g_lo], axis=0))


def _hgrn_stage2(q, g, sums, mask_ref, fwd, with_out):
    c = HGRN_CHUNK
    dk = g.shape[1]
    f = jnp.exp2(g)
    k = 1.0 - f
    cum = sums[0:c]
    e_cum = jnp.exp2(cum)
    total = cum[c - 1:c] if fwd else cum[0:1]
    kend = (k * jnp.exp2(total - cum)).astype(BF)
    tot = e_cum[c - 1:c] if fwd else e_cum[0:1]
    if not with_out:
        return None, None, kend, tot

    row = lax.broadcasted_iota(jnp.int32, (c, dk), 0)
    e2 = jnp.where((row % 2) == (1 if fwd else 0), f, 1.0)
    es = [e2, jnp.exp2(sums[c:2 * c]), jnp.exp2(sums[2 * c:3 * c])]
    es += [jnp.exp2(_to_midpoint(cum, b, fwd)) for b in HGRN_LEVELS[3:]]
    qf = q.astype(F32)
    units = [(q, k.astype(BF))] + [((qf * e).astype(BF), (k * e).astype(BF)) for e in es]
    z = jnp.zeros((c, dk), BF)
    res = []
    for a in range(0, len(units) - 1, 2):
        (qa, ka), (qb, kb) = units[a], units[a + 1]
        kbd = jnp.concatenate([jnp.concatenate([ka, z], axis=1), jnp.concatenate([z, kb], axis=1)], axis=0)
        res.append(_dot_nt(jnp.concatenate([qa, qb], axis=1), kbd))
    q6, k6 = units[-1]
    res.append(_dot_nt(q6, jnp.concatenate([k6, z], axis=0)))
    p = sum(r * mask_ref[:, i * 2 * c:(i + 1) * 2 * c] for i, r in enumerate(res)).astype(BF)
    return p, (qf * e_cum).astype(BF), kend, tot


def _hgrn_stage3(p, qe, kend, tot, v, st, with_out):
    st_new = st * tot + _dot_tn(v, kend)
    if not with_out:
        return None, st_new
    o = _dot(p, jnp.concatenate([v, v], axis=0)) + _dot_nt(qe, st.astype(BF))
    return o, st_new


def _hgrn_kernel(*refs, n, with_out, with_init, with_final):
    it = iter(refs)
    q_ref = next(it) if with_out else None
    v_ref = next(it)
    gs_ref = next(it) if with_out else None
    gf_ref, gb_ref = next(it), next(it)
    gain_ref = next(it) if with_out else None
    sf0_ref, sb0_ref = (next(it), next(it)) if with_init else (None, None)
    l2_refs = (next(it), next(it))
    mask_refs = (next(it), next(it))
    y_ref = next(it) if with_out else None
    sf_ref, sb_ref = (next(it), next(it)) if with_final else (None, None)
    sums_scr, kend_scr, tot_scr, st_scr = next(it), next(it), next(it), next(it)
    p_scr, qe_scr, o_scr = (next(it), next(it), next(it)) if with_out else (None, None, None)

    c = HGRN_CHUNK
    nc = n // c
    dk = v_ref.shape[1]
    g_refs = (gf_ref, gb_ref)

    st_scr[0] = sf0_ref[...] if with_init else jnp.zeros((dk, dk), F32)
    st_scr[1] = sb0_ref[...] if with_init else jnp.zeros((dk, dk), F32)

    def rows_of(d, ci):
        start = ci * c if d == 0 else (nc - 1 - ci) * c
        return pl.ds(start if isinstance(start, int) else pl.multiple_of(start, c), c)

    def step(i, par, do1, do2, do3):
        for d in (0, 1):
            if do3:
                r = rows_of(d, i - 2)
                o, st = _hgrn_stage3(p_scr[d, par] if with_out else None,
                                     qe_scr[d, par] if with_out else None,
                                     kend_scr[d, par], tot_scr[d, par, 0:1, :],
                                     v_ref[r, :], st_scr[d], with_out)
                st_scr[d] = st
                if with_out:
                    o_scr[d, r, :] = o
            if do2:
                r = rows_of(d, i - 1)
                p, qe, kend, tot = _hgrn_stage2(q_ref[r, :] if with_out else None, g_refs[d][r, :],
                                                sums_scr[d, 1 - par], mask_refs[d], d == 0, with_out)
                kend_scr[d, 1 - par] = kend
                tot_scr[d, 1 - par, 0:1, :] = tot
                if with_out:
                    p_scr[d, 1 - par] = p
                    qe_scr[d, 1 - par] = qe
            if do1:
                sums_scr[d, par] = _hgrn_stage1(g_refs[d][rows_of(d, i), :], l2_refs[d])

    step(0, 0, True, False, False)
    step(1, 1, True, True, False)

    n_loop = (nc - 2) // HGRN_UNROLL

    def body(ii, _):
        for u in range(HGRN_UNROLL):
            step(2 + HGRN_UNROLL * ii + u, u % 2, True, True, True)
        return 0

    lax.fori_loop(0, n_loop, body, 0)
    for i in range(2 + HGRN_UNROLL * n_loop, nc):
        step(i, i % 2, True, True, True)
    step(nc, nc % 2, False, True, True)
    step(nc + 1, (nc + 1) % 2, False, False, True)

    if with_final:
        sf_ref[...] = st_scr[0]
        sb_ref[...] = st_scr[1]

    if with_out:
        rows = 256

        def readout(ri, _):
            r = pl.ds(pl.multiple_of(ri * rows, rows), rows)
            o = o_scr[0, r, :] + o_scr[1, r, :]
            o = o * lax.rsqrt(jnp.mean(o * o, axis=-1, keepdims=True) + EPS) * gain_ref[...]
            y_ref[r, :] = (o * gs_ref[r, :].astype(F32)).astype(BF)
            return 0

        lax.fori_loop(0, n // rows, readout, 0, unroll=2)


def _hgrn_call(p16, pf, gain, init, n, cols, *, with_out, with_final):
    m = p16.shape[0]
    bsz = m // n
    dk = HGRN_DK
    with_init = init is not None
    cq, cv, cg = cols
    tok = lambda col: pl.BlockSpec((n, dk), lambda b, h: (b, col + h))
    st_spec = pl.BlockSpec((None, None, dk, dk), lambda b, h: (b, h, 0, 0))
    full = lambda a: pl.BlockSpec(a.shape, lambda b, h: (0,) * a.ndim)

    assert (n // HGRN_CHUNK) % 2 == 0 and n % HGRN_CHUNK == 0
    lf, mf = _hgrn_consts(True)
    lbw, mb = _hgrn_consts(False)
    consts = [jnp.asarray(lf, BF), jnp.asarray(lbw, BF), jnp.asarray(mf, F32), jnp.asarray(mb, F32)]

    args, specs = [], []
    if with_out:
        args.append(p16); specs.append(tok(cq))
    args.append(p16); specs.append(tok(cv))
    if with_out:
        args.append(p16); specs.append(tok(cg))
    args += [pf, pf]; specs += [tok(0), tok(HEADS_B)]
    if with_out:
        args.append(gain); specs.append(pl.BlockSpec((1, dk), lambda b, h: (0, h)))
    if with_init:
        args += list(init); specs += [st_spec, st_spec]
    args += consts; specs += [full(a) for a in consts]

    out_shape, out_specs = [], []
    if with_out:
        out_shape.append(jax.ShapeDtypeStruct((m, HEADS_B * dk), BF)); out_specs.append(tok(0))
    if with_final:
        st_shape = jax.ShapeDtypeStruct((bsz, HEADS_B, dk, dk), F32)
        out_shape += [st_shape, st_shape]; out_specs += [st_spec, st_spec]
    c = HGRN_CHUNK
    scratch = [pltpu.VMEM((2, 2, lf.shape[0], dk), F32), pltpu.VMEM((2, 2, c, dk), BF),
               pltpu.VMEM((2, 2, 8, dk), F32), pltpu.VMEM((2, dk, dk), F32)]
    if with_out:
        scratch += [pltpu.VMEM((2, 2, c, 2 * c), BF), pltpu.VMEM((2, 2, c, dk), BF),
                    pltpu.VMEM((2, n, dk), F32)]

    kern = functools.partial(_hgrn_kernel, n=n, with_out=with_out, with_init=with_init, with_final=with_final)
    return pl.pallas_call(
        kern,
        grid=(bsz, HEADS_B),
        in_specs=specs,
        out_specs=out_specs,
        out_shape=out_shape,
        scratch_shapes=scratch,
        compiler_params=_params(("parallel", "parallel")),
        name="hgrn",
    )(*args)


def _cast16_kernel(w_ref, o_ref):
    o_ref[...] = w_ref[...].astype(BF)


def _cast16(w, l, tr=256):
    _, r, c = w.shape
    return pl.pallas_call(
        _cast16_kernel,
        grid=(r // tr,),
        in_specs=[pl.BlockSpec((None, tr, c), lambda i: (l, i, 0))],
        out_specs=pl.BlockSpec((tr, c), lambda i: (i, 0)),
        out_shape=jax.ShapeDtypeStruct((r, c), BF),
        compiler_params=_params(("parallel",)),
        name="cast16",
    )(w)


def _merge_kernel(ya_ref, yb_ref, ga_ref, gb_ref, wa_ref, wb_ref, wo_ref, x_ref, gt_ref, gpost_ref,
                  o_ref, rs_scr, *, row_div, row_const, n_j):
    j = pl.program_id(1)
    row = _mod_row(row_div, row_const)

    @pl.when(j == 0)
    def _():
        o_ref[...] = jnp.zeros_like(o_ref)

    wa, wb, wo = wa_ref[...], wb_ref[...], wo_ref[...]
    tm = o_ref.shape[0]
    for r0 in range(0, tm, tm // MERGE_SLABS):
        sl = slice(r0, r0 + tm // MERGE_SLABS)
        ma = _dot(ya_ref[sl, :], wa)
        mb = _dot(yb_ref[sl, :], wb)
        mm = (ga_ref[sl, :].astype(F32) * ma + gb_ref[sl, :].astype(F32) * mb).astype(BF)
        o_ref[sl, :] += _dot(mm, wo)

    @pl.when(j == n_j - 1)
    def _():
        _residual_into(o_ref, rs_scr, x_ref, gpost_ref, gt_ref, row, 1.0)


def _merge_call(x, ya, yb, p16, gate_cols, mod_l, mod_k, gpost, w16, *, tm, row_div, row_const):
    m, d = x.shape
    da, db = ya.shape[1], yb.shape[1]
    tn = MERGE_TN
    n_j = d // tn
    ca, cb = (c // tn for c in gate_cols)
    kern = functools.partial(_merge_kernel, row_div=row_div, row_const=row_const, n_j=n_j)
    return pl.pallas_call(
        kern,
        grid=(m // tm, n_j),
        in_specs=[
            pl.BlockSpec((tm, da), lambda i, j: (i, 0)),
            pl.BlockSpec((tm, db), lambda i, j: (i, 0)),
            pl.BlockSpec((tm, tn), lambda i, j: (i, ca + j)),
            pl.BlockSpec((tm, tn), lambda i, j: (i, cb + j)),
            pl.BlockSpec((da, tn), lambda i, j: (0, j)),
            pl.BlockSpec((db, tn), lambda i, j: (0, j)),
            pl.BlockSpec((tn, d), lambda i, j: (j, 0)),
            pl.BlockSpec((tm, d), lambda i, j: (i, 0)),
            pl.BlockSpec((8, d), lambda i, j: (0, mod_k)),
            pl.BlockSpec((1, d), lambda i, j: (0, 0)),
        ],
        out_specs=pl.BlockSpec((tm, d), lambda i, j: (i, 0)),
        out_shape=jax.ShapeDtypeStruct((m, d), F32),
        scratch_shapes=[pltpu.VMEM((tm, LANES), F32)],
        compiler_params=_params(("parallel", "arbitrary")),
        name="merge",
    )(ya, yb, p16, p16, *w16, x, mod_l, gpost)


def _win_kinds(groups):
    return [kind for kind, width in groups for _ in range(width // WIN_TN)]


def kernel(x, c, ctx, c_ctx, w_mod, b_mod, norm_g, ffn1_w_gu, ffn1_w_down, ffn2_w_gu, ffn2_w_down,
           w_in, chunk_norm_g, w_spatial, b_spatial, lb_logits, hgrn_norm_g, w_up_a, w_up_b, w_out):
    bsz, n, d = x.shape
    n_ctx = ctx.shape[1]
    depth = w_mod.shape[0]
    d_a = chunk_norm_g.shape[1]
    d_b = hgrn_norm_g.shape[1]
    ctx_row = bsz

    cs = jnp.concatenate([c, c_ctx[None, :], jnp.zeros((8 - bsz - 1, d), F32)], axis=0)
    mod = _mod_call(cs, w_mod, b_mod)
    lb2 = lb_logits.reshape(depth, -1)

    xl = x.reshape(bsz * n, d)
    xc = ctx.reshape(bsz * n_ctx, d)
    tm = 1024
    lat = dict(tm=tm, row_div=n // tm, row_const=None)
    cx = dict(tm=bsz * n_ctx, row_div=None, row_const=ctx_row)
    col_q, col_i, col_g = 2 * d_a // 128, (2 * d_a + d_b) // 128, (2 * d_a + 2 * d_b) // 128
    gate_cols = (2 * d_a + 3 * d_b, 2 * d_a + 3 * d_b + d)
    kinds_full = _win_kinds([("gelu", 2 * d_a), ("id", d_b), ("logf", 2 * d_b), ("id", d_b),
                             ("silu", d_b), ("sigmoid", 2 * d)])
    kinds_state = _win_kinds([("logf", 2 * d_b), ("id", d_b)])

    for l in range(depth):
        last = l == depth - 1
        ml = mod[l]
        g = [norm_g[l, k][None, :] for k in range(norm_g.shape[1])]
        ffn1 = functools.partial(_ffn_call, mod_l=ml, mod_k=0, gpre=g[0], gpost=g[1],
                                 w_gu=ffn1_w_gu, w_down=ffn1_w_down, l=l)
        ffn2 = functools.partial(_ffn_call, mod_l=ml, mod_k=6, gpre=g[4], gpost=g[5],
                                 w_gu=ffn2_w_gu, w_down=ffn2_w_down, l=l)

        wm16 = (_cast16(w_up_a, l), _cast16(w_up_b, l), _cast16(w_out, l))

        xc, w16_1 = ffn1(xc, tf=FFN_TF, emit16=True, **cx)
        if not last:
            pc16, pcf = _win_call(xc, ml, 3, g[2], w_in, lb2, l, col0=0, kinds=kinds_full, **cx)
            ybc, s_f, s_b = _hgrn_call(pc16, pcf, hgrn_norm_g[l][None, :], None, n_ctx,
                                       (col_q, col_i, col_g), with_out=True, with_final=True)
            yac = _cmlp_call(pc16, chunk_norm_g[l][None, :], w_spatial[l], b_spatial[l], d_a, tm=512)
            xc = _merge_call(xc, yac, ybc, pc16, gate_cols, ml, 5, g[3], wm16, **cx)
            xc, w16_2 = ffn2(xc, tf=FFN_TF, emit16=True, **cx)
        else:
            pc16, pcf = _win_call(xc, ml, 3, g[2], w_in, lb2, l, col0=(2 * d_a + d_b) // WIN_TN,
                                  kinds=kinds_state, **cx)
            s_f, s_b = _hgrn_call(pc16, pcf, None, None, n_ctx, (0, 0, 0), with_out=False, with_final=True)
            w16_2 = None

        xl = ffn1(xl, tf=FFN_TF16, w16=w16_1, **lat)
        p16, pf = _win_call(xl, ml, 3, g[2], w_in, lb2, l, col0=0, kinds=kinds_full, **lat)
        (yb,) = _hgrn_call(p16, pf, hgrn_norm_g[l][None, :], (s_f, s_b), n,
                           (col_q, col_i, col_g), with_out=True, with_final=False)
        ya = _cmlp_call(p16, chunk_norm_g[l][None, :], w_spatial[l], b_spatial[l], d_a, tm=512)
        xl = _merge_call(xl, ya, yb, p16, gate_cols, ml, 5, g[3], wm16, **lat)
        if w16_2 is None:
            xl = ffn2(xl, tf=FFN_TF, **lat)
        else:
            xl = ffn2(xl, tf=FFN_TF16, w16=w16_2, **lat)
    return xl.reshape(bsz, n, d)
```

```python
import functools

import numpy as np
import jax
import jax.numpy as jnp
from jax import lax
from jax.experimental import pallas as pl
from jax.experimental.pallas import tpu as pltpu

BF = jnp.bfloat16
F32 = jnp.float32

EPS = 1e-6
MACARON = 0.5
F_FLOOR = 1e-30
N_MOD = 9
GROUPS_A = 8
CHUNK_A = 128
HEADS_B = 8
LANES = 128
HGRN_DK = 128
HGRN_CHUNK = 64
HGRN_UNROLL = 14
HGRN_LEVELS = (2, 4, 8, 16, 32, 64)

VMEM_LIMIT = 60 * 1024 * 1024


def _dot(a, b):
    return jnp.dot(a, b, preferred_element_type=F32)


def _dot_nt(a, b):
    return lax.dot_general(a, b, (((1,), (1,)), ((), ())), preferred_element_type=F32)


def _dot_tn(a, b):
    return lax.dot_general(a, b, (((0,), (0,)), ((), ())), preferred_element_type=F32)


def _rms(x, g):
    return x * lax.rsqrt(jnp.mean(x * x, axis=-1, keepdims=True) + EPS) * g


def _params(sem):
    return pltpu.CompilerParams(dimension_semantics=sem, vmem_limit_bytes=VMEM_LIMIT)


def _mod_kernel(cs_ref, w_ref, b_ref, o_ref):
    cs = cs_ref[...]
    s = (cs * jax.nn.sigmoid(cs)).astype(BF)
    o_ref[...] = _dot(s, w_ref[...].astype(BF)) + b_ref[...]


def _mod_call(cs, w_mod, b_mod):
    depth, d, n = w_mod.shape
    tn = 1024
    return pl.pallas_call(
        _mod_kernel,
        grid=(depth, n // tn),
        in_specs=[
            pl.BlockSpec((8, d), lambda l, j: (0, 0)),
            pl.BlockSpec((None, d, tn), lambda l, j: (l, 0, j)),
            pl.BlockSpec((None, 1, tn), lambda l, j: (l, 0, j)),
        ],
        out_specs=pl.BlockSpec((None, 8, tn), lambda l, j: (l, 0, j)),
        out_shape=jax.ShapeDtypeStruct((depth, 8, n), F32),
        compiler_params=_params(("parallel", "parallel")),
        name="mod",
    )(cs, w_mod, b_mod.reshape(depth, 1, n))


def _mod_row(row_div, row_const):
    if row_div is None:
        return row_const
    return pl.program_id(0) // row_div


ROW_CHUNK = 64
FFN_TF = 256
FFN_TF16 = 512


def _row_loop(n_rows, fn, unroll=2):
    def body(r, _):
        fn(pl.ds(pl.multiple_of(r * ROW_CHUNK, ROW_CHUNK), ROW_CHUNK))
        return 0

    lax.fori_loop(0, n_rows // ROW_CHUNK, body, 0, unroll=unroll)


def _row_rsqrt_ms(rs_scr, src_ref):
    def rows(sl):
        v = src_ref[sl, :]
        rs = lax.rsqrt(jnp.mean(v * v, axis=-1, keepdims=True) + EPS)
        rs_scr[sl, :] = jnp.broadcast_to(rs, (ROW_CHUNK, rs_scr.shape[1]))

    _row_loop(src_ref.shape[0], rows, unroll=4)


def _lane_tile(v, width):
    return jnp.concatenate([v] * (width // v.shape[1]), axis=1)


def _modulate_into(h_scr, x_ref, gpre_ref, sc_ref, sh_ref, row):
    gain = gpre_ref[...] * (1.0 + sc_ref[pl.ds(row, 1), :])
    shift = sh_ref[pl.ds(row, 1), :]

    def rows(sl):
        x = x_ref[sl, :]
        rs = lax.rsqrt(jnp.mean(x * x, axis=-1, keepdims=True) + EPS)
        h_scr[sl, :] = (x * rs * gain + shift).astype(BF)

    _row_loop(x_ref.shape[0], rows)


def _residual_into(o_ref, rs_scr, x_ref, gpost_ref, gt_ref, row, weight):
    gain = gpost_ref[...] * gt_ref[pl.ds(row, 1), :] * weight
    d = o_ref.shape[1]
    _row_rsqrt_ms(rs_scr, o_ref)

    def rows(sl):
        o_ref[sl, :] = x_ref[sl, :] + o_ref[sl, :] * _lane_tile(rs_scr[sl, :], d) * gain

    _row_loop(o_ref.shape[0], rows)


def _ffn_kernel(x_ref, sh_ref, sc_ref, gt_ref, gpre_ref, gpost_ref, wg_ref, wu_ref, wd_ref, *rest,
                row_div, row_const, n_f, emit16):
    if emit16:
        o_ref, wg16_ref, wu16_ref, wd16_ref, h_scr, rs_scr = rest
    else:
        o_ref, h_scr, rs_scr = rest
    j = pl.program_id(1)
    row = _mod_row(row_div, row_const)

    @pl.when(j == 0)
    def _():
        _modulate_into(h_scr, x_ref, gpre_ref, sc_ref, sh_ref, row)
        o_ref[...] = jnp.zeros_like(o_ref)

    wg, wu, wd = wg_ref[...].astype(BF), wu_ref[...].astype(BF), wd_ref[...].astype(BF)
    if emit16:
        wg16_ref[...] = wg
        wu16_ref[...] = wu
        wd16_ref[...] = wd
    h = h_scr[...]
    a = _dot(h, wg)
    b = _dot(h, wu)
    act = (a * jax.nn.sigmoid(a) * b).astype(BF)
    o_ref[...] += _dot(act, wd)

    @pl.when(j == n_f - 1)
    def _():
        _residual_into(o_ref, rs_scr, x_ref, gpost_ref, gt_ref, row, MACARON)


def _ffn_call(x, mod_l, mod_k, gpre, gpost, w_gu, w_down, l, *, tm, tf, row_div, row_const,
              w16=None, emit16=False):
    m, d = x.shape
    d_ff = w_down.shape[1]
    n_f = d_ff // tf
    kern = functools.partial(_ffn_kernel, row_div=row_div, row_const=row_const, n_f=n_f, emit16=emit16)
    mod_spec = lambda k: pl.BlockSpec((8, d), lambda i, j: (0, k))
    col_tile = pl.BlockSpec((d, tf), lambda i, j: (0, j))
    row_tile = pl.BlockSpec((tf, d), lambda i, j: (j, 0))
    if w16 is None:
        weights = (w_gu, w_gu, w_down)
        w_specs = [pl.BlockSpec((None, d, tf), lambda i, j: (l, 0, j)),
                   pl.BlockSpec((None, d, tf), lambda i, j: (l, 0, n_f + j)),
                   pl.BlockSpec((None, tf, d), lambda i, j: (l, j, 0))]
    elif len(w16) == 2:
        weights = (w16[0], w16[0], w16[1])
        w_specs = [col_tile, pl.BlockSpec((d, tf), lambda i, j: (0, n_f + j)), row_tile]
    else:
        weights = w16
        w_specs = [col_tile, col_tile, row_tile]
    out_specs = [pl.BlockSpec((tm, d), lambda i, j: (i, 0))]
    out_shape = [jax.ShapeDtypeStruct((m, d), F32)]
    if emit16:
        assert m == tm
        out_specs += [col_tile, col_tile, row_tile]
        out_shape += [jax.ShapeDtypeStruct((d, d_ff), BF), jax.ShapeDtypeStruct((d, d_ff), BF),
                      jax.ShapeDtypeStruct((d_ff, d), BF)]
    out = pl.pallas_call(
        kern,
        grid=(m // tm, n_f),
        in_specs=[
            pl.BlockSpec((tm, d), lambda i, j: (i, 0)),
            mod_spec(mod_k), mod_spec(mod_k + 1), mod_spec(mod_k + 2),
            pl.BlockSpec((1, d), lambda i, j: (0, 0)),
            pl.BlockSpec((1, d), lambda i, j: (0, 0)),
        ] + w_specs,
        out_specs=out_specs,
        out_shape=out_shape,
        scratch_shapes=[pltpu.VMEM((tm, d), BF), pltpu.VMEM((tm, LANES), F32)],
        compiler_params=_params(("parallel", "arbitrary")),
        name="ffn",
    )(x, mod_l, mod_l, mod_l, gpre, gpost, *weights)
    return (out[0], tuple(out[1:])) if emit16 else out[0]


WIN_TN = 1024
WIN_SLABS = 4
MERGE_TN = 512
MERGE_SLABS = 2


def _gelu_tanh(z):
    return z * jax.nn.sigmoid((2.0 * 0.7978845608028654) * (z + 0.044715 * (z * z * z)))


def _win_kernel(x_ref, sh_ref, sc_ref, gpre_ref, w_ref, lb_ref, p16_ref, pf_ref, h_scr,
                *, row_div, row_const, kinds, layer):
    j = pl.program_id(1)
    row = _mod_row(row_div, row_const)

    @pl.when(j == 0)
    def _():
        _modulate_into(h_scr, x_ref, gpre_ref, sc_ref, sh_ref, row)

    def in_kind(kind):
        cond = jnp.bool_(False)
        t = 0
        while t < len(kinds):
            t1 = t
            while t1 + 1 < len(kinds) and kinds[t1 + 1] == kinds[t]:
                t1 += 1
            if kinds[t] == kind:
                cond = cond | ((j >= t) & (j <= t1))
            t = t1 + 1
        return cond

    def slabs(out_ref, fn):
        w = w_ref[...].astype(BF)
        tm = h_scr.shape[0]
        for r0 in range(0, tm, tm // WIN_SLABS):
            sl = slice(r0, r0 + tm // WIN_SLABS)
            out_ref[sl, :] = fn(_dot(h_scr[sl, :], w)).astype(out_ref.dtype)

    def logf(z):
        ll = lb_ref[...]
        e = jnp.exp(ll - jnp.max(ll, axis=0, keepdims=True))
        p = e / jnp.sum(e, axis=0, keepdims=True)
        lb = jnp.zeros_like(p[0:1])
        for r in range(1, layer + 1):
            lb = lb + p[r:r + 1]
        return jnp.log2(jnp.maximum(lb + (1.0 - lb) * jax.nn.sigmoid(z), F_FLOOR))

    for kind, out_ref, fn in (("gelu", p16_ref, _gelu_tanh), ("id", p16_ref, lambda z: z),
                              ("silu", p16_ref, lambda z: z * jax.nn.sigmoid(z)),
                              ("sigmoid", p16_ref, jax.nn.sigmoid), ("logf", pf_ref, logf)):
        if kind in kinds:
            pl.when(in_kind(kind))(functools.partial(slabs, out_ref, fn))


def _win_call(x, mod_l, mod_k, gpre, w_in, lb_logits2, l, *, tm, row_div, row_const, col0, kinds):
    m, d = x.shape
    tn = WIN_TN
    nj = len(kinds)
    is_f = np.array([k == "logf" for k in kinds])
    n16, nf = int((~is_f).sum()), int(is_f.sum())
    c16 = np.maximum(np.cumsum(~is_f) - 1, 0)
    cf = np.maximum(np.cumsum(is_f) - 1, 0)
    f0 = int(np.argmax(is_f))

    def sel(table):
        def f(j):
            out = jnp.int32(int(table[0]))
            for t in range(1, nj):
                if table[t] != table[t - 1]:
                    out = jnp.where(j >= t, jnp.int32(int(table[t])), out)
            return out
        return f

    s16, sf = sel(c16), sel(cf)
    kern = functools.partial(_win_kernel, row_div=row_div, row_const=row_const, kinds=tuple(kinds), layer=l)
    mod_spec = lambda k: pl.BlockSpec((8, d), lambda i, j: (0, k))
    depth = lb_logits2.shape[0]
    return pl.pallas_call(
        kern,
        grid=(m // tm, nj),
        in_specs=[
            pl.BlockSpec((tm, d), lambda i, j: (i, 0)),
            mod_spec(mod_k), mod_spec(mod_k + 1),
            pl.BlockSpec((1, d), lambda i, j: (0, 0)),
            pl.BlockSpec((None, d, tn), lambda i, j: (l, 0, col0 + j)),
            pl.BlockSpec((depth, tn), lambda i, j: (0, jnp.clip(j - f0, 0, nf - 1))),
        ],
        out_specs=[
            pl.BlockSpec((tm, tn), lambda i, j: (i, s16(j))),
            pl.BlockSpec((tm, tn), lambda i, j: (i, sf(j))),
        ],
        out_shape=[jax.ShapeDtypeStruct((m, n16 * tn), BF), jax.ShapeDtypeStruct((m, nf * tn), F32)],
        scratch_shapes=[pltpu.VMEM((tm, d), BF)],
        compiler_params=_params(("parallel", "arbitrary")),
        name="win",
    )(x, mod_l, mod_l, gpre, w_in, lb_logits2)


def _cmlp_kernel(u_ref, v_ref, g_ref, ws_ref, bs_ref, o_ref, *, n_chunks):
    v = v_ref[...].astype(F32)
    vc = v - jnp.mean(v, axis=-1, keepdims=True)
    vn = (vc * lax.rsqrt(jnp.mean(vc * vc, axis=-1, keepdims=True) + EPS) * g_ref[...]).astype(BF)
    dg = vn.shape[1] // GROUPS_A
    for g in range(GROUPS_A):
        w = ws_ref[g].astype(BF)
        bias = bs_ref[g]
        for c in range(n_chunks):
            rows = slice(c * CHUNK_A, (c + 1) * CHUNK_A)
            cols = slice(g * dg, (g + 1) * dg)
            sv = _dot(w, vn[rows, cols]) + bias
            o_ref[rows, cols] = (u_ref[rows, cols].astype(F32) * sv).astype(BF)


def _cmlp_call(p16, chunk_g, w_s, b_s, d_a, *, tm):
    m = p16.shape[0]
    n_chunks = tm // CHUNK_A
    return pl.pallas_call(
        functools.partial(_cmlp_kernel, n_chunks=n_chunks),
        grid=(m // tm,),
        in_specs=[
            pl.BlockSpec((tm, d_a), lambda i: (i, 0)),
            pl.BlockSpec((tm, d_a), lambda i: (i, 1)),
            pl.BlockSpec((1, d_a), lambda i: (0, 0)),
            pl.BlockSpec((GROUPS_A, CHUNK_A, CHUNK_A), lambda i: (0, 0, 0)),
            pl.BlockSpec((GROUPS_A, CHUNK_A, 1), lambda i: (0, 0, 0)),
        ],
        out_specs=pl.BlockSpec((tm, d_a), lambda i: (i, 0)),
        out_shape=jax.ShapeDtypeStruct((m, d_a), BF),
        compiler_params=_params(("parallel",)),
        name="cmlp",
    )(p16, p16, chunk_g, w_s, b_s[..., None])


def _hgrn_consts(fwd):
    c = HGRN_CHUNK
    t = np.arange(c)[:, None]
    u = np.arange(c)[None, :]

    def level_sum(b):
        half = b // 2
        mid = (t // b) * b + half
        upper = (t % b) >= half
        if fwd:
            return np.where(upper, (u >= mid) & (u <= t), (u > t) & (u < mid))
        return np.where(upper, (u >= mid) & (u < t), (u >= t) & (u < mid))

    def level_mask(b):
        half = b // 2
        upper = (t % b) >= half
        same = (t // b) == (u // b)
        if fwd:
            return same & upper & ((u % b) < half)
        return same & ~upper & ((u % b) >= half)

    lsum = np.concatenate([(u <= t) if fwd else (u >= t), level_sum(4), level_sum(8)], 0)
    masks = [t == u] + [level_mask(b) for b in HGRN_LEVELS] + [np.zeros((c, c), bool)]
    return (np.concatenate([lsum, lsum], 1).astype(np.float32),
            np.concatenate(masks, 1).astype(np.float32))


def _to_midpoint(cum, b, fwd):
    c, dk = cum.shape
    half = b // 2
    pieces = []
    for m in range(c // b):
        lo, mid = m * b, m * b + half
        ref = jnp.broadcast_to(cum[mid - 1:mid, :] if fwd else cum[mid:mid + 1, :], (half, dk))
        lower, upper = cum[lo:mid], cum[mid:lo + b]
        pieces += [ref - lower, upper - ref] if fwd else [lower - ref, ref - upper]
    return jnp.concatenate(pieces, axis=0)


def _hgrn_stage1(g, l2_ref):
    g_hi = g.astype(BF)
    g_lo = (g - g_hi.astype(F32)).astype(BF)
    return _dot(l2_ref[...], jnp.concatenate([g_hi, g_lo], axis=0))


def _hgrn_stage2(q, g, sums, mask_ref, fwd, with_out):
    c = HGRN_CHUNK
    dk = g.shape[1]
    f = jnp.exp2(g)
    k = 1.0 - f
    cum = sums[0:c]
    e_cum = jnp.exp2(cum)
    total = cum[c - 1:c] if fwd else cum[0:1]
    kend = (k * jnp.exp2(total - cum)).astype(BF)
    tot = e_cum[c - 1:c] if fwd else e_cum[0:1]
    if not with_out:
        return None, None, kend, tot

    row = lax.broadcasted_iota(jnp.int32, (c, dk), 0)
    e2 = jnp.where((row % 2) == (1 if fwd else 0), f, 1.0)
    es = [e2, jnp.exp2(sums[c:2 * c]), jnp.exp2(sums[2 * c:3 * c])]
    es += [jnp.exp2(_to_midpoint(cum, b, fwd)) for b in HGRN_LEVELS[3:]]
    qf = q.astype(F32)
    units = [(q, k.astype(BF))] + [((qf * e).astype(BF), (k * e).astype(BF)) for e in es]
    z = jnp.zeros((c, dk), BF)
    res = []
    for a in range(0, len(units) - 1, 2):
        (qa, ka), (qb, kb) = units[a], units[a + 1]
        kbd = jnp.concatenate([jnp.concatenate([ka, z], axis=1), jnp.concatenate([z, kb], axis=1)], axis=0)
        res.append(_dot_nt(jnp.concatenate([qa, qb], axis=1), kbd))
    q6, k6 = units[-1]
    res.append(_dot_nt(q6, jnp.concatenate([k6, z], axis=0)))
    p = sum(r * mask_ref[:, i * 2 * c:(i + 1) * 2 * c] for i, r in enumerate(res)).astype(BF)
    return p, (qf * e_cum).astype(BF), kend, tot


def _hgrn_stage3(p, qe, kend, tot, v, st, with_out):
    st_new = st * tot + _dot_tn(v, kend)
    if not with_out:
        return None, st_new
    o = _dot(p, jnp.concatenate([v, v], axis=0)) + _dot_nt(qe, st.astype(BF))
    return o, st_new


def _hgrn_kernel(*refs, n, with_out, with_init, with_final, n_cast):
    it = iter(refs)
    q_ref = next(it) if with_out else None
    v_ref = next(it)
    gs_ref = next(it) if with_out else None
    gf_ref, gb_ref = next(it), next(it)
    gain_ref = next(it) if with_out else None
    sf0_ref, sb0_ref = (next(it), next(it)) if with_init else (None, None)
    l2_refs = (next(it), next(it))
    mask_refs = (next(it), next(it))
    cast_in = [next(it) for _ in range(n_cast)]
    y_ref = next(it) if with_out else None
    sf_ref, sb_ref = (next(it), next(it)) if with_final else (None, None)
    cast_out = [next(it) for _ in range(n_cast)]
    sums_scr, kend_scr, tot_scr, st_scr = next(it), next(it), next(it), next(it)
    p_scr, qe_scr, o_scr = (next(it), next(it), next(it)) if with_out else (None, None, None)

    for src, dst in zip(cast_in, cast_out):
        dst[...] = src[...].astype(BF)

    c = HGRN_CHUNK
    nc = n // c
    dk = v_ref.shape[1]
    g_refs = (gf_ref, gb_ref)

    st_scr[0] = sf0_ref[...] if with_init else jnp.zeros((dk, dk), F32)
    st_scr[1] = sb0_ref[...] if with_init else jnp.zeros((dk, dk), F32)

    def rows_of(d, ci):
        start = ci * c if d == 0 else (nc - 1 - ci) * c
        return pl.ds(start if isinstance(start, int) else pl.multiple_of(start, c), c)

    def step(i, par, do1, do2, do3):
        for d in (0, 1):
            if do3:
                r = rows_of(d, i - 2)
                o, st = _hgrn_stage3(p_scr[d, par] if with_out else None,
                                     qe_scr[d, par] if with_out else None,
                                     kend_scr[d, par], tot_scr[d, par, 0:1, :],
                                     v_ref[r, :], st_scr[d], with_out)
                st_scr[d] = st
                if with_out:
                    o_scr[d, r, :] = o
            if do2:
                r = rows_of(d, i - 1)
                p, qe, kend, tot = _hgrn_stage2(q_ref[r, :] if with_out else None, g_refs[d][r, :],
                                                sums_scr[d, 1 - par], mask_refs[d], d == 0, with_out)
                kend_scr[d, 1 - par] = kend
                tot_scr[d, 1 - par, 0:1, :] = tot
                if with_out:
                    p_scr[d, 1 - par] = p
                    qe_scr[d, 1 - par] = qe
            if do1:
                sums_scr[d, par] = _hgrn_stage1(g_refs[d][rows_of(d, i), :], l2_refs[d])

    step(0, 0, True, False, False)
    step(1, 1, True, True, False)

    n_loop = (nc - 2) // HGRN_UNROLL

    def body(ii, _):
        for u in range(HGRN_UNROLL):
            step(2 + HGRN_UNROLL * ii + u, u % 2, True, True, True)
        return 0

    lax.fori_loop(0, n_loop, body, 0)
    for i in range(2 + HGRN_UNROLL * n_loop, nc):
        step(i, i % 2, True, True, True)
    step(nc, nc % 2, False, True, True)
    step(nc + 1, (nc + 1) % 2, False, False, True)

    if with_final:
        sf_ref[...] = st_scr[0]
        sb_ref[...] = st_scr[1]

    if with_out:
        rows = 256

        def readout(ri, _):
            r = pl.ds(pl.multiple_of(ri * rows, rows), rows)
            o = o_scr[0, r, :] + o_scr[1, r, :]
            o = o * lax.rsqrt(jnp.mean(o * o, axis=-1, keepdims=True) + EPS) * gain_ref[...]
            y_ref[r, :] = (o * gs_ref[r, :].astype(F32)).astype(BF)
            return 0

        lax.fori_loop(0, n // rows, readout, 0, unroll=2)


def _hgrn_call(p16, pf, gain, init, n, cols, *, with_out, with_final, casts=()):
    m = p16.shape[0]
    bsz = m // n
    dk = HGRN_DK
    with_init = init is not None
    cq, cv, cg = cols
    tok = lambda col: pl.BlockSpec((n, dk), lambda b, h: (b, col + h))
    st_spec = pl.BlockSpec((None, None, dk, dk), lambda b, h: (b, h, 0, 0))
    full = lambda a: pl.BlockSpec(a.shape, lambda b, h: (0,) * a.ndim)

    assert (n // HGRN_CHUNK) % 2 == 0 and n % HGRN_CHUNK == 0
    lf, mf = _hgrn_consts(True)
    lbw, mb = _hgrn_consts(False)
    consts = [jnp.asarray(lf, BF), jnp.asarray(lbw, BF), jnp.asarray(mf, F32), jnp.asarray(mb, F32)]

    args, specs = [], []
    if with_out:
        args.append(p16); specs.append(tok(cq))
    args.append(p16); specs.append(tok(cv))
    if with_out:
        args.append(p16); specs.append(tok(cg))
    args += [pf, pf]; specs += [tok(0), tok(HEADS_B)]
    if with_out:
        args.append(gain); specs.append(pl.BlockSpec((1, dk), lambda b, h: (0, h)))
    if with_init:
        args += list(init); specs += [st_spec, st_spec]
    args += consts; specs += [full(a) for a in consts]

    out_shape, out_specs = [], []
    if with_out:
        out_shape.append(jax.ShapeDtypeStruct((m, HEADS_B * dk), BF)); out_specs.append(tok(0))
    if with_final:
        st_shape = jax.ShapeDtypeStruct((bsz, HEADS_B, dk, dk), F32)
        out_shape += [st_shape, st_shape]; out_specs += [st_spec, st_spec]
    steps = bsz * HEADS_B
    for w, layer in casts:
        _, r, cc = w.shape
        rr = r // steps
        assert r % steps == 0 and rr % 16 == 0
        args.append(w)
        specs.append(pl.BlockSpec((None, rr, cc), lambda b, h, layer=layer: (layer, b * HEADS_B + h, 0)))
        out_shape.append(jax.ShapeDtypeStruct((r, cc), BF))
        out_specs.append(pl.BlockSpec((rr, cc), lambda b, h: (b * HEADS_B + h, 0)))
    c = HGRN_CHUNK
    scratch = [pltpu.VMEM((2, 2, lf.shape[0], dk), F32), pltpu.VMEM((2, 2, c, dk), BF),
               pltpu.VMEM((2, 2, 8, dk), F32), pltpu.VMEM((2, dk, dk), F32)]
    if with_out:
        scratch += [pltpu.VMEM((2, 2, c, 2 * c), BF), pltpu.VMEM((2, 2, c, dk), BF),
                    pltpu.VMEM((2, n, dk), F32)]

    kern = functools.partial(_hgrn_kernel, n=n, with_out=with_out, with_init=with_init,
                             with_final=with_final, n_cast=len(casts))
    return pl.pallas_call(
        kern,
        grid=(bsz, HEADS_B),
        in_specs=specs,
        out_specs=out_specs,
        out_shape=out_shape,
        scratch_shapes=scratch,
        compiler_params=_params(("parallel", "parallel")),
        name="hgrn",
    )(*args)


def _merge_kernel(ya_ref, yb_ref, ga_ref, gb_ref, wa_ref, wb_ref, wo_ref, x_ref, gt_ref, gpost_ref,
                  o_ref, rs_scr, *, row_div, row_const, n_j):
    j = pl.program_id(1)
    row = _mod_row(row_div, row_const)

    @pl.when(j == 0)
    def _():
        o_ref[...] = jnp.zeros_like(o_ref)

    wa, wb, wo = wa_ref[...], wb_ref[...], wo_ref[...]
    tm = o_ref.shape[0]
    for r0 in range(0, tm, tm // MERGE_SLABS):
        sl = slice(r0, r0 + tm // MERGE_SLABS)
        ma = _dot(ya_ref[sl, :], wa)
        mb = _dot(yb_ref[sl, :], wb)
        mm = (ga_ref[sl, :].astype(F32) * ma + gb_ref[sl, :].astype(F32) * mb).astype(BF)
        o_ref[sl, :] += _dot(mm, wo)

    @pl.when(j == n_j - 1)
    def _():
        _residual_into(o_ref, rs_scr, x_ref, gpost_ref, gt_ref, row, 1.0)


def _merge_call(x, ya, yb, p16, gate_cols, mod_l, mod_k, gpost, w16, *, tm, row_div, row_const):
    m, d = x.shape
    da, db = ya.shape[1], yb.shape[1]
    tn = MERGE_TN
    n_j = d // tn
    ca, cb = (c // tn for c in gate_cols)
    kern = functools.partial(_merge_kernel, row_div=row_div, row_const=row_const, n_j=n_j)
    return pl.pallas_call(
        kern,
        grid=(m // tm, n_j),
        in_specs=[
            pl.BlockSpec((tm, da), lambda i, j: (i, 0)),
            pl.BlockSpec((tm, db), lambda i, j: (i, 0)),
            pl.BlockSpec((tm, tn), lambda i, j: (i, ca + j)),
            pl.BlockSpec((tm, tn), lambda i, j: (i, cb + j)),
            pl.BlockSpec((da, tn), lambda i, j: (0, j)),
            pl.BlockSpec((db, tn), lambda i, j: (0, j)),
            pl.BlockSpec((tn, d), lambda i, j: (j, 0)),
            pl.BlockSpec((tm, d), lambda i, j: (i, 0)),
            pl.BlockSpec((8, d), lambda i, j: (0, mod_k)),
            pl.BlockSpec((1, d), lambda i, j: (0, 0)),
        ],
        out_specs=pl.BlockSpec((tm, d), lambda i, j: (i, 0)),
        out_shape=jax.ShapeDtypeStruct((m, d), F32),
        scratch_shapes=[pltpu.VMEM((tm, LANES), F32)],
        compiler_params=_params(("parallel", "arbitrary")),
        name="merge",
    )(ya, yb, p16, p16, *w16, x, mod_l, gpost)


def _win_kinds(groups):
    return [kind for kind, width in groups for _ in range(width // WIN_TN)]


def kernel(x, c, ctx, c_ctx, w_mod, b_mod, norm_g, ffn1_w_gu, ffn1_w_down, ffn2_w_gu, ffn2_w_down,
           w_in, chunk_norm_g, w_spatial, b_spatial, lb_logits, hgrn_norm_g, w_up_a, w_up_b, w_out):
    bsz, n, d = x.shape
    n_ctx = ctx.shape[1]
    depth = w_mod.shape[0]
    d_a = chunk_norm_g.shape[1]
    d_b = hgrn_norm_g.shape[1]
    ctx_row = bsz

    cs = jnp.concatenate([c, c_ctx[None, :], jnp.zeros((8 - bsz - 1, d), F32)], axis=0)
    mod = _mod_call(cs, w_mod, b_mod)
    lb2 = lb_logits.reshape(depth, -1)

    xl = x.reshape(bsz * n, d)
    xc = ctx.reshape(bsz * n_ctx, d)
    tm = 1024
    lat = dict(tm=tm, row_div=n // tm, row_const=None)
    cx = dict(tm=bsz * n_ctx, row_div=None, row_const=ctx_row)
    col_q, col_i, col_g = 2 * d_a // 128, (2 * d_a + d_b) // 128, (2 * d_a + 2 * d_b) // 128
    gate_cols = (2 * d_a + 3 * d_b, 2 * d_a + 3 * d_b + d)
    kinds_full = _win_kinds([("gelu", 2 * d_a), ("id", d_b), ("logf", 2 * d_b), ("id", d_b),
                             ("silu", d_b), ("sigmoid", 2 * d)])
    kinds_state = _win_kinds([("logf", 2 * d_b), ("id", d_b)])

    for l in range(depth):
        last = l == depth - 1
        ml = mod[l]
        g = [norm_g[l, k][None, :] for k in range(norm_g.shape[1])]
        ffn1 = functools.partial(_ffn_call, mod_l=ml, mod_k=0, gpre=g[0], gpost=g[1],
                                 w_gu=ffn1_w_gu, w_down=ffn1_w_down, l=l)
        ffn2 = functools.partial(_ffn_call, mod_l=ml, mod_k=6, gpre=g[4], gpost=g[5],
                                 w_gu=ffn2_w_gu, w_down=ffn2_w_down, l=l)

        merge_w = [(w_up_a, l), (w_up_b, l), (w_out, l)]

        xc, w16_1 = ffn1(xc, tf=FFN_TF, emit16=True, **cx)
        if not last:
            pc16, pcf = _win_call(xc, ml, 3, g[2], w_in, lb2, l, col0=0, kinds=kinds_full, **cx)
            ybc, s_f, s_b, *wm16 = _hgrn_call(pc16, pcf, hgrn_norm_g[l][None, :], None, n_ctx,
                                              (col_q, col_i, col_g), with_out=True, with_final=True,
                                              casts=merge_w)
            yac = _cmlp_call(pc16, chunk_norm_g[l][None, :], w_spatial[l], b_spatial[l], d_a, tm=512)
            xc = _merge_call(xc, yac, ybc, pc16, gate_cols, ml, 5, g[3], wm16, **cx)
            xc, w16_2 = ffn2(xc, tf=FFN_TF, emit16=True, **cx)
            ffn2_w = []
        else:
            pc16, pcf = _win_call(xc, ml, 3, g[2], w_in, lb2, l, col0=(2 * d_a + d_b) // WIN_TN,
                                  kinds=kinds_state, **cx)
            s_f, s_b, *wm16 = _hgrn_call(pc16, pcf, None, None, n_ctx, (0, 0, 0), with_out=False,
                                         with_final=True, casts=merge_w)
            ffn2_w = [(ffn2_w_gu, l), (ffn2_w_down, l)]

        xl = ffn1(xl, tf=FFN_TF16, w16=w16_1, **lat)
        p16, pf = _win_call(xl, ml, 3, g[2], w_in, lb2, l, col0=0, kinds=kinds_full, **lat)
        yb, *w16_late = _hgrn_call(p16, pf, hgrn_norm_g[l][None, :], (s_f, s_b), n,
                                   (col_q, col_i, col_g), with_out=True, with_final=False, casts=ffn2_w)
        ya = _cmlp_call(p16, chunk_norm_g[l][None, :], w_spatial[l], b_spatial[l], d_a, tm=512)
        xl = _merge_call(xl, ya, yb, p16, gate_cols, ml, 5, g[3], wm16, **lat)
        xl = ffn2(xl, tf=FFN_TF16, w16=w16_late or w16_2, **lat)
    return xl.reshape(bsz, n, d)
```

```python
import functools

import numpy as np
import jax
import jax.numpy as jnp
from jax import lax
from jax.experimental import pallas as pl
from jax.experimental.pallas import tpu as pltpu

BF = jnp.bfloat16
F32 = jnp.float32

EPS = 1e-6
MACARON = 0.5
F_FLOOR = 1e-30
N_MOD = 9
GROUPS_A = 8
CHUNK_A = 128
HEADS_B = 8
LANES = 128
HGRN_DK = 128
HGRN_CHUNK = 64
HGRN_UNROLL = 6
HGRN_HEADS_PER_STEP = 2
HGRN_LEVELS = (2, 4, 8, 16, 32, 64)

VMEM_LIMIT = 60 * 1024 * 1024


def _dot(a, b):
    return jnp.dot(a, b, preferred_element_type=F32)


def _dot_nt(a, b):
    return lax.dot_general(a, b, (((1,), (1,)), ((), ())), preferred_element_type=F32)


def _dot_tn(a, b):
    return lax.dot_general(a, b, (((0,), (0,)), ((), ())), preferred_element_type=F32)


def _rms(x, g):
    return x * lax.rsqrt(jnp.mean(x * x, axis=-1, keepdims=True) + EPS) * g


def _params(sem):
    return pltpu.CompilerParams(dimension_semantics=sem, vmem_limit_bytes=VMEM_LIMIT)


def _mod_kernel(cs_ref, w_ref, b_ref, o_ref):
    cs = cs_ref[...]
    s = (cs * jax.nn.sigmoid(cs)).astype(BF)
    o_ref[...] = _dot(s, w_ref[...].astype(BF)) + b_ref[...]


def _mod_call(cs, w_mod, b_mod):
    depth, d, n = w_mod.shape
    tn = 1024
    return pl.pallas_call(
        _mod_kernel,
        grid=(depth, n // tn),
        in_specs=[
            pl.BlockSpec((8, d), lambda l, j: (0, 0)),
            pl.BlockSpec((None, d, tn), lambda l, j: (l, 0, j)),
            pl.BlockSpec((None, 1, tn), lambda l, j: (l, 0, j)),
        ],
        out_specs=pl.BlockSpec((None, 8, tn), lambda l, j: (l, 0, j)),
        out_shape=jax.ShapeDtypeStruct((depth, 8, n), F32),
        compiler_params=_params(("parallel", "parallel")),
        name="mod",
    )(cs, w_mod, b_mod.reshape(depth, 1, n))


def _mod_row(row_div, row_const):
    if row_div is None:
        return row_const
    return pl.program_id(0) // row_div


ROW_CHUNK = 64
FFN_TF = 256
FFN_TF16 = 512


def _row_loop(n_rows, fn, unroll=2):
    def body(r, _):
        fn(pl.ds(pl.multiple_of(r * ROW_CHUNK, ROW_CHUNK), ROW_CHUNK))
        return 0

    lax.fori_loop(0, n_rows // ROW_CHUNK, body, 0, unroll=unroll)


def _row_rsqrt_ms(rs_scr, src_ref):
    def rows(sl):
        v = src_ref[sl, :]
        rs = lax.rsqrt(jnp.mean(v * v, axis=-1, keepdims=True) + EPS)
        rs_scr[sl, :] = jnp.broadcast_to(rs, (ROW_CHUNK, rs_scr.shape[1]))

    _row_loop(src_ref.shape[0], rows, unroll=4)


def _lane_tile(v, width):
    return jnp.concatenate([v] * (width // v.shape[1]), axis=1)


def _modulate_into(h_scr, x_ref, gpre_ref, sc_ref, sh_ref, row):
    gain = gpre_ref[...] * (1.0 + sc_ref[pl.ds(row, 1), :])
    shift = sh_ref[pl.ds(row, 1), :]

    def rows(sl):
        x = x_ref[sl, :]
        rs = lax.rsqrt(jnp.mean(x * x, axis=-1, keepdims=True) + EPS)
        h_scr[sl, :] = (x * rs * gain + shift).astype(BF)

    _row_loop(x_ref.shape[0], rows)


def _residual_into(o_ref, rs_scr, x_ref, gpost_ref, gt_ref, row, weight):
    gain = gpost_ref[...] * gt_ref[pl.ds(row, 1), :] * weight
    d = o_ref.shape[1]
    _row_rsqrt_ms(rs_scr, o_ref)

    def rows(sl):
        o_ref[sl, :] = x_ref[sl, :] + o_ref[sl, :] * _lane_tile(rs_scr[sl, :], d) * gain

    _row_loop(o_ref.shape[0], rows)


def _ffn_kernel(x_ref, sh_ref, sc_ref, gt_ref, gpre_ref, gpost_ref, wg_ref, wu_ref, wd_ref, *rest,
                row_div, row_const, n_f, emit16):
    if emit16:
        o_ref, wg16_ref, wu16_ref, wd16_ref, h_scr, rs_scr = rest
    else:
        o_ref, h_scr, rs_scr = rest
    j = pl.program_id(1)
    row = _mod_row(row_div, row_const)

    @pl.when(j == 0)
    def _():
        _modulate_into(h_scr, x_ref, gpre_ref, sc_ref, sh_ref, row)
        o_ref[...] = jnp.zeros_like(o_ref)

    wg, wu, wd = wg_ref[...].astype(BF), wu_ref[...].astype(BF), wd_ref[...].astype(BF)
    if emit16:
        wg16_ref[...] = wg
        wu16_ref[...] = wu
        wd16_ref[...] = wd
    h = h_scr[...]
    a = _dot(h, wg)
    b = _dot(h, wu)
    act = (a * jax.nn.sigmoid(a) * b).astype(BF)
    o_ref[...] += _dot(act, wd)

    @pl.when(j == n_f - 1)
    def _():
        _residual_into(o_ref, rs_scr, x_ref, gpost_ref, gt_ref, row, MACARON)


def _ffn_call(x, mod_l, mod_k, gpre, gpost, w_gu, w_down, l, *, tm, tf, row_div, row_const,
              w16=None, emit16=False):
    m, d = x.shape
    d_ff = w_down.shape[1]
    n_f = d_ff // tf
    kern = functools.partial(_ffn_kernel, row_div=row_div, row_const=row_const, n_f=n_f, emit16=emit16)
    mod_spec = lambda k: pl.BlockSpec((8, d), lambda i, j: (0, k))
    col_tile = pl.BlockSpec((d, tf), lambda i, j: (0, j))
    row_tile = pl.BlockSpec((tf, d), lambda i, j: (j, 0))
    if w16 is None:
        weights = (w_gu, w_gu, w_down)
        w_specs = [pl.BlockSpec((None, d, tf), lambda i, j: (l, 0, j)),
                   pl.BlockSpec((None, d, tf), lambda i, j: (l, 0, n_f + j)),
                   pl.BlockSpec((None, tf, d), lambda i, j: (l, j, 0))]
    elif len(w16) == 2:
        weights = (w16[0], w16[0], w16[1])
        w_specs = [col_tile, pl.BlockSpec((d, tf), lambda i, j: (0, n_f + j)), row_tile]
    else:
        weights = w16
        w_specs = [col_tile, col_tile, row_tile]
    out_specs = [pl.BlockSpec((tm, d), lambda i, j: (i, 0))]
    out_shape = [jax.ShapeDtypeStruct((m, d), F32)]
    if emit16:
        assert m == tm
        out_specs += [col_tile, col_tile, row_tile]
        out_shape += [jax.ShapeDtypeStruct((d, d_ff), BF), jax.ShapeDtypeStruct((d, d_ff), BF),
                      jax.ShapeDtypeStruct((d_ff, d), BF)]
    out = pl.pallas_call(
        kern,
        grid=(m // tm, n_f),
        in_specs=[
            pl.BlockSpec((tm, d), lambda i, j: (i, 0)),
            mod_spec(mod_k), mod_spec(mod_k + 1), mod_spec(mod_k + 2),
            pl.BlockSpec((1, d), lambda i, j: (0, 0)),
            pl.BlockSpec((1, d), lambda i, j: (0, 0)),
        ] + w_specs,
        out_specs=out_specs,
        out_shape=out_shape,
        scratch_shapes=[pltpu.VMEM((tm, d), BF), pltpu.VMEM((tm, LANES), F32)],
        compiler_params=_params(("parallel", "arbitrary")),
        name="ffn",
    )(x, mod_l, mod_l, mod_l, gpre, gpost, *weights)
    return (out[0], tuple(out[1:])) if emit16 else out[0]


WIN_TN = 1024
WIN_SLABS = 4
MERGE_TN = 512
MERGE_SLABS = 2


def _gelu_tanh(z):
    return z * jax.nn.sigmoid((2.0 * 0.7978845608028654) * (z + 0.044715 * (z * z * z)))


def _win_kernel(x_ref, sh_ref, sc_ref, gpre_ref, w_ref, lb_ref, p16_ref, pf_ref, h_scr,
                *, row_div, row_const, kinds, layer):
    j = pl.program_id(1)
    row = _mod_row(row_div, row_const)

    @pl.when(j == 0)
    def _():
        _modulate_into(h_scr, x_ref, gpre_ref, sc_ref, sh_ref, row)

    def in_kind(kind):
        cond = jnp.bool_(False)
        t = 0
        while t < len(kinds):
            t1 = t
            while t1 + 1 < len(kinds) and kinds[t1 + 1] == kinds[t]:
                t1 += 1
            if kinds[t] == kind:
                cond = cond | ((j >= t) & (j <= t1))
            t = t1 + 1
        return cond

    def slabs(out_ref, fn):
        w = w_ref[...].astype(BF)
        tm = h_scr.shape[0]
        for r0 in range(0, tm, tm // WIN_SLABS):
            sl = slice(r0, r0 + tm // WIN_SLABS)
            out_ref[sl, :] = fn(_dot(h_scr[sl, :], w)).astype(out_ref.dtype)

    def logf(z):
        ll = lb_ref[...]
        e = jnp.exp(ll - jnp.max(ll, axis=0, keepdims=True))
        p = e / jnp.sum(e, axis=0, keepdims=True)
        lb = jnp.zeros_like(p[0:1])
        for r in range(1, layer + 1):
            lb = lb + p[r:r + 1]
        return jnp.log2(jnp.maximum(lb + (1.0 - lb) * jax.nn.sigmoid(z), F_FLOOR))

    for kind, out_ref, fn in (("gelu", p16_ref, _gelu_tanh), ("id", p16_ref, lambda z: z),
                              ("silu", p16_ref, lambda z: z * jax.nn.sigmoid(z)),
                              ("sigmoid", p16_ref, jax.nn.sigmoid), ("logf", pf_ref, logf)):
        if kind in kinds:
            pl.when(in_kind(kind))(functools.partial(slabs, out_ref, fn))


def _win_call(x, mod_l, mod_k, gpre, w_in, lb_logits2, l, *, tm, row_div, row_const, col0, kinds):
    m, d = x.shape
    tn = WIN_TN
    nj = len(kinds)
    is_f = np.array([k == "logf" for k in kinds])
    n16, nf = int((~is_f).sum()), int(is_f.sum())
    c16 = np.maximum(np.cumsum(~is_f) - 1, 0)
    cf = np.maximum(np.cumsum(is_f) - 1, 0)
    f0 = int(np.argmax(is_f))

    def sel(table):
        def f(j):
            out = jnp.int32(int(table[0]))
            for t in range(1, nj):
                if table[t] != table[t - 1]:
                    out = jnp.where(j >= t, jnp.int32(int(table[t])), out)
            return out
        return f

    s16, sf = sel(c16), sel(cf)
    kern = functools.partial(_win_kernel, row_div=row_div, row_const=row_const, kinds=tuple(kinds), layer=l)
    mod_spec = lambda k: pl.BlockSpec((8, d), lambda i, j: (0, k))
    depth = lb_logits2.shape[0]
    return pl.pallas_call(
        kern,
        grid=(m // tm, nj),
        in_specs=[
            pl.BlockSpec((tm, d), lambda i, j: (i, 0)),
            mod_spec(mod_k), mod_spec(mod_k + 1),
            pl.BlockSpec((1, d), lambda i, j: (0, 0)),
            pl.BlockSpec((None, d, tn), lambda i, j: (l, 0, col0 + j)),
            pl.BlockSpec((depth, tn), lambda i, j: (0, jnp.clip(j - f0, 0, nf - 1))),
        ],
        out_specs=[
            pl.BlockSpec((tm, tn), lambda i, j: (i, s16(j))),
            pl.BlockSpec((tm, tn), lambda i, j: (i, sf(j))),
        ],
        out_shape=[jax.ShapeDtypeStruct((m, n16 * tn), BF), jax.ShapeDtypeStruct((m, nf * tn), F32)],
        scratch_shapes=[pltpu.VMEM((tm, d), BF)],
        compiler_params=_params(("parallel", "arbitrary")),
        name="win",
    )(x, mod_l, mod_l, gpre, w_in, lb_logits2)


def _cmlp_kernel(u_ref, v_ref, g_ref, ws_ref, bs_ref, o_ref, *, n_chunks):
    v = v_ref[...].astype(F32)
    vc = v - jnp.mean(v, axis=-1, keepdims=True)
    vn = (vc * lax.rsqrt(jnp.mean(vc * vc, axis=-1, keepdims=True) + EPS) * g_ref[...]).astype(BF)
    dg = vn.shape[1] // GROUPS_A
    for g in range(GROUPS_A):
        w = ws_ref[g].astype(BF)
        bias = bs_ref[g]
        for c in range(n_chunks):
            rows = slice(c * CHUNK_A, (c + 1) * CHUNK_A)
            cols = slice(g * dg, (g + 1) * dg)
            sv = _dot(w, vn[rows, cols]) + bias
            o_ref[rows, cols] = (u_ref[rows, cols].astype(F32) * sv).astype(BF)


def _cmlp_call(p16, chunk_g, w_s, b_s, d_a, *, tm):
    m = p16.shape[0]
    n_chunks = tm // CHUNK_A
    return pl.pallas_call(
        functools.partial(_cmlp_kernel, n_chunks=n_chunks),
        grid=(m // tm,),
        in_specs=[
            pl.BlockSpec((tm, d_a), lambda i: (i, 0)),
            pl.BlockSpec((tm, d_a), lambda i: (i, 1)),
            pl.BlockSpec((1, d_a), lambda i: (0, 0)),
            pl.BlockSpec((GROUPS_A, CHUNK_A, CHUNK_A), lambda i: (0, 0, 0)),
            pl.BlockSpec((GROUPS_A, CHUNK_A, 1), lambda i: (0, 0, 0)),
        ],
        out_specs=pl.BlockSpec((tm, d_a), lambda i: (i, 0)),
        out_shape=jax.ShapeDtypeStruct((m, d_a), BF),
        compiler_params=_params(("parallel",)),
        name="cmlp",
    )(p16, p16, chunk_g, w_s, b_s[..., None])


def _hgrn_consts(fwd):
    c = HGRN_CHUNK
    t = np.arange(c)[:, None]
    u = np.arange(c)[None, :]

    def level_sum(b):
        half = b // 2
        mid = (t // b) * b + half
        upper = (t % b) >= half
        if fwd:
            return np.where(upper, (u >= mid) & (u <= t), (u > t) & (u < mid))
        return np.where(upper, (u >= mid) & (u < t), (u >= t) & (u < mid))

    def level_mask(b):
        half = b // 2
        upper = (t % b) >= half
        same = (t // b) == (u // b)
        if fwd:
            return same & upper & ((u % b) < half)
        return same & ~upper & ((u % b) >= half)

    lsum = np.concatenate([(u <= t) if fwd else (u >= t), level_sum(4), level_sum(8)], 0)
    masks = [t == u] + [level_mask(b) for b in HGRN_LEVELS] + [np.zeros((c, c), bool)]
    return (np.concatenate([lsum, lsum], 1).astype(np.float32),
            np.concatenate(masks, 1).astype(np.float32))


def _to_midpoint(cum, b, fwd):
    c, dk = cum.shape
    half = b // 2
    pieces = []
    for m in range(c // b):
        lo, mid = m * b, m * b + half
        ref = jnp.broadcast_to(cum[mid - 1:mid, :] if fwd else cum[mid:mid + 1, :], (half, dk))
        lower, upper = cum[lo:mid], cum[mid:lo + b]
        pieces += [ref - lower, upper - ref] if fwd else [lower - ref, ref - upper]
    return jnp.concatenate(pieces, axis=0)


def _hgrn_stage1(g, l2_ref):
    g_hi = g.astype(BF)
    g_lo = (g - g_hi.astype(F32)).astype(BF)
    return _dot(l2_ref[...], jnp.concatenate([g_hi, g_lo], axis=0))


def _hgrn_stage2(q, g, sums, mask_ref, fwd, with_out):
    c = HGRN_CHUNK
    dk = g.shape[1]
    f = jnp.exp2(g)
    k = 1.0 - f
    cum = sums[0:c]
    e_cum = jnp.exp2(cum)
    total = cum[c - 1:c] if fwd else cum[0:1]
    kend = (k * jnp.exp2(total - cum)).astype(BF)
    tot = e_cum[c - 1:c] if fwd else e_cum[0:1]
    if not with_out:
        return None, None, kend, tot

    row = lax.broadcasted_iota(jnp.int32, (c, dk), 0)
    e2 = jnp.where((row % 2) == (1 if fwd else 0), f, 1.0)
    es = [e2, jnp.exp2(sums[c:2 * c]), jnp.exp2(sums[2 * c:3 * c])]
    es += [jnp.exp2(_to_midpoint(cum, b, fwd)) for b in HGRN_LEVELS[3:]]
    qf = q.astype(F32)
    units = [(q, k.astype(BF))] + [((qf * e).astype(BF), (k * e).astype(BF)) for e in es]
    z = jnp.zeros((c, dk), BF)
    res = []
    for a in range(0, len(units) - 1, 2):
        (qa, ka), (qb, kb) = units[a], units[a + 1]
        kbd = jnp.concatenate([jnp.concatenate([ka, z], axis=1), jnp.concatenate([z, kb], axis=1)], axis=0)
        res.append(_dot_nt(jnp.concatenate([qa, qb], axis=1), kbd))
    q6, k6 = units[-1]
    res.append(_dot_nt(q6, jnp.concatenate([k6, z], axis=0)))
    p = sum(r * mask_ref[:, i * 2 * c:(i + 1) * 2 * c] for i, r in enumerate(res)).astype(BF)
    return p, (qf * e_cum).astype(BF), kend, tot


def _hgrn_stage3(p, qe, kend, tot, v, st, with_out):
    st_new = st * tot + _dot_tn(v, kend)
    if not with_out:
        return None, st_new
    o = _dot(p, jnp.concatenate([v, v], axis=0)) + _dot_nt(qe, st.astype(BF))
    return o, st_new


def _hgrn_kernel(*refs, n, hp, with_out, with_init, with_final, n_cast):
    it = iter(refs)
    q_ref = next(it) if with_out else None
    v_ref = next(it)
    gs_ref = next(it) if with_out else None
    gf_ref, gb_ref = next(it), next(it)
    gain_ref = next(it) if with_out else None
    sf0_ref, sb0_ref = (next(it), next(it)) if with_init else (None, None)
    l2_refs = (next(it), next(it))
    mask_refs = (next(it), next(it))
    cast_in = [next(it) for _ in range(n_cast)]
    y_ref = next(it) if with_out else None
    sf_ref, sb_ref = (next(it), next(it)) if with_final else (None, None)
    cast_out = [next(it) for _ in range(n_cast)]
    sums_scr, kend_scr, tot_scr, st_scr = next(it), next(it), next(it), next(it)
    p_scr, qe_scr, o_scr = (next(it), next(it), next(it)) if with_out else (None, None, None)

    for src, dst in zip(cast_in, cast_out):
        dst[...] = src[...].astype(BF)

    c = HGRN_CHUNK
    nc = n // c
    dk = HGRN_DK
    g_refs = (gf_ref, gb_ref)
    lanes = [slice(hh * dk, (hh + 1) * dk) for hh in range(hp)]

    for hh in range(hp):
        st_scr[hh, 0] = sf0_ref[hh] if with_init else jnp.zeros((dk, dk), F32)
        st_scr[hh, 1] = sb0_ref[hh] if with_init else jnp.zeros((dk, dk), F32)

    def rows_of(d, ci):
        start = ci * c if d == 0 else (nc - 1 - ci) * c
        return pl.ds(start if isinstance(start, int) else pl.multiple_of(start, c), c)

    def step(i, par, do1, do2, do3):
        for hh, d in [(hh, d) for hh in range(hp) for d in (0, 1)]:
            ln = lanes[hh]
            if do3:
                r = rows_of(d, i - 2)
                o, st = _hgrn_stage3(p_scr[hh, d, par] if with_out else None,
                                     qe_scr[hh, d, par] if with_out else None,
                                     kend_scr[hh, d, par], tot_scr[hh, d, par, 0:1, :],
                                     v_ref[r, ln], st_scr[hh, d], with_out)
                st_scr[hh, d] = st
                if with_out:
                    o_scr[d, r, ln] = o
            if do2:
                r = rows_of(d, i - 1)
                p, qe, kend, tot = _hgrn_stage2(q_ref[r, ln] if with_out else None, g_refs[d][r, ln],
                                                sums_scr[hh, d, 1 - par], mask_refs[d], d == 0, with_out)
                kend_scr[hh, d, 1 - par] = kend
                tot_scr[hh, d, 1 - par, 0:1, :] = tot
                if with_out:
                    p_scr[hh, d, 1 - par] = p
                    qe_scr[hh, d, 1 - par] = qe
            if do1:
                sums_scr[hh, d, par] = _hgrn_stage1(g_refs[d][rows_of(d, i), ln], l2_refs[d])

    step(0, 0, True, False, False)
    step(1, 1, True, True, False)

    n_loop = (nc - 2) // HGRN_UNROLL

    def body(ii, _):
        for u in range(HGRN_UNROLL):
            step(2 + HGRN_UNROLL * ii + u, u % 2, True, True, True)
        return 0

    lax.fori_loop(0, n_loop, body, 0)
    for i in range(2 + HGRN_UNROLL * n_loop, nc):
        step(i, i % 2, True, True, True)
    step(nc, nc % 2, False, True, True)
    step(nc + 1, (nc + 1) % 2, False, False, True)

    if with_final:
        for hh in range(hp):
            sf_ref[hh] = st_scr[hh, 0]
            sb_ref[hh] = st_scr[hh, 1]

    if with_out:
        rows = 256

        def readout(ri, _):
            r = pl.ds(pl.multiple_of(ri * rows, rows), rows)
            for ln in lanes:
                o = o_scr[0, r, ln] + o_scr[1, r, ln]
                o = o * lax.rsqrt(jnp.mean(o * o, axis=-1, keepdims=True) + EPS) * gain_ref[:, ln]
                y_ref[r, ln] = (o * gs_ref[r, ln].astype(F32)).astype(BF)
            return 0

        lax.fori_loop(0, n // rows, readout, 0, unroll=2)


def _hgrn_call(p16, pf, gain, init, n, cols, *, with_out, with_final, casts=()):
    m = p16.shape[0]
    bsz = m // n
    dk = HGRN_DK
    with_init = init is not None
    cq, cv, cg = cols
    hp = HGRN_HEADS_PER_STEP
    assert HEADS_B % hp == 0 and all(col % hp == 0 for col in (cq, cv, cg, HEADS_B))
    tok = lambda col: pl.BlockSpec((n, hp * dk), lambda b, h: (b, col // hp + h))
    st_spec = pl.BlockSpec((None, hp, dk, dk), lambda b, h: (b, h, 0, 0))
    full = lambda a: pl.BlockSpec(a.shape, lambda b, h: (0,) * a.ndim)

    assert (n // HGRN_CHUNK) % 2 == 0 and n % HGRN_CHUNK == 0
    lf, mf = _hgrn_consts(True)
    lbw, mb = _hgrn_consts(False)
    consts = [jnp.asarray(lf, BF), jnp.asarray(lbw, BF), jnp.asarray(mf, F32), jnp.asarray(mb, F32)]

    args, specs = [], []
    if with_out:
        args.append(p16); specs.append(tok(cq))
    args.append(p16); specs.append(tok(cv))
    if with_out:
        args.append(p16); specs.append(tok(cg))
    args += [pf, pf]; specs += [tok(0), tok(HEADS_B)]
    if with_out:
        args.append(gain); specs.append(pl.BlockSpec((1, hp * dk), lambda b, h: (0, h)))
    if with_init:
        args += list(init); specs += [st_spec, st_spec]
    args += consts; specs += [full(a) for a in consts]

    out_shape, out_specs = [], []
    if with_out:
        out_shape.append(jax.ShapeDtypeStruct((m, HEADS_B * dk), BF)); out_specs.append(tok(0))
    if with_final:
        st_shape = jax.ShapeDtypeStruct((bsz, HEADS_B, dk, dk), F32)
        out_shape += [st_shape, st_shape]; out_specs += [st_spec, st_spec]
    hsteps = HEADS_B // hp
    steps = bsz * hsteps
    for w, layer in casts:
        _, r, cc = w.shape
        rr = r // steps
        assert r % steps == 0 and rr % 16 == 0
        args.append(w)
        specs.append(pl.BlockSpec((None, rr, cc), lambda b, h, layer=layer: (layer, b * hsteps + h, 0)))
        out_shape.append(jax.ShapeDtypeStruct((r, cc), BF))
        out_specs.append(pl.BlockSpec((rr, cc), lambda b, h: (b * hsteps + h, 0)))
    c = HGRN_CHUNK
    scratch = [pltpu.VMEM((hp, 2, 2, lf.shape[0], dk), F32), pltpu.VMEM((hp, 2, 2, c, dk), BF),
               pltpu.VMEM((hp, 2, 2, 8, dk), F32), pltpu.VMEM((hp, 2, dk, dk), F32)]
    if with_out:
        scratch += [pltpu.VMEM((hp, 2, 2, c, 2 * c), BF), pltpu.VMEM((hp, 2, 2, c, dk), BF),
                    pltpu.VMEM((2, n, hp * dk), F32)]

    kern = functools.partial(_hgrn_kernel, n=n, hp=hp, with_out=with_out, with_init=with_init,
                             with_final=with_final, n_cast=len(casts))
    return pl.pallas_call(
        kern,
        grid=(bsz, hsteps),
        in_specs=specs,
        out_specs=out_specs,
        out_shape=out_shape,
        scratch_shapes=scratch,
        compiler_params=_params(("parallel", "parallel")),
        name="hgrn",
    )(*args)


def _merge_kernel(ya_ref, yb_ref, ga_ref, gb_ref, wa_ref, wb_ref, wo_ref, x_ref, gt_ref, gpost_ref,
                  o_ref, rs_scr, *, row_div, row_const, n_j):
    j = pl.program_id(1)
    row = _mod_row(row_div, row_const)

    @pl.when(j == 0)
    def _():
        o_ref[...] = jnp.zeros_like(o_ref)

    wa, wb, wo = wa_ref[...], wb_ref[...], wo_ref[...]
    tm = o_ref.shape[0]
    for r0 in range(0, tm, tm // MERGE_SLABS):
        sl = slice(r0, r0 + tm // MERGE_SLABS)
        ma = _dot(ya_ref[sl, :], wa)
        mb = _dot(yb_ref[sl, :], wb)
        mm = (ga_ref[sl, :].astype(F32) * ma + gb_ref[sl, :].astype(F32) * mb).astype(BF)
        o_ref[sl, :] += _dot(mm, wo)

    @pl.when(j == n_j - 1)
    def _():
        _residual_into(o_ref, rs_scr, x_ref, gpost_ref, gt_ref, row, 1.0)


def _merge_call(x, ya, yb, p16, gate_cols, mod_l, mod_k, gpost, w16, *, tm, row_div, row_const):
    m, d = x.shape
    da, db = ya.shape[1], yb.shape[1]
    tn = MERGE_TN
    n_j = d // tn
    ca, cb = (c // tn for c in gate_cols)
    kern = functools.partial(_merge_kernel, row_div=row_div, row_const=row_const, n_j=n_j)
    return pl.pallas_call(
        kern,
        grid=(m // tm, n_j),
        in_specs=[
            pl.BlockSpec((tm, da), lambda i, j: (i, 0)),
            pl.BlockSpec((tm, db), lambda i, j: (i, 0)),
            pl.BlockSpec((tm, tn), lambda i, j: (i, ca + j)),
            pl.BlockSpec((tm, tn), lambda i, j: (i, cb + j)),
            pl.BlockSpec((da, tn), lambda i, j: (0, j)),
            pl.BlockSpec((db, tn), lambda i, j: (0, j)),
            pl.BlockSpec((tn, d), lambda i, j: (j, 0)),
            pl.BlockSpec((tm, d), lambda i, j: (i, 0)),
            pl.BlockSpec((8, d), lambda i, j: (0, mod_k)),
            pl.BlockSpec((1, d), lambda i, j: (0, 0)),
        ],
        out_specs=pl.BlockSpec((tm, d), lambda i, j: (i, 0)),
        out_shape=jax.ShapeDtypeStruct((m, d), F32),
        scratch_shapes=[pltpu.VMEM((tm, LANES), F32)],
        compiler_params=_params(("parallel", "arbitrary")),
        name="merge",
    )(ya, yb, p16, p16, *w16, x, mod_l, gpost)


def _win_kinds(groups):
    return [kind for kind, width in groups for _ in range(width // WIN_TN)]


def kernel(x, c, ctx, c_ctx, w_mod, b_mod, norm_g, ffn1_w_gu, ffn1_w_down, ffn2_w_gu, ffn2_w_down,
           w_in, chunk_norm_g, w_spatial, b_spatial, lb_logits, hgrn_norm_g, w_up_a, w_up_b, w_out):
    bsz, n, d = x.shape
    n_ctx = ctx.shape[1]
    depth = w_mod.shape[0]
    d_a = chunk_norm_g.shape[1]
    d_b = hgrn_norm_g.shape[1]
    ctx_row = bsz

    cs = jnp.concatenate([c, c_ctx[None, :], jnp.zeros((8 - bsz - 1, d), F32)], axis=0)
    mod = _mod_call(cs, w_mod, b_mod)
    lb2 = lb_logits.reshape(depth, -1)

    xl = x.reshape(bsz * n, d)
    xc = ctx.reshape(bsz * n_ctx, d)
    tm = 1024
    lat = dict(tm=tm, row_div=n // tm, row_const=None)
    cx = dict(tm=bsz * n_ctx, row_div=None, row_const=ctx_row)
    col_q, col_i, col_g = 2 * d_a // 128, (2 * d_a + d_b) // 128, (2 * d_a + 2 * d_b) // 128
    gate_cols = (2 * d_a + 3 * d_b, 2 * d_a + 3 * d_b + d)
    kinds_full = _win_kinds([("gelu", 2 * d_a), ("id", d_b), ("logf", 2 * d_b), ("id", d_b),
                             ("silu", d_b), ("sigmoid", 2 * d)])
    kinds_state = _win_kinds([("logf", 2 * d_b), ("id", d_b)])

    for l in range(depth):
        last = l == depth - 1
        ml = mod[l]
        g = [norm_g[l, k][None, :] for k in range(norm_g.shape[1])]
        ffn1 = functools.partial(_ffn_call, mod_l=ml, mod_k=0, gpre=g[0], gpost=g[1],
                                 w_gu=ffn1_w_gu, w_down=ffn1_w_down, l=l)
        ffn2 = functools.partial(_ffn_call, mod_l=ml, mod_k=6, gpre=g[4], gpost=g[5],
                                 w_gu=ffn2_w_gu, w_down=ffn2_w_down, l=l)

        merge_w = [(w_up_a, l), (w_up_b, l), (w_out, l)]

        xc, w16_1 = ffn1(xc, tf=FFN_TF, emit16=True, **cx)
        if not last:
            pc16, pcf = _win_call(xc, ml, 3, g[2], w_in, lb2, l, col0=0, kinds=kinds_full, **cx)
            ybc, s_f, s_b, *wm16 = _hgrn_call(pc16, pcf, hgrn_norm_g[l][None, :], None, n_ctx,
                                              (col_q, col_i, col_g), with_out=True, with_final=True,
                                              casts=merge_w)
            yac = _cmlp_call(pc16, chunk_norm_g[l][None, :], w_spatial[l], b_spatial[l], d_a, tm=512)
            xc = _merge_call(xc, yac, ybc, pc16, gate_cols, ml, 5, g[3], wm16, **cx)
            xc, w16_2 = ffn2(xc, tf=FFN_TF, emit16=True, **cx)
            ffn2_w = []
        else:
            pc16, pcf = _win_call(xc, ml, 3, g[2], w_in, lb2, l, col0=(2 * d_a + d_b) // WIN_TN,
                                  kinds=kinds_state, **cx)
            s_f, s_b, *wm16 = _hgrn_call(pc16, pcf, None, None, n_ctx, (0, 0, 0), with_out=False,
                                         with_final=True, casts=merge_w)
            ffn2_w = [(ffn2_w_gu, l), (ffn2_w_down, l)]

        xl = ffn1(xl, tf=FFN_TF16, w16=w16_1, **lat)
        p16, pf = _win_call(xl, ml, 3, g[2], w_in, lb2, l, col0=0, kinds=kinds_full, **lat)
        yb, *w16_late = _hgrn_call(p16, pf, hgrn_norm_g[l][None, :], (s_f, s_b), n,
                                   (col_q, col_i, col_g), with_out=True, with_final=False, casts=ffn2_w)
        ya = _cmlp_call(p16, chunk_norm_g[l][None, :], w_spatial[l], b_spatial[l], d_a, tm=512)
        xl = _merge_call(xl, ya, yb, p16, gate_cols, ml, 5, g[3], wm16, **lat)
        xl = ffn2(xl, tf=FFN_TF16, w16=w16_late or w16_2, **lat)
    return xl.reshape(bsz, n, d)
```

```python
import functools

import numpy as np
import jax
import jax.numpy as jnp
from jax import lax
from jax.experimental import pallas as pl
from jax.experimental.pallas import tpu as pltpu

BF = jnp.bfloat16
F32 = jnp.float32

EPS = 1e-6
MACARON = 0.5
F_FLOOR = 1e-30
GROUPS_A = 8
CHUNK_A = 128
HEADS_B = 8
LANES = 128
HGRN_DK = 128
HGRN_CHUNK = 64
HGRN_UNROLL = 6
HGRN_HEADS_PER_STEP = 2
HGRN_LEVELS = (2, 4, 8, 16, 32, 64)

VMEM_LIMIT = 60 * 1024 * 1024


def _dot(a, b):
    return jnp.dot(a, b, preferred_element_type=F32)


def _dot_nt(a, b):
    return lax.dot_general(a, b, (((1,), (1,)), ((), ())), preferred_element_type=F32)


def _dot_tn(a, b):
    return lax.dot_general(a, b, (((0,), (0,)), ((), ())), preferred_element_type=F32)


def _params(sem):
    return pltpu.CompilerParams(dimension_semantics=sem, vmem_limit_bytes=VMEM_LIMIT)


def _mod_kernel(cs_ref, w_ref, b_ref, o_ref):
    cs = cs_ref[...]
    s = (cs * jax.nn.sigmoid(cs)).astype(BF)
    o_ref[...] = _dot(s, w_ref[...].astype(BF)) + b_ref[...]


def _mod_call(cs, w_mod, b_mod):
    depth, d, n = w_mod.shape
    tn = 1024
    return pl.pallas_call(
        _mod_kernel,
        grid=(depth, n // tn),
        in_specs=[
            pl.BlockSpec((8, d), lambda l, j: (0, 0)),
            pl.BlockSpec((None, d, tn), lambda l, j: (l, 0, j)),
            pl.BlockSpec((None, 1, tn), lambda l, j: (l, 0, j)),
        ],
        out_specs=pl.BlockSpec((None, 8, tn), lambda l, j: (l, 0, j)),
        out_shape=jax.ShapeDtypeStruct((depth, 8, n), F32),
        compiler_params=_params(("parallel", "parallel")),
        name="mod",
    )(cs, w_mod, b_mod.reshape(depth, 1, n))


def _mod_row(row_div, row_const):
    if row_div is None:
        return row_const
    return pl.program_id(0) // row_div


ROW_CHUNK = 64
FFN_TF = 256
FFN_TF16 = 512


def _row_loop(n_rows, fn, unroll=2):
    def body(r, _):
        fn(pl.ds(pl.multiple_of(r * ROW_CHUNK, ROW_CHUNK), ROW_CHUNK))
        return 0

    lax.fori_loop(0, n_rows // ROW_CHUNK, body, 0, unroll=unroll)


def _row_rsqrt_ms(rs_scr, src_ref):
    def rows(sl):
        v = src_ref[sl, :]
        rs = lax.rsqrt(jnp.mean(v * v, axis=-1, keepdims=True) + EPS)
        rs_scr[sl, :] = jnp.broadcast_to(rs, (ROW_CHUNK, rs_scr.shape[1]))

    _row_loop(src_ref.shape[0], rows, unroll=4)


def _lane_tile(v, width):
    return jnp.concatenate([v] * (width // v.shape[1]), axis=1)


def _modulate_into(h_scr, x_ref, gpre_ref, sc_ref, sh_ref, row):
    gain = gpre_ref[...] * (1.0 + sc_ref[pl.ds(row, 1), :])
    shift = sh_ref[pl.ds(row, 1), :]

    def rows(sl):
        x = x_ref[sl, :]
        rs = lax.rsqrt(jnp.mean(x * x, axis=-1, keepdims=True) + EPS)
        h_scr[sl, :] = (x * rs * gain + shift).astype(BF)

    _row_loop(x_ref.shape[0], rows)


def _residual_into(o_ref, rs_scr, x_ref, gpost_ref, gt_ref, row, weight):
    gain = gpost_ref[...] * gt_ref[pl.ds(row, 1), :] * weight
    d = o_ref.shape[1]
    _row_rsqrt_ms(rs_scr, o_ref)

    def rows(sl):
        o_ref[sl, :] = x_ref[sl, :] + o_ref[sl, :] * _lane_tile(rs_scr[sl, :], d) * gain

    _row_loop(o_ref.shape[0], rows)


def _ffn_kernel(x_ref, sh_ref, sc_ref, gt_ref, gpre_ref, gpost_ref, wg_ref, wu_ref, wd_ref, *rest,
                row_div, row_const, n_f, emit16):
    if emit16:
        o_ref, wg16_ref, wu16_ref, wd16_ref, h_scr, rs_scr = rest
    else:
        o_ref, h_scr, rs_scr = rest
    j = pl.program_id(1)
    row = _mod_row(row_div, row_const)

    @pl.when(j == 0)
    def _():
        _modulate_into(h_scr, x_ref, gpre_ref, sc_ref, sh_ref, row)
        o_ref[...] = jnp.zeros_like(o_ref)

    wg, wu, wd = wg_ref[...].astype(BF), wu_ref[...].astype(BF), wd_ref[...].astype(BF)
    if emit16:
        wg16_ref[...] = wg
        wu16_ref[...] = wu
        wd16_ref[...] = wd
    h = h_scr[...]
    a = _dot(h, wg)
    b = _dot(h, wu)
    act = (a * jax.nn.sigmoid(a) * b).astype(BF)
    o_ref[...] += _dot(act, wd)

    @pl.when(j == n_f - 1)
    def _():
        _residual_into(o_ref, rs_scr, x_ref, gpost_ref, gt_ref, row, MACARON)


def _ffn_call(x, mod_l, mod_k, gpre, gpost, w_gu, w_down, l, *, tm, tf, row_div, row_const,
              w16=None, emit16=False):
    m, d = x.shape
    d_ff = w_down.shape[1]
    n_f = d_ff // tf
    kern = functools.partial(_ffn_kernel, row_div=row_div, row_const=row_const, n_f=n_f, emit16=emit16)
    mod_spec = lambda k: pl.BlockSpec((8, d), lambda i, j: (0, k))
    col_tile = pl.BlockSpec((d, tf), lambda i, j: (0, j))
    row_tile = pl.BlockSpec((tf, d), lambda i, j: (j, 0))
    if w16 is None:
        weights = (w_gu, w_gu, w_down)
        w_specs = [pl.BlockSpec((None, d, tf), lambda i, j: (l, 0, j)),
                   pl.BlockSpec((None, d, tf), lambda i, j: (l, 0, n_f + j)),
                   pl.BlockSpec((None, tf, d), lambda i, j: (l, j, 0))]
    elif len(w16) == 2:
        weights = (w16[0], w16[0], w16[1])
        w_specs = [col_tile, pl.BlockSpec((d, tf), lambda i, j: (0, n_f + j)), row_tile]
    else:
        weights = w16
        w_specs = [col_tile, col_tile, row_tile]
    out_specs = [pl.BlockSpec((tm, d), lambda i, j: (i, 0))]
    out_shape = [jax.ShapeDtypeStruct((m, d), F32)]
    if emit16:
        assert m == tm
        out_specs += [col_tile, col_tile, row_tile]
        out_shape += [jax.ShapeDtypeStruct((d, d_ff), BF), jax.ShapeDtypeStruct((d, d_ff), BF),
                      jax.ShapeDtypeStruct((d_ff, d), BF)]
    out = pl.pallas_call(
        kern,
        grid=(m // tm, n_f),
        in_specs=[
            pl.BlockSpec((tm, d), lambda i, j: (i, 0)),
            mod_spec(mod_k), mod_spec(mod_k + 1), mod_spec(mod_k + 2),
            pl.BlockSpec((1, d), lambda i, j: (0, 0)),
            pl.BlockSpec((1, d), lambda i, j: (0, 0)),
        ] + w_specs,
        out_specs=out_specs,
        out_shape=out_shape,
        scratch_shapes=[pltpu.VMEM((tm, d), BF), pltpu.VMEM((tm, LANES), F32)],
        compiler_params=_params(("parallel", "arbitrary")),
        name="ffn",
    )(x, mod_l, mod_l, mod_l, gpre, gpost, *weights)
    return (out[0], tuple(out[1:])) if emit16 else out[0]


WIN_TN = 1024
WIN_SLABS = 4
MERGE_TN = 512
MERGE_SLABS = 2


def _gelu_tanh(z):
    return z * jax.nn.sigmoid((2.0 * 0.7978845608028654) * (z + 0.044715 * (z * z * z)))


def _win_kernel(x_ref, sh_ref, sc_ref, gpre_ref, w_ref, lb_ref, p16_ref, pf_ref, h_scr,
                *, row_div, row_const, kinds, layer):
    j = pl.program_id(1)
    row = _mod_row(row_div, row_const)

    @pl.when(j == 0)
    def _():
        _modulate_into(h_scr, x_ref, gpre_ref, sc_ref, sh_ref, row)

    def in_kind(kind):
        cond = jnp.bool_(False)
        t = 0
        while t < len(kinds):
            t1 = t
            while t1 + 1 < len(kinds) and kinds[t1 + 1] == kinds[t]:
                t1 += 1
            if kinds[t] == kind:
                cond = cond | ((j >= t) & (j <= t1))
            t = t1 + 1
        return cond

    def slabs(out_ref, fn):
        w = w_ref[...].astype(BF)
        tm = h_scr.shape[0]
        for r0 in range(0, tm, tm // WIN_SLABS):
            sl = slice(r0, r0 + tm // WIN_SLABS)
            out_ref[sl, :] = fn(_dot(h_scr[sl, :], w)).astype(out_ref.dtype)

    def logf(z):
        ll = lb_ref[...]
        e = jnp.exp(ll - jnp.max(ll, axis=0, keepdims=True))
        p = e / jnp.sum(e, axis=0, keepdims=True)
        lb = jnp.zeros_like(p[0:1])
        for r in range(1, layer + 1):
            lb = lb + p[r:r + 1]
        return jnp.log2(jnp.maximum(lb + (1.0 - lb) * jax.nn.sigmoid(z), F_FLOOR))

    for kind, out_ref, fn in (("gelu", p16_ref, _gelu_tanh), ("id", p16_ref, lambda z: z),
                              ("silu", p16_ref, lambda z: z * jax.nn.sigmoid(z)),
                              ("sigmoid", p16_ref, jax.nn.sigmoid), ("logf", pf_ref, logf)):
        if kind in kinds:
            pl.when(in_kind(kind))(functools.partial(slabs, out_ref, fn))


def _win_call(x, mod_l, mod_k, gpre, w_in, lb_logits2, l, *, tm, row_div, row_const, col0, kinds):
    m, d = x.shape
    tn = WIN_TN
    nj = len(kinds)
    is_f = np.array([k == "logf" for k in kinds])
    n16, nf = int((~is_f).sum()), int(is_f.sum())
    c16 = np.maximum(np.cumsum(~is_f) - 1, 0)
    cf = np.maximum(np.cumsum(is_f) - 1, 0)
    f0 = int(np.argmax(is_f))

    def sel(table):
        def f(j):
            out = jnp.int32(int(table[0]))
            for t in range(1, nj):
                if table[t] != table[t - 1]:
                    out = jnp.where(j >= t, jnp.int32(int(table[t])), out)
            return out
        return f

    s16, sf = sel(c16), sel(cf)
    kern = functools.partial(_win_kernel, row_div=row_div, row_const=row_const, kinds=tuple(kinds), layer=l)
    mod_spec = lambda k: pl.BlockSpec((8, d), lambda i, j: (0, k))
    depth = lb_logits2.shape[0]
    return pl.pallas_call(
        kern,
        grid=(m // tm, nj),
        in_specs=[
            pl.BlockSpec((tm, d), lambda i, j: (i, 0)),
            mod_spec(mod_k), mod_spec(mod_k + 1),
            pl.BlockSpec((1, d), lambda i, j: (0, 0)),
            pl.BlockSpec((None, d, tn), lambda i, j: (l, 0, col0 + j)),
            pl.BlockSpec((depth, tn), lambda i, j: (0, jnp.clip(j - f0, 0, nf - 1))),
        ],
        out_specs=[
            pl.BlockSpec((tm, tn), lambda i, j: (i, s16(j))),
            pl.BlockSpec((tm, tn), lambda i, j: (i, sf(j))),
        ],
        out_shape=[jax.ShapeDtypeStruct((m, n16 * tn), BF), jax.ShapeDtypeStruct((m, nf * tn), F32)],
        scratch_shapes=[pltpu.VMEM((tm, d), BF)],
        compiler_params=_params(("parallel", "arbitrary")),
        name="win",
    )(x, mod_l, mod_l, gpre, w_in, lb_logits2)


def _cmlp_kernel(u_ref, v_ref, g_ref, ws_ref, bs_ref, o_ref, *, n_chunks):
    v = v_ref[...].astype(F32)
    vc = v - jnp.mean(v, axis=-1, keepdims=True)
    vn = (vc * lax.rsqrt(jnp.mean(vc * vc, axis=-1, keepdims=True) + EPS) * g_ref[...]).astype(BF)
    dg = vn.shape[1] // GROUPS_A
    for g in range(GROUPS_A):
        w = ws_ref[g].astype(BF)
        bias = bs_ref[g]
        for c in range(n_chunks):
            rows = slice(c * CHUNK_A, (c + 1) * CHUNK_A)
            cols = slice(g * dg, (g + 1) * dg)
            sv = _dot(w, vn[rows, cols]) + bias
            o_ref[rows, cols] = (u_ref[rows, cols].astype(F32) * sv).astype(BF)


def _cmlp_call(p16, chunk_g, w_s, b_s, d_a, *, tm):
    m = p16.shape[0]
    n_chunks = tm // CHUNK_A
    return pl.pallas_call(
        functools.partial(_cmlp_kernel, n_chunks=n_chunks),
        grid=(m // tm,),
        in_specs=[
            pl.BlockSpec((tm, d_a), lambda i: (i, 0)),
            pl.BlockSpec((tm, d_a), lambda i: (i, 1)),
            pl.BlockSpec((1, d_a), lambda i: (0, 0)),
            pl.BlockSpec((GROUPS_A, CHUNK_A, CHUNK_A), lambda i: (0, 0, 0)),
            pl.BlockSpec((GROUPS_A, CHUNK_A, 1), lambda i: (0, 0, 0)),
        ],
        out_specs=pl.BlockSpec((tm, d_a), lambda i: (i, 0)),
        out_shape=jax.ShapeDtypeStruct((m, d_a), BF),
        compiler_params=_params(("parallel",)),
        name="cmlp",
    )(p16, p16, chunk_g, w_s, b_s[..., None])


def _hgrn_consts(fwd):
    c = HGRN_CHUNK
    t = np.arange(c)[:, None]
    u = np.arange(c)[None, :]

    def level_sum(b):
        half = b // 2
        mid = (t // b) * b + half
        upper = (t % b) >= half
        if fwd:
            return np.where(upper, (u >= mid) & (u <= t), (u > t) & (u < mid))
        return np.where(upper, (u >= mid) & (u < t), (u >= t) & (u < mid))

    def level_mask(b):
        half = b // 2
        upper = (t % b) >= half
        same = (t // b) == (u // b)
        if fwd:
            return same & upper & ((u % b) < half)
        return same & ~upper & ((u % b) >= half)

    lsum = np.concatenate([(u <= t) if fwd else (u >= t), level_sum(4), level_sum(8)], 0)
    masks = [t == u] + [level_mask(b) for b in HGRN_LEVELS] + [np.zeros((c, c), bool)]
    return (np.concatenate([lsum, lsum], 1).astype(np.float32),
            np.concatenate(masks, 1).astype(np.float32))


def _to_midpoint(cum, b, fwd):
    c, dk = cum.shape
    half = b // 2
    pieces = []
    for m in range(c // b):
        lo, mid = m * b, m * b + half
        ref = jnp.broadcast_to(cum[mid - 1:mid, :] if fwd else cum[mid:mid + 1, :], (half, dk))
        lower, upper = cum[lo:mid], cum[mid:lo + b]
        pieces += [ref - lower, upper - ref] if fwd else [lower - ref, ref - upper]
    return jnp.concatenate(pieces, axis=0)


def _hgrn_stage1(g, l2_ref):
    g_hi = g.astype(BF)
    g_lo = (g - g_hi.astype(F32)).astype(BF)
    return _dot(l2_ref[...], jnp.concatenate([g_hi, g_lo], axis=0))


def _hgrn_stage2(q, g, sums, mask_ref, fwd, with_out):
    c = HGRN_CHUNK
    dk = g.shape[1]
    f = jnp.exp2(g)
    k = 1.0 - f
    cum = sums[0:c]
    e_cum = jnp.exp2(cum)
    total = cum[c - 1:c] if fwd else cum[0:1]
    kend = (k * jnp.exp2(total - cum)).astype(BF)
    tot = e_cum[c - 1:c] if fwd else e_cum[0:1]
    if not with_out:
        return None, None, kend, tot

    row = lax.broadcasted_iota(jnp.int32, (c, dk), 0)
    e2 = jnp.where((row % 2) == (1 if fwd else 0), f, 1.0)
    es = [e2, jnp.exp2(sums[c:2 * c]), jnp.exp2(sums[2 * c:3 * c])]
    es += [jnp.exp2(_to_midpoint(cum, b, fwd)) for b in HGRN_LEVELS[3:]]
    qf = q.astype(F32)
    units = [(q, k.astype(BF))] + [((qf * e).astype(BF), (k * e).astype(BF)) for e in es]
    z = jnp.zeros((c, dk), BF)
    res = []
    for a in range(0, len(units) - 1, 2):
        (qa, ka), (qb, kb) = units[a], units[a + 1]
        kbd = jnp.concatenate([jnp.concatenate([ka, z], axis=1), jnp.concatenate([z, kb], axis=1)], axis=0)
        res.append(_dot_nt(jnp.concatenate([qa, qb], axis=1), kbd))
    q6, k6 = units[-1]
    res.append(_dot_nt(q6, jnp.concatenate([k6, z], axis=0)))
    p = sum(r * mask_ref[:, i * 2 * c:(i + 1) * 2 * c] for i, r in enumerate(res)).astype(BF)
    return p, (qf * e_cum).astype(BF), kend, tot


def _hgrn_stage3(p, qe, kend, tot, v, st, with_out):
    st_new = st * tot + _dot_tn(v, kend)
    if not with_out:
        return None, st_new
    o = _dot(p, jnp.concatenate([v, v], axis=0)) + _dot_nt(qe, st.astype(BF))
    return o, st_new


def _hgrn_kernel(*refs, n, hp, with_out, with_init, with_final, n_cast):
    it = iter(refs)
    q_ref = next(it) if with_out else None
    v_ref = next(it)
    gs_ref = next(it) if with_out else None
    gf_ref, gb_ref = next(it), next(it)
    gain_ref = next(it) if with_out else None
    sf0_ref, sb0_ref = (next(it), next(it)) if with_init else (None, None)
    l2_refs = (next(it), next(it))
    mask_refs = (next(it), next(it))
    cast_in = [next(it) for _ in range(n_cast)]
    y_ref = next(it) if with_out else None
    sf_ref, sb_ref = (next(it), next(it)) if with_final else (None, None)
    cast_out = [next(it) for _ in range(n_cast)]
    sums_scr, kend_scr, tot_scr, st_scr = next(it), next(it), next(it), next(it)
    p_scr, qe_scr, o_scr = (next(it), next(it), next(it)) if with_out else (None, None, None)

    for src, dst in zip(cast_in, cast_out):
        dst[...] = src[...].astype(BF)

    c = HGRN_CHUNK
    nc = n // c
    dk = HGRN_DK
    g_refs = (gf_ref, gb_ref)
    lanes = [slice(hh * dk, (hh + 1) * dk) for hh in range(hp)]

    for hh in range(hp):
        st_scr[hh, 0] = sf0_ref[hh] if with_init else jnp.zeros((dk, dk), F32)
        st_scr[hh, 1] = sb0_ref[hh] if with_init else jnp.zeros((dk, dk), F32)

    def rows_of(d, ci):
        start = ci * c if d == 0 else (nc - 1 - ci) * c
        return pl.ds(start if isinstance(start, int) else pl.multiple_of(start, c), c)

    def step(i, par, do1, do2, do3):
        for hh, d in [(hh, d) for hh in range(hp) for d in (0, 1)]:
            ln = lanes[hh]
            if do3:
                r = rows_of(d, i - 2)
                o, st = _hgrn_stage3(p_scr[hh, d, par] if with_out else None,
                                     qe_scr[hh, d, par] if with_out else None,
                                     kend_scr[hh, d, par], tot_scr[hh, d, par, 0:1, :],
                                     v_ref[r, ln], st_scr[hh, d], with_out)
                st_scr[hh, d] = st
                if with_out:
                    o_scr[d, r, ln] = o
            if do2:
                r = rows_of(d, i - 1)
                p, qe, kend, tot = _hgrn_stage2(q_ref[r, ln] if with_out else None, g_refs[d][r, ln],
                                                sums_scr[hh, d, 1 - par], mask_refs[d], d == 0, with_out)
                kend_scr[hh, d, 1 - par] = kend
                tot_scr[hh, d, 1 - par, 0:1, :] = tot
                if with_out:
                    p_scr[hh, d, 1 - par] = p
                    qe_scr[hh, d, 1 - par] = qe
            if do1:
                sums_scr[hh, d, par] = _hgrn_stage1(g_refs[d][rows_of(d, i), ln], l2_refs[d])

    step(0, 0, True, False, False)
    step(1, 1, True, True, False)

    n_loop = (nc - 2) // HGRN_UNROLL

    def body(ii, _):
        for u in range(HGRN_UNROLL):
            step(2 + HGRN_UNROLL * ii + u, u % 2, True, True, True)
        return 0

    lax.fori_loop(0, n_loop, body, 0)
    for i in range(2 + HGRN_UNROLL * n_loop, nc):
        step(i, i % 2, True, True, True)
    step(nc, nc % 2, False, True, True)
    step(nc + 1, (nc + 1) % 2, False, False, True)

    if with_final:
        for hh in range(hp):
            sf_ref[hh] = st_scr[hh, 0]
            sb_ref[hh] = st_scr[hh, 1]

    if with_out:
        rows = 256

        def readout(ri, _):
            r = pl.ds(pl.multiple_of(ri * rows, rows), rows)
            for ln in lanes:
                o = o_scr[0, r, ln] + o_scr[1, r, ln]
                o = o * lax.rsqrt(jnp.mean(o * o, axis=-1, keepdims=True) + EPS) * gain_ref[:, ln]
                y_ref[r, ln] = (o * gs_ref[r, ln].astype(F32)).astype(BF)
            return 0

        lax.fori_loop(0, n // rows, readout, 0, unroll=2)


def _hgrn_call(p16, pf, gain, init, n, cols, *, with_out, with_final, casts=()):
    m = p16.shape[0]
    bsz = m // n
    dk = HGRN_DK
    with_init = init is not None
    cq, cv, cg = cols
    hp = HGRN_HEADS_PER_STEP
    assert HEADS_B % hp == 0 and all(col % hp == 0 for col in (cq, cv, cg, HEADS_B))
    tok = lambda col: pl.BlockSpec((n, hp * dk), lambda b, h: (b, col // hp + h))
    st_spec = pl.BlockSpec((None, hp, dk, dk), lambda b, h: (b, h, 0, 0))
    full = lambda a: pl.BlockSpec(a.shape, lambda b, h: (0,) * a.ndim)

    assert (n // HGRN_CHUNK) % 2 == 0 and n % HGRN_CHUNK == 0
    lf, mf = _hgrn_consts(True)
    lbw, mb = _hgrn_consts(False)
    consts = [jnp.asarray(lf, BF), jnp.asarray(lbw, BF), jnp.asarray(mf, F32), jnp.asarray(mb, F32)]

    args, specs = [], []
    if with_out:
        args.append(p16); specs.append(tok(cq))
    args.append(p16); specs.append(tok(cv))
    if with_out:
        args.append(p16); specs.append(tok(cg))
    args += [pf, pf]; specs += [tok(0), tok(HEADS_B)]
    if with_out:
        args.append(gain); specs.append(pl.BlockSpec((1, hp * dk), lambda b, h: (0, h)))
    if with_init:
        args += list(init); specs += [st_spec, st_spec]
    args += consts; specs += [full(a) for a in consts]

    out_shape, out_specs = [], []
    if with_out:
        out_shape.append(jax.ShapeDtypeStruct((m, HEADS_B * dk), BF)); out_specs.append(tok(0))
    if with_final:
        st_shape = jax.ShapeDtypeStruct((bsz, HEADS_B, dk, dk), F32)
        out_shape += [st_shape, st_shape]; out_specs += [st_spec, st_spec]
    hsteps = HEADS_B // hp
    steps = bsz * hsteps
    for w, layer in casts:
        _, r, cc = w.shape
        rr = r // steps
        assert r % steps == 0 and rr % 16 == 0
        args.append(w)
        specs.append(pl.BlockSpec((None, rr, cc), lambda b, h, layer=layer: (layer, b * hsteps + h, 0)))
        out_shape.append(jax.ShapeDtypeStruct((r, cc), BF))
        out_specs.append(pl.BlockSpec((rr, cc), lambda b, h: (b * hsteps + h, 0)))
    c = HGRN_CHUNK
    scratch = [pltpu.VMEM((hp, 2, 2, lf.shape[0], dk), F32), pltpu.VMEM((hp, 2, 2, c, dk), BF),
               pltpu.VMEM((hp, 2, 2, 8, dk), F32), pltpu.VMEM((hp, 2, dk, dk), F32)]
    if with_out:
        scratch += [pltpu.VMEM((hp, 2, 2, c, 2 * c), BF), pltpu.VMEM((hp, 2, 2, c, dk), BF),
                    pltpu.VMEM((2, n, hp * dk), F32)]

    kern = functools.partial(_hgrn_kernel, n=n, hp=hp, with_out=with_out, with_init=with_init,
                             with_final=with_final, n_cast=len(casts))
    return pl.pallas_call(
        kern,
        grid=(bsz, hsteps),
        in_specs=specs,
        out_specs=out_specs,
        out_shape=out_shape,
        scratch_shapes=scratch,
        compiler_params=_params(("parallel", "parallel")),
        name="hgrn",
    )(*args)


def _merge_kernel(ya_ref, yb_ref, ga_ref, gb_ref, wa_ref, wb_ref, wo_ref, x_ref, gt_ref, gpost_ref,
                  o_ref, rs_scr, *, row_div, row_const, n_j):
    j = pl.program_id(1)
    row = _mod_row(row_div, row_const)

    @pl.when(j == 0)
    def _():
        o_ref[...] = jnp.zeros_like(o_ref)

    wa, wb, wo = wa_ref[...], wb_ref[...], wo_ref[...]
    tm = o_ref.shape[0]
    for r0 in range(0, tm, tm // MERGE_SLABS):
        sl = slice(r0, r0 + tm // MERGE_SLABS)
        ma = _dot(ya_ref[sl, :], wa)
        mb = _dot(yb_ref[sl, :], wb)
        mm = (ga_ref[sl, :].astype(F32) * ma + gb_ref[sl, :].astype(F32) * mb).astype(BF)
        o_ref[sl, :] += _dot(mm, wo)

    @pl.when(j == n_j - 1)
    def _():
        _residual_into(o_ref, rs_scr, x_ref, gpost_ref, gt_ref, row, 1.0)


def _merge_call(x, ya, yb, p16, gate_cols, mod_l, mod_k, gpost, w16, *, tm, row_div, row_const):
    m, d = x.shape
    da, db = ya.shape[1], yb.shape[1]
    tn = MERGE_TN
    n_j = d // tn
    ca, cb = (c // tn for c in gate_cols)
    kern = functools.partial(_merge_kernel, row_div=row_div, row_const=row_const, n_j=n_j)
    return pl.pallas_call(
        kern,
        grid=(m // tm, n_j),
        in_specs=[
            pl.BlockSpec((tm, da), lambda i, j: (i, 0)),
            pl.BlockSpec((tm, db), lambda i, j: (i, 0)),
            pl.BlockSpec((tm, tn), lambda i, j: (i, ca + j)),
            pl.BlockSpec((tm, tn), lambda i, j: (i, cb + j)),
            pl.BlockSpec((da, tn), lambda i, j: (0, j)),
            pl.BlockSpec((db, tn), lambda i, j: (0, j)),
            pl.BlockSpec((tn, d), lambda i, j: (j, 0)),
            pl.BlockSpec((tm, d), lambda i, j: (i, 0)),
            pl.BlockSpec((8, d), lambda i, j: (0, mod_k)),
            pl.BlockSpec((1, d), lambda i, j: (0, 0)),
        ],
        out_specs=pl.BlockSpec((tm, d), lambda i, j: (i, 0)),
        out_shape=jax.ShapeDtypeStruct((m, d), F32),
        scratch_shapes=[pltpu.VMEM((tm, LANES), F32)],
        compiler_params=_params(("parallel", "arbitrary")),
        name="merge",
    )(ya, yb, p16, p16, *w16, x, mod_l, gpost)


def _win_kinds(groups):
    return [kind for kind, width in groups for _ in range(width // WIN_TN)]


def kernel(x, c, ctx, c_ctx, w_mod, b_mod, norm_g, ffn1_w_gu, ffn1_w_down, ffn2_w_gu, ffn2_w_down,
           w_in, chunk_norm_g, w_spatial, b_spatial, lb_logits, hgrn_norm_g, w_up_a, w_up_b, w_out):
    bsz, n, d = x.shape
    n_ctx = ctx.shape[1]
    depth = w_mod.shape[0]
    d_a = chunk_norm_g.shape[1]
    d_b = hgrn_norm_g.shape[1]
    ctx_row = bsz

    cs = jnp.concatenate([c, c_ctx[None, :], jnp.zeros((8 - bsz - 1, d), F32)], axis=0)
    mod = _mod_call(cs, w_mod, b_mod)
    lb2 = lb_logits.reshape(depth, -1)

    xl = x.reshape(bsz * n, d)
    xc = ctx.reshape(bsz * n_ctx, d)
    tm = 1024
    lat = dict(tm=tm, row_div=n // tm, row_const=None)
    cx = dict(tm=bsz * n_ctx, row_div=None, row_const=ctx_row)
    col_q, col_i, col_g = 2 * d_a // 128, (2 * d_a + d_b) // 128, (2 * d_a + 2 * d_b) // 128
    gate_cols = (2 * d_a + 3 * d_b, 2 * d_a + 3 * d_b + d)
    kinds_full = _win_kinds([("gelu", 2 * d_a), ("id", d_b), ("logf", 2 * d_b), ("id", d_b),
                             ("silu", d_b), ("sigmoid", 2 * d)])
    kinds_state = _win_kinds([("logf", 2 * d_b), ("id", d_b)])

    for l in range(depth):
        last = l == depth - 1
        ml = mod[l]
        g = [norm_g[l, k][None, :] for k in range(norm_g.shape[1])]
        ffn1 = functools.partial(_ffn_call, mod_l=ml, mod_k=0, gpre=g[0], gpost=g[1],
                                 w_gu=ffn1_w_gu, w_down=ffn1_w_down, l=l)
        ffn2 = functools.partial(_ffn_call, mod_l=ml, mod_k=6, gpre=g[4], gpost=g[5],
                                 w_gu=ffn2_w_gu, w_down=ffn2_w_down, l=l)

        merge_w = [(w_up_a, l), (w_up_b, l), (w_out, l)]

        xc, w16_1 = ffn1(xc, tf=FFN_TF, emit16=True, **cx)
        if not last:
            pc16, pcf = _win_call(xc, ml, 3, g[2], w_in, lb2, l, col0=0, kinds=kinds_full, **cx)
            ybc, s_f, s_b, *wm16 = _hgrn_call(pc16, pcf, hgrn_norm_g[l][None, :], None, n_ctx,
                                              (col_q, col_i, col_g), with_out=True, with_final=True,
                                              casts=merge_w)
            yac = _cmlp_call(pc16, chunk_norm_g[l][None, :], w_spatial[l], b_spatial[l], d_a, tm=512)
            xc = _merge_call(xc, yac, ybc, pc16, gate_cols, ml, 5, g[3], wm16, **cx)
            xc, w16_2 = ffn2(xc, tf=FFN_TF, emit16=True, **cx)
            ffn2_w = []
        else:
            pc16, pcf = _win_call(xc, ml, 3, g[2], w_in, lb2, l, col0=(2 * d_a + d_b) // WIN_TN,
                                  kinds=kinds_state, **cx)
            s_f, s_b, *wm16 = _hgrn_call(pc16, pcf, None, None, n_ctx, (0, 0, 0), with_out=False,
                                         with_final=True, casts=merge_w)
            ffn2_w = [(ffn2_w_gu, l), (ffn2_w_down, l)]

        xl = ffn1(xl, tf=FFN_TF16, w16=w16_1, **lat)
        p16, pf = _win_call(xl, ml, 3, g[2], w_in, lb2, l, col0=0, kinds=kinds_full, **lat)
        yb, *w16_late = _hgrn_call(p16, pf, hgrn_norm_g[l][None, :], (s_f, s_b), n,
                                   (col_q, col_i, col_g), with_out=True, with_final=False, casts=ffn2_w)
        ya = _cmlp_call(p16, chunk_norm_g[l][None, :], w_spatial[l], b_spatial[l], d_a, tm=512)
        xl = _merge_call(xl, ya, yb, p16, gate_cols, ml, 5, g[3], wm16, **lat)
        xl = ffn2(xl, tf=FFN_TF16, w16=w16_late or w16_2, **lat)
    return xl.reshape(bsz, n, d)
```

```python
import functools

import numpy as np
import jax
import jax.numpy as jnp
from jax import lax
from jax.experimental import pallas as pl
from jax.experimental.pallas import tpu as pltpu

BF = jnp.bfloat16
F32 = jnp.float32

EPS = 1e-6
MACARON = 0.5
F_FLOOR = 1e-30
GROUPS_A = 8
CHUNK_A = 128
HEADS_B = 8
LANES = 128
HGRN_DK = 128
HGRN_CHUNK = 64
HGRN_UNROLL = 6
HGRN_HEADS_PER_STEP = 2
HGRN_LEVELS = (2, 4, 8, 16, 32, 64)

VMEM_LIMIT = 60 * 1024 * 1024


def _dot(a, b):
    return jnp.dot(a, b, preferred_element_type=F32)


def _dot_nt(a, b):
    return lax.dot_general(a, b, (((1,), (1,)), ((), ())), preferred_element_type=F32)


def _dot_tn(a, b):
    return lax.dot_general(a, b, (((0,), (0,)), ((), ())), preferred_element_type=F32)


def _params(sem):
    return pltpu.CompilerParams(dimension_semantics=sem, vmem_limit_bytes=VMEM_LIMIT)


def _mod_kernel(cs_ref, w_ref, b_ref, o_ref):
    cs = cs_ref[...]
    s = (cs * jax.nn.sigmoid(cs)).astype(BF)
    o_ref[...] = _dot(s, w_ref[...].astype(BF)) + b_ref[...]


def _mod_call(cs, w_mod, b_mod):
    depth, d, n = w_mod.shape
    tn = 1024
    return pl.pallas_call(
        _mod_kernel,
        grid=(depth, n // tn),
        in_specs=[
            pl.BlockSpec((8, d), lambda l, j: (0, 0)),
            pl.BlockSpec((None, d, tn), lambda l, j: (l, 0, j)),
            pl.BlockSpec((None, 1, tn), lambda l, j: (l, 0, j)),
        ],
        out_specs=pl.BlockSpec((None, 8, tn), lambda l, j: (l, 0, j)),
        out_shape=jax.ShapeDtypeStruct((depth, 8, n), F32),
        compiler_params=_params(("parallel", "parallel")),
        name="mod",
    )(cs, w_mod, b_mod.reshape(depth, 1, n))


def _mod_row(row_div, row_const):
    if row_div is None:
        return row_const
    return pl.program_id(0) // row_div


ROW_CHUNK = 64
FFN_TF = 256
FFN_TF16 = 512


def _row_loop(n_rows, fn, unroll=2):
    def body(r, _):
        fn(pl.ds(pl.multiple_of(r * ROW_CHUNK, ROW_CHUNK), ROW_CHUNK))
        return 0

    lax.fori_loop(0, n_rows // ROW_CHUNK, body, 0, unroll=unroll)


def _row_rsqrt_ms(rs_scr, src_ref):
    def rows(sl):
        v = src_ref[sl, :]
        rs = lax.rsqrt(jnp.mean(v * v, axis=-1, keepdims=True) + EPS)
        rs_scr[sl, :] = jnp.broadcast_to(rs, (ROW_CHUNK, rs_scr.shape[1]))

    _row_loop(src_ref.shape[0], rows, unroll=4)


def _lane_tile(v, width):
    return jnp.concatenate([v] * (width // v.shape[1]), axis=1)


def _modulate_into(h_scr, x_ref, gpre_ref, sc_ref, sh_ref, row):
    gain = gpre_ref[...] * (1.0 + sc_ref[pl.ds(row, 1), :])
    shift = sh_ref[pl.ds(row, 1), :]

    def rows(sl):
        x = x_ref[sl, :]
        rs = lax.rsqrt(jnp.mean(x * x, axis=-1, keepdims=True) + EPS)
        h_scr[sl, :] = (x * rs * gain + shift).astype(BF)

    _row_loop(x_ref.shape[0], rows)


def _residual_into(o_ref, rs_scr, x_ref, gpost_ref, gt_ref, row, weight):
    gain = gpost_ref[...] * gt_ref[pl.ds(row, 1), :] * weight
    d = o_ref.shape[1]
    _row_rsqrt_ms(rs_scr, o_ref)

    def rows(sl):
        o_ref[sl, :] = x_ref[sl, :] + o_ref[sl, :] * _lane_tile(rs_scr[sl, :], d) * gain

    _row_loop(o_ref.shape[0], rows)


def _ffn_kernel(x_ref, sh_ref, sc_ref, gt_ref, gpre_ref, gpost_ref, wg_ref, wu_ref, wd_ref, *rest,
                row_div, row_const, n_f, emit16):
    if emit16:
        o_ref, wg16_ref, wu16_ref, wd16_ref, h_scr, rs_scr = rest
    else:
        o_ref, h_scr, rs_scr = rest
    j = pl.program_id(1)
    row = _mod_row(row_div, row_const)

    @pl.when(j == 0)
    def _():
        _modulate_into(h_scr, x_ref, gpre_ref, sc_ref, sh_ref, row)
        o_ref[...] = jnp.zeros_like(o_ref)

    wg, wu, wd = wg_ref[...].astype(BF), wu_ref[...].astype(BF), wd_ref[...].astype(BF)
    if emit16:
        wg16_ref[...] = wg
        wu16_ref[...] = wu
        wd16_ref[...] = wd
    h = h_scr[...]
    a = _dot(h, wg)
    b = _dot(h, wu)
    act = (a * jax.nn.sigmoid(a) * b).astype(BF)
    o_ref[...] += _dot(act, wd)

    @pl.when(j == n_f - 1)
    def _():
        _residual_into(o_ref, rs_scr, x_ref, gpost_ref, gt_ref, row, MACARON)


def _ffn_call(x, mod_l, mod_k, gpre, gpost, w_gu, w_down, l, *, tm, tf, row_div, row_const,
              w16=None, emit16=False):
    m, d = x.shape
    d_ff = w_down.shape[1]
    n_f = d_ff // tf
    kern = functools.partial(_ffn_kernel, row_div=row_div, row_const=row_const, n_f=n_f, emit16=emit16)
    mod_spec = lambda k: pl.BlockSpec((8, d), lambda i, j: (0, k))
    col_tile = pl.BlockSpec((d, tf), lambda i, j: (0, j))
    row_tile = pl.BlockSpec((tf, d), lambda i, j: (j, 0))
    if w16 is None:
        weights = (w_gu, w_gu, w_down)
        w_specs = [pl.BlockSpec((None, d, tf), lambda i, j: (l, 0, j)),
                   pl.BlockSpec((None, d, tf), lambda i, j: (l, 0, n_f + j)),
                   pl.BlockSpec((None, tf, d), lambda i, j: (l, j, 0))]
    elif len(w16) == 2:
        weights = (w16[0], w16[0], w16[1])
        w_specs = [col_tile, pl.BlockSpec((d, tf), lambda i, j: (0, n_f + j)), row_tile]
    else:
        weights = w16
        w_specs = [col_tile, col_tile, row_tile]
    out_specs = [pl.BlockSpec((tm, d), lambda i, j: (i, 0))]
    out_shape = [jax.ShapeDtypeStruct((m, d), F32)]
    if emit16:
        assert m == tm
        out_specs += [col_tile, col_tile, row_tile]
        out_shape += [jax.ShapeDtypeStruct((d, d_ff), BF), jax.ShapeDtypeStruct((d, d_ff), BF),
                      jax.ShapeDtypeStruct((d_ff, d), BF)]
    out = pl.pallas_call(
        kern,
        grid=(m // tm, n_f),
        in_specs=[
            pl.BlockSpec((tm, d), lambda i, j: (i, 0)),
            mod_spec(mod_k), mod_spec(mod_k + 1), mod_spec(mod_k + 2),
            pl.BlockSpec((1, d), lambda i, j: (0, 0)),
            pl.BlockSpec((1, d), lambda i, j: (0, 0)),
        ] + w_specs,
        out_specs=out_specs,
        out_shape=out_shape,
        scratch_shapes=[pltpu.VMEM((tm, d), BF), pltpu.VMEM((tm, LANES), F32)],
        compiler_params=_params(("parallel", "arbitrary")),
        name="ffn",
    )(x, mod_l, mod_l, mod_l, gpre, gpost, *weights)
    return (out[0], tuple(out[1:])) if emit16 else out[0]


WIN_TN = 1024
WIN_SLABS = 8
MERGE_TN = 512
MERGE_SLABS = 2


def _gelu_tanh(z):
    return z * jax.nn.sigmoid((2.0 * 0.7978845608028654) * (z + 0.044715 * (z * z * z)))


def _win_kernel(x_ref, sh_ref, sc_ref, gpre_ref, w_ref, lb_ref, p16_ref, pf_ref, h_scr,
                *, row_div, row_const, kinds, layer):
    j = pl.program_id(1)
    row = _mod_row(row_div, row_const)

    @pl.when(j == 0)
    def _():
        _modulate_into(h_scr, x_ref, gpre_ref, sc_ref, sh_ref, row)

    def in_kind(kind):
        cond = jnp.bool_(False)
        t = 0
        while t < len(kinds):
            t1 = t
            while t1 + 1 < len(kinds) and kinds[t1 + 1] == kinds[t]:
                t1 += 1
            if kinds[t] == kind:
                cond = cond | ((j >= t) & (j <= t1))
            t = t1 + 1
        return cond

    def slabs(out_ref, fn):
        w = w_ref[...].astype(BF)
        tm = h_scr.shape[0]
        for r0 in range(0, tm, tm // WIN_SLABS):
            sl = slice(r0, r0 + tm // WIN_SLABS)
            out_ref[sl, :] = fn(_dot(h_scr[sl, :], w)).astype(out_ref.dtype)

    def logf(z):
        ll = lb_ref[...]
        e = jnp.exp(ll - jnp.max(ll, axis=0, keepdims=True))
        p = e / jnp.sum(e, axis=0, keepdims=True)
        lb = jnp.zeros_like(p[0:1])
        for r in range(1, layer + 1):
            lb = lb + p[r:r + 1]
        return jnp.log2(jnp.maximum(lb + (1.0 - lb) * jax.nn.sigmoid(z), F_FLOOR))

    for kind, out_ref, fn in (("gelu", p16_ref, _gelu_tanh), ("id", p16_ref, lambda z: z),
                              ("silu", p16_ref, lambda z: z * jax.nn.sigmoid(z)),
                              ("sigmoid", p16_ref, jax.nn.sigmoid), ("logf", pf_ref, logf)):
        if kind in kinds:
            pl.when(in_kind(kind))(functools.partial(slabs, out_ref, fn))


def _win_call(x, mod_l, mod_k, gpre, w_in, lb_logits2, l, *, tm, row_div, row_const, col0, kinds):
    m, d = x.shape
    tn = WIN_TN
    nj = len(kinds)
    is_f = np.array([k == "logf" for k in kinds])
    n16, nf = int((~is_f).sum()), int(is_f.sum())
    c16 = np.maximum(np.cumsum(~is_f) - 1, 0)
    cf = np.maximum(np.cumsum(is_f) - 1, 0)
    f0 = int(np.argmax(is_f))

    def sel(table):
        def f(j):
            out = jnp.int32(int(table[0]))
            for t in range(1, nj):
                if table[t] != table[t - 1]:
                    out = jnp.where(j >= t, jnp.int32(int(table[t])), out)
            return out
        return f

    s16, sf = sel(c16), sel(cf)
    kern = functools.partial(_win_kernel, row_div=row_div, row_const=row_const, kinds=tuple(kinds), layer=l)
    mod_spec = lambda k: pl.BlockSpec((8, d), lambda i, j: (0, k))
    depth = lb_logits2.shape[0]
    return pl.pallas_call(
        kern,
        grid=(m // tm, nj),
        in_specs=[
            pl.BlockSpec((tm, d), lambda i, j: (i, 0)),
            mod_spec(mod_k), mod_spec(mod_k + 1),
            pl.BlockSpec((1, d), lambda i, j: (0, 0)),
            pl.BlockSpec((None, d, tn), lambda i, j: (l, 0, col0 + j)),
            pl.BlockSpec((depth, tn), lambda i, j: (0, jnp.clip(j - f0, 0, nf - 1))),
        ],
        out_specs=[
            pl.BlockSpec((tm, tn), lambda i, j: (i, s16(j))),
            pl.BlockSpec((tm, tn), lambda i, j: (i, sf(j))),
        ],
        out_shape=[jax.ShapeDtypeStruct((m, n16 * tn), BF), jax.ShapeDtypeStruct((m, nf * tn), F32)],
        scratch_shapes=[pltpu.VMEM((tm, d), BF)],
        compiler_params=_params(("parallel", "arbitrary")),
        name="win",
    )(x, mod_l, mod_l, gpre, w_in, lb_logits2)


def _cmlp_kernel(u_ref, v_ref, g_ref, ws_ref, bs_ref, o_ref, *, n_chunks):
    v = v_ref[...].astype(F32)
    vc = v - jnp.mean(v, axis=-1, keepdims=True)
    vn = (vc * lax.rsqrt(jnp.mean(vc * vc, axis=-1, keepdims=True) + EPS) * g_ref[...]).astype(BF)
    dg = vn.shape[1] // GROUPS_A
    for g in range(GROUPS_A):
        w = ws_ref[g].astype(BF)
        bias = bs_ref[g]
        for c in range(n_chunks):
            rows = slice(c * CHUNK_A, (c + 1) * CHUNK_A)
            cols = slice(g * dg, (g + 1) * dg)
            sv = _dot(w, vn[rows, cols]) + bias
            o_ref[rows, cols] = (u_ref[rows, cols].astype(F32) * sv).astype(BF)


def _cmlp_call(p16, chunk_g, w_s, b_s, d_a, *, tm):
    m = p16.shape[0]
    n_chunks = tm // CHUNK_A
    return pl.pallas_call(
        functools.partial(_cmlp_kernel, n_chunks=n_chunks),
        grid=(m // tm,),
        in_specs=[
            pl.BlockSpec((tm, d_a), lambda i: (i, 0)),
            pl.BlockSpec((tm, d_a), lambda i: (i, 1)),
            pl.BlockSpec((1, d_a), lambda i: (0, 0)),
            pl.BlockSpec((GROUPS_A, CHUNK_A, CHUNK_A), lambda i: (0, 0, 0)),
            pl.BlockSpec((GROUPS_A, CHUNK_A, 1), lambda i: (0, 0, 0)),
        ],
        out_specs=pl.BlockSpec((tm, d_a), lambda i: (i, 0)),
        out_shape=jax.ShapeDtypeStruct((m, d_a), BF),
        compiler_params=_params(("parallel",)),
        name="cmlp",
    )(p16, p16, chunk_g, w_s, b_s[..., None])


def _hgrn_consts(fwd):
    c = HGRN_CHUNK
    t = np.arange(c)[:, None]
    u = np.arange(c)[None, :]

    def level_sum(b):
        half = b // 2
        mid = (t // b) * b + half
        upper = (t % b) >= half
        if fwd:
            return np.where(upper, (u >= mid) & (u <= t), (u > t) & (u < mid))
        return np.where(upper, (u >= mid) & (u < t), (u >= t) & (u < mid))

    def level_mask(b):
        half = b // 2
        upper = (t % b) >= half
        same = (t // b) == (u // b)
        if fwd:
            return same & upper & ((u % b) < half)
        return same & ~upper & ((u % b) >= half)

    lsum = np.concatenate([(u <= t) if fwd else (u >= t), level_sum(4), level_sum(8)], 0)
    masks = [t == u] + [level_mask(b) for b in HGRN_LEVELS] + [np.zeros((c, c), bool)]
    return (np.concatenate([lsum, lsum], 1).astype(np.float32),
            np.concatenate(masks, 1).astype(np.float32))


def _to_midpoint(cum, b, fwd):
    c, dk = cum.shape
    half = b // 2
    pieces = []
    for m in range(c // b):
        lo, mid = m * b, m * b + half
        ref = jnp.broadcast_to(cum[mid - 1:mid, :] if fwd else cum[mid:mid + 1, :], (half, dk))
        lower, upper = cum[lo:mid], cum[mid:lo + b]
        pieces += [ref - lower, upper - ref] if fwd else [lower - ref, ref - upper]
    return jnp.concatenate(pieces, axis=0)


def _hgrn_stage1(g, l2_ref):
    g_hi = g.astype(BF)
    g_lo = (g - g_hi.astype(F32)).astype(BF)
    return _dot(l2_ref[...], jnp.concatenate([g_hi, g_lo], axis=0))


def _hgrn_stage2(q, g, sums, mask_ref, fwd, with_out):
    c = HGRN_CHUNK
    dk = g.shape[1]
    f = jnp.exp2(g)
    k = 1.0 - f
    cum = sums[0:c]
    e_cum = jnp.exp2(cum)
    total = cum[c - 1:c] if fwd else cum[0:1]
    kend = (k * jnp.exp2(total - cum)).astype(BF)
    tot = e_cum[c - 1:c] if fwd else e_cum[0:1]
    if not with_out:
        return None, None, kend, tot

    row = lax.broadcasted_iota(jnp.int32, (c, dk), 0)
    e2 = jnp.where((row % 2) == (1 if fwd else 0), f, 1.0)
    es = [e2, jnp.exp2(sums[c:2 * c]), jnp.exp2(sums[2 * c:3 * c])]
    es += [jnp.exp2(_to_midpoint(cum, b, fwd)) for b in HGRN_LEVELS[3:]]
    qf = q.astype(F32)
    units = [(q, k.astype(BF))] + [((qf * e).astype(BF), (k * e).astype(BF)) for e in es]
    z = jnp.zeros((c, dk), BF)
    res = []
    for a in range(0, len(units) - 1, 2):
        (qa, ka), (qb, kb) = units[a], units[a + 1]
        kbd = jnp.concatenate([jnp.concatenate([ka, z], axis=1), jnp.concatenate([z, kb], axis=1)], axis=0)
        res.append(_dot_nt(jnp.concatenate([qa, qb], axis=1), kbd))
    q6, k6 = units[-1]
    res.append(_dot_nt(q6, jnp.concatenate([k6, z], axis=0)))
    p = sum(r * mask_ref[:, i * 2 * c:(i + 1) * 2 * c] for i, r in enumerate(res)).astype(BF)
    return p, (qf * e_cum).astype(BF), kend, tot


def _hgrn_stage3(p, qe, kend, tot, v, st, with_out):
    st_new = st * tot + _dot_tn(v, kend)
    if not with_out:
        return None, st_new
    o = _dot(p, jnp.concatenate([v, v], axis=0)) + _dot_nt(qe, st.astype(BF))
    return o, st_new


def _hgrn_kernel(*refs, n, hp, with_out, with_init, with_final, n_cast):
    it = iter(refs)
    q_ref = next(it) if with_out else None
    v_ref = next(it)
    gs_ref = next(it) if with_out else None
    gf_ref, gb_ref = next(it), next(it)
    gain_ref = next(it) if with_out else None
    sf0_ref, sb0_ref = (next(it), next(it)) if with_init else (None, None)
    l2_refs = (next(it), next(it))
    mask_refs = (next(it), next(it))
    cast_in = [next(it) for _ in range(n_cast)]
    y_ref = next(it) if with_out else None
    sf_ref, sb_ref = (next(it), next(it)) if with_final else (None, None)
    cast_out = [next(it) for _ in range(n_cast)]
    sums_scr, kend_scr, tot_scr, st_scr = next(it), next(it), next(it), next(it)
    p_scr, qe_scr, o_scr = (next(it), next(it), next(it)) if with_out else (None, None, None)

    for src, dst in zip(cast_in, cast_out):
        dst[...] = src[...].astype(BF)

    c = HGRN_CHUNK
    nc = n // c
    dk = HGRN_DK
    g_refs = (gf_ref, gb_ref)
    lanes = [slice(hh * dk, (hh + 1) * dk) for hh in range(hp)]

    for hh in range(hp):
        st_scr[hh, 0] = sf0_ref[hh] if with_init else jnp.zeros((dk, dk), F32)
        st_scr[hh, 1] = sb0_ref[hh] if with_init else jnp.zeros((dk, dk), F32)

    def rows_of(d, ci):
        start = ci * c if d == 0 else (nc - 1 - ci) * c
        return pl.ds(start if isinstance(start, int) else pl.multiple_of(start, c), c)

    def step(i, par, do1, do2, do3):
        for hh, d in [(hh, d) for hh in range(hp) for d in (0, 1)]:
            ln = lanes[hh]
            if do3:
                r = rows_of(d, i - 2)
                o, st = _hgrn_stage3(p_scr[hh, d, par] if with_out else None,
                                     qe_scr[hh, d, par] if with_out else None,
                                     kend_scr[hh, d, par], tot_scr[hh, d, par, 0:1, :],
                                     v_ref[r, ln], st_scr[hh, d], with_out)
                st_scr[hh, d] = st
                if with_out:
                    o_scr[d, r, ln] = o
            if do2:
                r = rows_of(d, i - 1)
                p, qe, kend, tot = _hgrn_stage2(q_ref[r, ln] if with_out else None, g_refs[d][r, ln],
                                                sums_scr[hh, d, 1 - par], mask_refs[d], d == 0, with_out)
                kend_scr[hh, d, 1 - par] = kend
                tot_scr[hh, d, 1 - par, 0:1, :] = tot
                if with_out:
                    p_scr[hh, d, 1 - par] = p
                    qe_scr[hh, d, 1 - par] = qe
            if do1:
                sums_scr[hh, d, par] = _hgrn_stage1(g_refs[d][rows_of(d, i), ln], l2_refs[d])

    step(0, 0, True, False, False)
    step(1, 1, True, True, False)

    n_loop = (nc - 2) // HGRN_UNROLL

    def body(ii, _):
        for u in range(HGRN_UNROLL):
            step(2 + HGRN_UNROLL * ii + u, u % 2, True, True, True)
        return 0

    lax.fori_loop(0, n_loop, body, 0)
    for i in range(2 + HGRN_UNROLL * n_loop, nc):
        step(i, i % 2, True, True, True)
    step(nc, nc % 2, False, True, True)
    step(nc + 1, (nc + 1) % 2, False, False, True)

    if with_final:
        for hh in range(hp):
            sf_ref[hh] = st_scr[hh, 0]
            sb_ref[hh] = st_scr[hh, 1]

    if with_out:
        rows = 256

        def readout(ri, _):
            r = pl.ds(pl.multiple_of(ri * rows, rows), rows)
            for ln in lanes:
                o = o_scr[0, r, ln] + o_scr[1, r, ln]
                o = o * lax.rsqrt(jnp.mean(o * o, axis=-1, keepdims=True) + EPS) * gain_ref[:, ln]
                y_ref[r, ln] = (o * gs_ref[r, ln].astype(F32)).astype(BF)
            return 0

        lax.fori_loop(0, n // rows, readout, 0, unroll=2)


def _hgrn_call(p16, pf, gain, init, n, cols, *, with_out, with_final, casts=()):
    m = p16.shape[0]
    bsz = m // n
    dk = HGRN_DK
    with_init = init is not None
    cq, cv, cg = cols
    hp = HGRN_HEADS_PER_STEP
    assert HEADS_B % hp == 0 and all(col % hp == 0 for col in (cq, cv, cg, HEADS_B))
    tok = lambda col: pl.BlockSpec((n, hp * dk), lambda b, h: (b, col // hp + h))
    st_spec = pl.BlockSpec((None, hp, dk, dk), lambda b, h: (b, h, 0, 0))
    full = lambda a: pl.BlockSpec(a.shape, lambda b, h: (0,) * a.ndim)

    assert (n // HGRN_CHUNK) % 2 == 0 and n % HGRN_CHUNK == 0
    lf, mf = _hgrn_consts(True)
    lbw, mb = _hgrn_consts(False)
    consts = [jnp.asarray(lf, BF), jnp.asarray(lbw, BF), jnp.asarray(mf, F32), jnp.asarray(mb, F32)]

    args, specs = [], []
    if with_out:
        args.append(p16); specs.append(tok(cq))
    args.append(p16); specs.append(tok(cv))
    if with_out:
        args.append(p16); specs.append(tok(cg))
    args += [pf, pf]; specs += [tok(0), tok(HEADS_B)]
    if with_out:
        args.append(gain); specs.append(pl.BlockSpec((1, hp * dk), lambda b, h: (0, h)))
    if with_init:
        args += list(init); specs += [st_spec, st_spec]
    args += consts; specs += [full(a) for a in consts]

    out_shape, out_specs = [], []
    if with_out:
        out_shape.append(jax.ShapeDtypeStruct((m, HEADS_B * dk), BF)); out_specs.append(tok(0))
    if with_final:
        st_shape = jax.ShapeDtypeStruct((bsz, HEADS_B, dk, dk), F32)
        out_shape += [st_shape, st_shape]; out_specs += [st_spec, st_spec]
    hsteps = HEADS_B // hp
    steps = bsz * hsteps
    for w, layer in casts:
        _, r, cc = w.shape
        rr = r // steps
        assert r % steps == 0 and rr % 16 == 0
        args.append(w)
        specs.append(pl.BlockSpec((None, rr, cc), lambda b, h, layer=layer: (layer, b * hsteps + h, 0)))
        out_shape.append(jax.ShapeDtypeStruct((r, cc), BF))
        out_specs.append(pl.BlockSpec((rr, cc), lambda b, h: (b * hsteps + h, 0)))
    c = HGRN_CHUNK
    scratch = [pltpu.VMEM((hp, 2, 2, lf.shape[0], dk), F32), pltpu.VMEM((hp, 2, 2, c, dk), BF),
               pltpu.VMEM((hp, 2, 2, 8, dk), F32), pltpu.VMEM((hp, 2, dk, dk), F32)]
    if with_out:
        scratch += [pltpu.VMEM((hp, 2, 2, c, 2 * c), BF), pltpu.VMEM((hp, 2, 2, c, dk), BF),
                    pltpu.VMEM((2, n, hp * dk), F32)]

    kern = functools.partial(_hgrn_kernel, n=n, hp=hp, with_out=with_out, with_init=with_init,
                             with_final=with_final, n_cast=len(casts))
    return pl.pallas_call(
        kern,
        grid=(bsz, hsteps),
        in_specs=specs,
        out_specs=out_specs,
        out_shape=out_shape,
        scratch_shapes=scratch,
        compiler_params=_params(("parallel", "parallel")),
        name="hgrn",
    )(*args)


def _merge_kernel(ya_ref, yb_ref, ga_ref, gb_ref, wa_ref, wb_ref, wo_ref, x_ref, gt_ref, gpost_ref,
                  o_ref, rs_scr, *, row_div, row_const, n_j):
    j = pl.program_id(1)
    row = _mod_row(row_div, row_const)

    @pl.when(j == 0)
    def _():
        o_ref[...] = jnp.zeros_like(o_ref)

    wa, wb, wo = wa_ref[...], wb_ref[...], wo_ref[...]
    tm = o_ref.shape[0]
    for r0 in range(0, tm, tm // MERGE_SLABS):
        sl = slice(r0, r0 + tm // MERGE_SLABS)
        ma = _dot(ya_ref[sl, :], wa)
        mb = _dot(yb_ref[sl, :], wb)
        mm = (ga_ref[sl, :].astype(F32) * ma + gb_ref[sl, :].astype(F32) * mb).astype(BF)
        o_ref[sl, :] += _dot(mm, wo)

    @pl.when(j == n_j - 1)
    def _():
        _residual_into(o_ref, rs_scr, x_ref, gpost_ref, gt_ref, row, 1.0)


def _merge_call(x, ya, yb, p16, gate_cols, mod_l, mod_k, gpost, w16, *, tm, row_div, row_const):
    m, d = x.shape
    da, db = ya.shape[1], yb.shape[1]
    tn = MERGE_TN
    n_j = d // tn
    ca, cb = (c // tn for c in gate_cols)
    kern = functools.partial(_merge_kernel, row_div=row_div, row_const=row_const, n_j=n_j)
    return pl.pallas_call(
        kern,
        grid=(m // tm, n_j),
        in_specs=[
            pl.BlockSpec((tm, da), lambda i, j: (i, 0)),
            pl.BlockSpec((tm, db), lambda i, j: (i, 0)),
            pl.BlockSpec((tm, tn), lambda i, j: (i, ca + j)),
            pl.BlockSpec((tm, tn), lambda i, j: (i, cb + j)),
            pl.BlockSpec((da, tn), lambda i, j: (0, j)),
            pl.BlockSpec((db, tn), lambda i, j: (0, j)),
            pl.BlockSpec((tn, d), lambda i, j: (j, 0)),
            pl.BlockSpec((tm, d), lambda i, j: (i, 0)),
            pl.BlockSpec((8, d), lambda i, j: (0, mod_k)),
            pl.BlockSpec((1, d), lambda i, j: (0, 0)),
        ],
        out_specs=pl.BlockSpec((tm, d), lambda i, j: (i, 0)),
        out_shape=jax.ShapeDtypeStruct((m, d), F32),
        scratch_shapes=[pltpu.VMEM((tm, LANES), F32)],
        compiler_params=_params(("parallel", "arbitrary")),
        name="merge",
    )(ya, yb, p16, p16, *w16, x, mod_l, gpost)


def _win_kinds(groups):
    return [kind for kind, width in groups for _ in range(width // WIN_TN)]


def kernel(x, c, ctx, c_ctx, w_mod, b_mod, norm_g, ffn1_w_gu, ffn1_w_down, ffn2_w_gu, ffn2_w_down,
           w_in, chunk_norm_g, w_spatial, b_spatial, lb_logits, hgrn_norm_g, w_up_a, w_up_b, w_out):
    bsz, n, d = x.shape
    n_ctx = ctx.shape[1]
    depth = w_mod.shape[0]
    d_a = chunk_norm_g.shape[1]
    d_b = hgrn_norm_g.shape[1]
    ctx_row = bsz

    cs = jnp.concatenate([c, c_ctx[None, :], jnp.zeros((8 - bsz - 1, d), F32)], axis=0)
    mod = _mod_call(cs, w_mod, b_mod)
    lb2 = lb_logits.reshape(depth, -1)

    xl = x.reshape(bsz * n, d)
    xc = ctx.reshape(bsz * n_ctx, d)
    tm = 1024
    lat = dict(tm=tm, row_div=n // tm, row_const=None)
    cx = dict(tm=bsz * n_ctx, row_div=None, row_const=ctx_row)
    col_q, col_i, col_g = 2 * d_a // 128, (2 * d_a + d_b) // 128, (2 * d_a + 2 * d_b) // 128
    gate_cols = (2 * d_a + 3 * d_b, 2 * d_a + 3 * d_b + d)
    kinds_full = _win_kinds([("gelu", 2 * d_a), ("id", d_b), ("logf", 2 * d_b), ("id", d_b),
                             ("silu", d_b), ("sigmoid", 2 * d)])
    kinds_state = _win_kinds([("logf", 2 * d_b), ("id", d_b)])

    for l in range(depth):
        last = l == depth - 1
        ml = mod[l]
        g = [norm_g[l, k][None, :] for k in range(norm_g.shape[1])]
        ffn1 = functools.partial(_ffn_call, mod_l=ml, mod_k=0, gpre=g[0], gpost=g[1],
                                 w_gu=ffn1_w_gu, w_down=ffn1_w_down, l=l)
        ffn2 = functools.partial(_ffn_call, mod_l=ml, mod_k=6, gpre=g[4], gpost=g[5],
                                 w_gu=ffn2_w_gu, w_down=ffn2_w_down, l=l)

        merge_w = [(w_up_a, l), (w_up_b, l), (w_out, l)]

        xc, w16_1 = ffn1(xc, tf=FFN_TF, emit16=True, **cx)
        if not last:
            pc16, pcf = _win_call(xc, ml, 3, g[2], w_in, lb2, l, col0=0, kinds=kinds_full, **cx)
            ybc, s_f, s_b, *wm16 = _hgrn_call(pc16, pcf, hgrn_norm_g[l][None, :], None, n_ctx,
                                              (col_q, col_i, col_g), with_out=True, with_final=True,
                                              casts=merge_w)
            yac = _cmlp_call(pc16, chunk_norm_g[l][None, :], w_spatial[l], b_spatial[l], d_a, tm=512)
            xc = _merge_call(xc, yac, ybc, pc16, gate_cols, ml, 5, g[3], wm16, **cx)
            xc, w16_2 = ffn2(xc, tf=FFN_TF, emit16=True, **cx)
            ffn2_w = []
        else:
            pc16, pcf = _win_call(xc, ml, 3, g[2], w_in, lb2, l, col0=(2 * d_a + d_b) // WIN_TN,
                                  kinds=kinds_state, **cx)
            s_f, s_b, *wm16 = _hgrn_call(pc16, pcf, None, None, n_ctx, (0, 0, 0), with_out=False,
                                         with_final=True, casts=merge_w)
            ffn2_w = [(ffn2_w_gu, l), (ffn2_w_down, l)]

        xl = ffn1(xl, tf=FFN_TF16, w16=w16_1, **lat)
        p16, pf = _win_call(xl, ml, 3, g[2], w_in, lb2, l, col0=0, kinds=kinds_full, **lat)
        yb, *w16_late = _hgrn_call(p16, pf, hgrn_norm_g[l][None, :], (s_f, s_b), n,
                                   (col_q, col_i, col_g), with_out=True, with_final=False, casts=ffn2_w)
        ya = _cmlp_call(p16, chunk_norm_g[l][None, :], w_spatial[l], b_spatial[l], d_a, tm=512)
        xl = _merge_call(xl, ya, yb, p16, gate_cols, ml, 5, g[3], wm16, **lat)
        xl = ffn2(xl, tf=FFN_TF16, w16=w16_late or w16_2, **lat)
    return xl.reshape(bsz, n, d)
```
